```python
import math
import jax, jax.numpy as jnp
from jax import lax
import numpy as np

D_MODEL = 1024
BATCH = 2
SEQ = 8192
DEPTH = 1

ATTN_WIDTH = D_MODEL // 2
POOL_WIDTH = D_MODEL - ATTN_WIDTH
HEAD_DIM = 64
N_HEADS = ATTN_WIDTH // HEAD_DIM
MOBA_BLOCK = 256
MOBA_TOPK = 3
Q_CHUNK = 32
POOL_WINDOWS = (2, 4, 8, 16)
N_POOL_GROUPS = len(POOL_WINDOWS)
POOL_GROUP = POOL_WIDTH // N_POOL_GROUPS
ROPE_THETA = 10000.0
EPS = 1e-6
IN_WIDTH = 4 * ATTN_WIDTH + 2 * POOL_WIDTH

kernel_name = "hymba_moba_pool_hybrid"


def rmsnorm(x, g):
    xf = x.astype(jnp.float32)
    y = xf * lax.rsqrt(jnp.mean(xf * xf, axis=-1, keepdims=True) + EPS)
    return (y * g.astype(jnp.float32)).astype(x.dtype)


def rope(x, pos):
    dh = x.shape[-1]
    inv_freq = 1.0 / (ROPE_THETA ** (jnp.arange(0, dh, 2, dtype=jnp.float32) / dh))
    ang = pos[:, None] * inv_freq[None, :]
    cos, sin = jnp.cos(ang), jnp.sin(ang)
    xf = x.astype(jnp.float32)
    x1, x2 = xf[..., : dh // 2], xf[..., dh // 2:]
    out = jnp.concatenate([x1 * cos - x2 * sin, x2 * cos + x1 * sin], axis=-1)
    return out.astype(x.dtype)


def moba_attention(q, k, v):
    B, H, S, dh = q.shape
    nb = -(-S // MOBA_BLOCK)
    s_pad = nb * MOBA_BLOCK
    padw = ((0, 0), (0, 0), (0, s_pad - S), (0, 0))
    q, k, v = jnp.pad(q, padw), jnp.pad(k, padw), jnp.pad(v, padw)
    kb = k.reshape(B, H, nb, MOBA_BLOCK, dh)
    vb = v.reshape(B, H, nb, MOBA_BLOCK, dh)
    k_mean = jnp.mean(kb.astype(jnp.float32), axis=3)
    n_chunks = s_pad // Q_CHUNK
    qc = jnp.transpose(q.reshape(B, H, n_chunks, Q_CHUNK, dh), (2, 0, 1, 3, 4))
    k_sel = min(MOBA_TOPK, nb)
    scale = HEAD_DIM ** -0.5
    bi = jnp.arange(B)[:, None, None, None]
    hi = jnp.arange(H)[None, :, None, None]
    blk_ids = jnp.arange(nb)
    own_off = jnp.arange(MOBA_BLOCK)

    def step(args):
        c, qi = args
        q_start = c * Q_CHUNK
        qpos = q_start + jnp.arange(Q_CHUNK)
        blk = q_start // MOBA_BLOCK
        qf = qi.astype(jnp.float32)
        gate = jnp.einsum('bhqd,bhnd->bhqn', qf, k_mean)
        gate = jnp.where(blk_ids[None, None, None, :] < blk, gate, -jnp.inf)
        _, top_idx = lax.top_k(gate, k_sel)
        sel_valid = top_idx < blk
        ks = kb[bi, hi, top_idx]
        vs = vb[bi, hi, top_idx]
        s_sel = jnp.einsum('bhqd,bhqkjd->bhqkj', qf, ks.astype(jnp.float32)) * scale
        s_sel = jnp.where(sel_valid[..., None], s_sel, -jnp.inf)
        s_sel = s_sel.reshape(B, H, Q_CHUNK, k_sel * MOBA_BLOCK)
        k_own = lax.dynamic_index_in_dim(kb, blk, axis=2, keepdims=False)
        v_own = lax.dynamic_index_in_dim(vb, blk, axis=2, keepdims=False)
        kpos = blk * MOBA_BLOCK + own_off
        s_own = jnp.einsum('bhqd,bhjd->bhqj', qf, k_own.astype(jnp.float32)) * scale
        s_own = jnp.where(kpos[None, :] <= qpos[:, None], s_own, -jnp.inf)
        p = jax.nn.softmax(jnp.concatenate([s_sel, s_own], axis=-1), axis=-1)
        p_sel = p[..., : k_sel * MOBA_BLOCK].reshape(B, H, Q_CHUNK, k_sel, MOBA_BLOCK)
        p_own = p[..., k_sel * MOBA_BLOCK:]
        out = (jnp.einsum('bhqkj,bhqkjd->bhqd', p_sel, vs.astype(jnp.float32))
               + jnp.einsum('bhqj,bhjd->bhqd', p_own, v_own.astype(jnp.float32)))
        return out.astype(qi.dtype)

    outs = lax.map(step, (jnp.arange(n_chunks), qc))
    out = jnp.transpose(outs, (1, 2, 0, 3, 4)).reshape(B, H, s_pad, dh)
    return out[:, :, :S]


def causal_pool_minus_identity(u, window):
    S = u.shape[1]
    cs = jnp.cumsum(u.astype(jnp.float32), axis=1)
    cs_prev = jnp.pad(cs, ((0, 0), (window, 0), (0, 0)))[:, :S]
    count = jnp.minimum(jnp.arange(1, S + 1), window).astype(jnp.float32)[None, :, None]
    mean = (cs - cs_prev) / count
    return (mean - u.astype(jnp.float32)).astype(u.dtype)


def multiscale_pool(u, w_pool, b_pool, pool_scale):
    B, S, _ = u.shape
    ug = u.reshape(B, S, N_POOL_GROUPS, POOL_GROUP)
    pooled = jnp.stack([causal_pool_minus_identity(ug[:, :, g], POOL_WINDOWS[g])
                        for g in range(N_POOL_GROUPS)], axis=2)
    y = jnp.einsum('bsgc,gcd->bsgd', pooled, w_pool) + b_pool
    return y.reshape(B, S, POOL_WIDTH) * pool_scale


def setup_inputs(seed: int = 0) -> dict:
    key = jax.random.key(seed)
    ks = jax.random.split(key, 8)
    f32 = jnp.float32
    x = jax.random.normal(ks[0], (BATCH, SEQ, D_MODEL), f32)
    norm_gain = 1.0 + 0.02 * jax.random.normal(ks[1], (DEPTH, D_MODEL), f32)
    w_in = jax.random.normal(ks[2], (DEPTH, D_MODEL, IN_WIDTH), f32) * D_MODEL ** -0.5
    w_pool = jax.random.normal(ks[3], (DEPTH, N_POOL_GROUPS, POOL_GROUP, POOL_GROUP), f32) * POOL_GROUP ** -0.5
    b_pool = 0.02 * jax.random.normal(ks[4], (DEPTH, N_POOL_GROUPS, POOL_GROUP), f32)
    pool_scale = 1.0 + 0.02 * jax.random.normal(ks[5], (DEPTH, POOL_WIDTH), f32)
    w_out = jax.random.normal(ks[6], (DEPTH, D_MODEL, D_MODEL), f32) * D_MODEL ** -0.5
    final_gain = 1.0 + 0.02 * jax.random.normal(ks[7], (D_MODEL,), f32)
    return {"x": x, "norm_gain": norm_gain, "w_in": w_in, "w_pool": w_pool,
            "b_pool": b_pool, "pool_scale": pool_scale, "w_out": w_out,
            "final_gain": final_gain}


def reference(x, norm_gain, w_in, w_pool, b_pool, pool_scale, w_out, final_gain):
    B, S, _ = x.shape
    pos = jnp.arange(S, dtype=jnp.float32)
    splits = [ATTN_WIDTH, 2 * ATTN_WIDTH, 3 * ATTN_WIDTH, 4 * ATTN_WIDTH,
              4 * ATTN_WIDTH + POOL_WIDTH]
    for l in range(DEPTH):
        h = rmsnorm(x, norm_gain[l])
        proj = h @ w_in[l]
        q, k, v, g_attn, u, g_pool = jnp.split(proj, splits, axis=-1)

        def heads(t):
            return jnp.transpose(t.reshape(B, S, N_HEADS, HEAD_DIM), (0, 2, 1, 3))

        qh, kh, vh = rope(heads(q), pos), rope(heads(k), pos), heads(v)
        attn = moba_attention(qh, kh, vh)
        attn = jnp.transpose(attn, (0, 2, 1, 3)).reshape(B, S, ATTN_WIDTH)
        attn = attn * jax.nn.silu(g_attn)

        pool = multiscale_pool(u, w_pool[l], b_pool[l], pool_scale[l])
        pool = pool * jax.nn.silu(g_pool)

        mixed = jnp.concatenate([attn, pool], axis=-1)
        x = x + mixed @ w_out[l]
    return rmsnorm(x, final_gain)
```

```python
import functools

import jax
import jax.numpy as jnp
from jax import lax
from jax.experimental import pallas as pl
from jax.experimental.pallas import tpu as pltpu

D_MODEL = 1024
ATTN_WIDTH = D_MODEL // 2
POOL_WIDTH = D_MODEL - ATTN_WIDTH
HEAD_DIM = 64
MOBA_BLOCK = 256
MOBA_TOPK = 3
POOL_WINDOWS = (2, 4, 8, 16)
POOL_GROUP = POOL_WIDTH // len(POOL_WINDOWS)
ROPE_THETA = 10000.0
EPS = 1e-6
IN_WIDTH = 4 * ATTN_WIDTH + 2 * POOL_WIDTH

LANES = 128
HEADS_PER_TILE = LANES // HEAD_DIM
ROW_TILE = 512
POOL_HALO = 16
MASK_BIAS = -1e30
VMEM_LIMIT = 56 * 1024 * 1024

_NT = (((1,), (1,)), ((), ()))


def _silu(t):
    return t * (1.0 / (1.0 + jnp.exp(-t)))


def _rope(t, cos, sin_signed, first_half):
    swapped = jnp.where(first_half, pltpu.roll(t, LANES - HEAD_DIM // 2, axis=1),
                        pltpu.roll(t, HEAD_DIM // 2, axis=1))
    return t * cos + swapped * sin_signed


def _in_proj_kernel(x_ref, gain_ref, w_ref, cos_ref, sin_ref,
                    q_ref, k_ref, v_ref, ga_ref, u_ref, gp_ref, kmean_ref):
    x = x_ref[...]
    inv = lax.rsqrt(jnp.mean(x * x, axis=-1, keepdims=True) + EPS)
    h = (x * inv * gain_ref[...]).astype(jnp.bfloat16)

    def proj(c):
        return jnp.dot(h, w_ref[:, c * ATTN_WIDTH:(c + 1) * ATTN_WIDTH],
                       preferred_element_type=jnp.float32)

    cos = cos_ref[...]
    sin = sin_ref[...]
    lane = lax.broadcasted_iota(jnp.int32, (ROW_TILE, LANES), 1)
    first_half = (lane % HEAD_DIM) < (HEAD_DIM // 2)

    q = proj(0)
    k = proj(1)
    for t in range(ATTN_WIDTH // LANES):
        sl = slice(t * LANES, (t + 1) * LANES)
        q_ref[:, sl] = (_rope(q[:, sl], cos, sin, first_half) * HEAD_DIM ** -0.5).astype(q_ref.dtype)
        kr = _rope(k[:, sl], cos, sin, first_half)
        k_ref[:, sl] = kr.astype(k_ref.dtype)
        for b in range(ROW_TILE // MOBA_BLOCK):
            kmean_ref[b, :, sl] = jnp.sum(kr[b * MOBA_BLOCK:(b + 1) * MOBA_BLOCK], axis=0,
                                          keepdims=True) * (1.0 / MOBA_BLOCK)
    v_ref[...] = proj(2).astype(v_ref.dtype)
    ga_ref[...] = proj(3)
    u_ref[...] = proj(4)
    gp_ref[...] = proj(5)


def _in_proj(x2, gain, w_bf16, cos_t, sin_t, seq):
    rows = x2.shape[0]
    n_tiles = rows // ROW_TILE
    seq_tiles = seq // ROW_TILE
    row_spec = lambda w: pl.BlockSpec((ROW_TILE, w), lambda i: (i, 0))
    tab_spec = pl.BlockSpec((ROW_TILE, LANES), lambda i: (i % seq_tiles, 0))
    blocks_per_tile = ROW_TILE // MOBA_BLOCK
    out_shape = (
        jax.ShapeDtypeStruct((rows, ATTN_WIDTH), jnp.bfloat16),
        jax.ShapeDtypeStruct((rows, ATTN_WIDTH), jnp.bfloat16),
        jax.ShapeDtypeStruct((rows, ATTN_WIDTH), jnp.bfloat16),
        jax.ShapeDtypeStruct((rows, ATTN_WIDTH), jnp.float32),
        jax.ShapeDtypeStruct((rows, POOL_WIDTH), jnp.float32),
        jax.ShapeDtypeStruct((rows, POOL_WIDTH), jnp.float32),
        jax.ShapeDtypeStruct((rows // MOBA_BLOCK, 1, ATTN_WIDTH), jnp.float32),
    )
    return pl.pallas_call(
        _in_proj_kernel,
        grid=(n_tiles,),
        in_specs=[
            row_spec(D_MODEL),
            pl.BlockSpec((1, D_MODEL), lambda i: (0, 0)),
            pl.BlockSpec((D_MODEL, IN_WIDTH), lambda i: (0, 0)),
            tab_spec, tab_spec,
        ],
        out_specs=(
            row_spec(ATTN_WIDTH), row_spec(ATTN_WIDTH), row_spec(ATTN_WIDTH),
            row_spec(ATTN_WIDTH), row_spec(POOL_WIDTH), row_spec(POOL_WIDTH),
            pl.BlockSpec((blocks_per_tile, 1, ATTN_WIDTH), lambda i: (i, 0, 0)),
        ),
        out_shape=out_shape,
        compiler_params=pltpu.CompilerParams(
            dimension_semantics=("arbitrary",), vmem_limit_bytes=VMEM_LIMIT),
        name="in_proj",
    )(x2, gain, w_bf16, cos_t, sin_t)


def _select_bias(gate, n_past):
    lane = lax.broadcasted_iota(jnp.int32, gate.shape, 1)
    lane_f = lane.astype(jnp.float32)
    neg_inf = jnp.float32(-jnp.inf)
    g = jnp.where(lane < n_past, gate, neg_inf)
    bias = jnp.full(gate.shape, MASK_BIAS, dtype=jnp.float32)
    for _ in range(MOBA_TOPK):
        best = jnp.max(g, axis=1, keepdims=True)
        first = jnp.min(jnp.where(g == best, lane_f, float(LANES)), axis=1, keepdims=True)
        pick = (lane_f == first) & (best > neg_inf)
        bias = jnp.where(pick, 0.0, bias)
        g = jnp.where(pick, neg_inf, g)
    return bias


def _moba_kernel(q_ref, k_ref, v_ref, onehot_ref, kmean_ref, ga_ref, o_ref,
                 qaug_ref, m_ref, l_ref, acc_ref):
    j = pl.program_id(2)
    blk = MOBA_BLOCK
    q = q_ref[...].astype(jnp.float32)
    lane = lax.broadcasted_iota(jnp.int32, (blk, LANES), 1)
    km = kmean_ref[...]
    km_hi = km.astype(jnp.bfloat16)
    km_lo = (km - km_hi.astype(jnp.float32)).astype(jnp.bfloat16)

    row = lax.broadcasted_iota(jnp.int32, (blk, blk), 0)
    col = lax.broadcasted_iota(jnp.int32, (blk, blk), 1)
    causal = col <= row
    own = pl.ds(pl.multiple_of(j * blk, blk), blk)
    k_own = k_ref[own, :]
    v_own = v_ref[own, :]

    for h in range(HEADS_PER_TILE):
        head_lanes = (lane // HEAD_DIM) == h
        qh = jnp.where(head_lanes, q, 0.0).astype(jnp.bfloat16)
        gate = (lax.dot_general(qh, km_hi, _NT, preferred_element_type=jnp.float32)
                + lax.dot_general(qh, km_lo, _NT, preferred_element_type=jnp.float32))
        bias = _select_bias(gate, j)
        qaug_ref[h, :, :LANES] = qh
        qaug_ref[h, :, LANES:] = bias.astype(jnp.bfloat16)

        s = lax.dot_general(qh, k_own, _NT, preferred_element_type=jnp.float32)
        s = jnp.where(causal, s, -jnp.inf)
        m = jnp.max(s, axis=1, keepdims=True)
        p = jnp.exp(s - m)
        m_ref[h] = m
        l_ref[h] = jnp.sum(p, axis=1, keepdims=True)
        acc_ref[h] = jnp.dot(p.astype(jnp.bfloat16), v_own, preferred_element_type=jnp.float32)

    def body(n, carry):
        rows = pl.ds(pl.multiple_of(n * blk, blk), blk)
        k_aug = jnp.concatenate([k_ref[rows, :], onehot_ref[rows, :]], axis=1)
        v_n = v_ref[rows, :]
        for h in range(HEADS_PER_TILE):
            s = lax.dot_general(qaug_ref[h], k_aug, _NT, preferred_element_type=jnp.float32)
            m_prev = m_ref[h]
            m_new = jnp.maximum(m_prev, jnp.max(s, axis=1, keepdims=True))
            alpha = jnp.exp(m_prev - m_new)
            p = jnp.exp(s - m_new)
            m_ref[h] = m_new
            l_ref[h] = alpha * l_ref[h] + jnp.sum(p, axis=1, keepdims=True)
            acc_ref[h] = alpha * acc_ref[h] + jnp.dot(p.astype(jnp.bfloat16), v_n,
                                                      preferred_element_type=jnp.float32)
        return carry

    lax.fori_loop(0, j, body, 0)

    out = jnp.where(lane < HEAD_DIM, acc_ref[0] / l_ref[0], acc_ref[1] / l_ref[1])
    o_ref[...] = (out * _silu(ga_ref[...])).astype(o_ref.dtype)


def _moba_attention(q, k, v, onehot, kmean_pad, gate_attn):
    batch, seq, _ = q.shape
    n_blocks = seq // MOBA_BLOCK
    tile_spec = pl.BlockSpec((None, MOBA_BLOCK, LANES), lambda b, hp, j: (b, j, hp))
    full_spec = pl.BlockSpec((None, seq, LANES), lambda b, hp, j: (b, 0, hp))
    return pl.pallas_call(
        _moba_kernel,
        grid=(batch, ATTN_WIDTH // LANES, n_blocks),
        in_specs=[
            tile_spec, full_spec, full_spec,
            pl.BlockSpec((seq, LANES), lambda b, hp, j: (0, 0)),
            pl.BlockSpec((None, LANES, LANES), lambda b, hp, j: (b, 0, hp)),
            tile_spec,
        ],
        out_specs=tile_spec,
        out_shape=jax.ShapeDtypeStruct((batch, seq, ATTN_WIDTH), jnp.bfloat16),
        scratch_shapes=[
            pltpu.VMEM((HEADS_PER_TILE, MOBA_BLOCK, 2 * LANES), jnp.bfloat16),
            pltpu.VMEM((HEADS_PER_TILE, MOBA_BLOCK, 1), jnp.float32),
            pltpu.VMEM((HEADS_PER_TILE, MOBA_BLOCK, 1), jnp.float32),
            pltpu.VMEM((HEADS_PER_TILE, MOBA_BLOCK, LANES), jnp.float32),
        ],
        compiler_params=pltpu.CompilerParams(
            dimension_semantics=("arbitrary", "arbitrary", "arbitrary"),
            vmem_limit_bytes=VMEM_LIMIT),
        name="moba_attention",
    )(q, k, v, onehot, kmean_pad, gate_attn)


def _out_proj_kernel(apply_final_norm, x_ref, attn_ref, u_ref, uprev_ref, gp_ref, wpool_ref, bpool_ref,
                     pscale_ref, wout_ref, fgain_ref, o_ref, ext_ref):
    i = pl.program_id(1)
    halo = uprev_ref[...]
    ext_ref[:POOL_HALO, :] = jnp.where(i > 0, halo, jnp.zeros_like(halo))
    ext_ref[POOL_HALO:, :] = u_ref[...]
    pos = i * ROW_TILE + lax.broadcasted_iota(jnp.int32, (ROW_TILE, 1), 0)

    pooled_parts = []
    for g, window in enumerate(POOL_WINDOWS):
        sl = slice(g * POOL_GROUP, (g + 1) * POOL_GROUP)
        u_g = ext_ref[POOL_HALO:, sl]
        total = u_g
        for t in range(1, window):
            total = total + ext_ref[POOL_HALO - t:POOL_HALO - t + ROW_TILE, sl]
        count = jnp.minimum(pos + 1, window).astype(jnp.float32)
        pooled = total / count - u_g
        y = jnp.dot(pooled.astype(jnp.bfloat16), wpool_ref[g], preferred_element_type=jnp.float32)
        pooled_parts.append(y + bpool_ref[g])
    pool = jnp.concatenate(pooled_parts, axis=1) * pscale_ref[...]
    pool = (pool * _silu(gp_ref[...])).astype(jnp.bfloat16)

    mixed = jnp.concatenate([attn_ref[...], pool], axis=1)
    y = x_ref[...] + jnp.dot(mixed, wout_ref[...], preferred_element_type=jnp.float32)
    if apply_final_norm:
        inv = lax.rsqrt(jnp.mean(y * y, axis=-1, keepdims=True) + EPS)
        y = y * inv * fgain_ref[...]
    o_ref[...] = y


def _out_proj(x, attn, u, gate_pool, w_pool, b_pool, pool_scale, w_out, final_gain, apply_final_norm):
    batch, seq, _ = x.shape
    tiles = seq // ROW_TILE
    halo_per_tile = ROW_TILE // POOL_HALO
    row_spec = lambda w: pl.BlockSpec((None, ROW_TILE, w), lambda b, i: (b, i, 0))
    const2 = lambda shape: pl.BlockSpec(shape, lambda b, i: (0, 0))
    const3 = lambda shape: pl.BlockSpec(shape, lambda b, i: (0, 0, 0))
    return pl.pallas_call(
        functools.partial(_out_proj_kernel, apply_final_norm),
        grid=(batch, tiles),
        in_specs=[
            row_spec(D_MODEL),
            row_spec(ATTN_WIDTH),
            row_spec(POOL_WIDTH),
            pl.BlockSpec((None, POOL_HALO, POOL_WIDTH),
                         lambda b, i: (b, jnp.maximum(i * halo_per_tile - 1, 0), 0)),
            row_spec(POOL_WIDTH),
            const3((len(POOL_WINDOWS), POOL_GROUP, POOL_GROUP)),
            const3((len(POOL_WINDOWS), 1, POOL_GROUP)),
            const2((1, POOL_WIDTH)),
            const2((D_MODEL, D_MODEL)),
            const2((1, D_MODEL)),
        ],
        out_specs=row_spec(D_MODEL),
        out_shape=jax.ShapeDtypeStruct((batch, seq, D_MODEL), jnp.float32),
        scratch_shapes=[pltpu.VMEM((POOL_HALO + ROW_TILE, POOL_WIDTH), jnp.float32)],
        compiler_params=pltpu.CompilerParams(
            dimension_semantics=("arbitrary", "arbitrary"), vmem_limit_bytes=VMEM_LIMIT),
        name="pool_out_proj",
    )(x, attn, u, u, gate_pool, w_pool, b_pool, pool_scale, w_out, final_gain)


def _rope_tables(seq):
    pos = jnp.arange(seq, dtype=jnp.float32)
    inv_freq = 1.0 / (ROPE_THETA ** (jnp.arange(0, HEAD_DIM, 2, dtype=jnp.float32) / HEAD_DIM))
    ang = pos[:, None] * inv_freq[None, :]
    cos, sin = jnp.cos(ang), jnp.sin(ang)
    reps = LANES // (HEAD_DIM // 2)
    cos_t = jnp.tile(cos, (1, reps))
    sin_t = jnp.tile(jnp.concatenate([-sin, sin], axis=1), (1, HEADS_PER_TILE))
    return cos_t, sin_t


def kernel(x, norm_gain, w_in, w_pool, b_pool, pool_scale, w_out, final_gain):
    batch, seq, d_model = x.shape
    depth = w_in.shape[0]
    assert d_model == D_MODEL and seq % ROW_TILE == 0 and seq // MOBA_BLOCK <= LANES
    n_blocks = seq // MOBA_BLOCK
    cos_t, sin_t = _rope_tables(seq)
    onehot = (jnp.arange(seq)[:, None] // MOBA_BLOCK == jnp.arange(LANES)[None, :]).astype(jnp.bfloat16)

    for l in range(depth):
        q, k, v, gate_attn, u, gate_pool, kmean = _in_proj(
            x.reshape(batch * seq, d_model), norm_gain[l][None, :], w_in[l].astype(jnp.bfloat16),
            cos_t, sin_t, seq)
        shape3 = lambda t: t.reshape(batch, seq, t.shape[-1])
        kmean_pad = jnp.pad(kmean.reshape(batch, n_blocks, ATTN_WIDTH),
                            ((0, 0), (0, LANES - n_blocks), (0, 0)))
        attn = _moba_attention(shape3(q), shape3(k), shape3(v), onehot, kmean_pad, shape3(gate_attn))
        x = _out_proj(x, attn, shape3(u), shape3(gate_pool), w_pool[l].astype(jnp.bfloat16),
                      b_pool[l][:, None, :], pool_scale[l][None, :], w_out[l].astype(jnp.bfloat16),
                      final_gain[None, :], apply_final_norm=(l == depth - 1))
    return x
```

```python
import functools

import jax
import jax.numpy as jnp
from jax import lax
from jax.experimental import pallas as pl
from jax.experimental.pallas import tpu as pltpu

D_MODEL = 1024
ATTN_WIDTH = D_MODEL // 2
POOL_WIDTH = D_MODEL - ATTN_WIDTH
HEAD_DIM = 64
MOBA_BLOCK = 256
MOBA_TOPK = 3
POOL_WINDOWS = (2, 4, 8, 16)
POOL_GROUP = POOL_WIDTH // len(POOL_WINDOWS)
ROPE_THETA = 10000.0
EPS = 1e-6
IN_WIDTH = 4 * ATTN_WIDTH + 2 * POOL_WIDTH

LANES = 128
SUBLANES = 8
HEADS_PER_TILE = LANES // HEAD_DIM
HEAD_TILES = ATTN_WIDTH // LANES
ROW_TILE = 512
POOL_HALO = 16
MASK_BIAS = -1e30
KEY_UNROLL = 4
BF16_ROWS = 16
VT_ROWS = LANES + BF16_ROWS
Q_SCALE = HEAD_DIM ** -0.5 * 1.4426950408889634
VMEM_LIMIT = 56 * 1024 * 1024


def _silu(t):
    return t * (1.0 / (1.0 + jnp.exp(-t)))


def _rope(t, cos, sin_signed, first_half):
    swapped = jnp.where(first_half, pltpu.roll(t, LANES - HEAD_DIM // 2, axis=1),
                        pltpu.roll(t, HEAD_DIM // 2, axis=1))
    return t * cos + swapped * sin_signed


def _in_proj_kernel(x_ref, gain_ref, w_ref, cos_ref, sin_ref,
                    qt_ref, k_ref, vt_ref, ga_ref, u_ref, gp_ref, kmean_ref):
    x = x_ref[...]
    inv = lax.rsqrt(jnp.mean(x * x, axis=-1, keepdims=True) + EPS)
    h = (x * inv * gain_ref[...]).astype(jnp.bfloat16)

    def proj(c):
        return jnp.dot(h, w_ref[:, c * ATTN_WIDTH:(c + 1) * ATTN_WIDTH],
                       preferred_element_type=jnp.float32)

    cos = cos_ref[...]
    sin = sin_ref[...]
    lane = lax.broadcasted_iota(jnp.int32, (ROW_TILE, LANES), 1)
    first_half = (lane % HEAD_DIM) < (HEAD_DIM // 2)

    q = proj(0)
    k = proj(1)
    v = proj(2)
    for t in range(HEAD_TILES):
        sl = slice(t * LANES, (t + 1) * LANES)
        qr = _rope(q[:, sl], cos, sin, first_half) * Q_SCALE
        kr = _rope(k[:, sl], cos, sin, first_half)
        k_ref[:, sl] = kr.astype(k_ref.dtype)
        for b in range(ROW_TILE // MOBA_BLOCK):
            blk_rows = slice(b * MOBA_BLOCK, (b + 1) * MOBA_BLOCK)
            kmean_ref[b, :, sl] = jnp.sum(kr[blk_rows], axis=0, keepdims=True) * (1.0 / MOBA_BLOCK)
            qt_ref[b, t] = qr[blk_rows].T.astype(qt_ref.dtype)
            vt_ref[b, t, :LANES] = v[blk_rows, sl].T.astype(vt_ref.dtype)
            vt_ref[b, t, LANES:] = jnp.ones((BF16_ROWS, MOBA_BLOCK), vt_ref.dtype)
    ga_ref[...] = proj(3)
    u_ref[...] = proj(4)
    gp_ref[...] = proj(5)


def _in_proj(x2, gain, w_bf16, cos_t, sin_t, seq):
    rows = x2.shape[0]
    n_tiles = rows // ROW_TILE
    seq_tiles = seq // ROW_TILE
    row_spec = lambda w: pl.BlockSpec((ROW_TILE, w), lambda i: (i, 0))
    tab_spec = pl.BlockSpec((ROW_TILE, LANES), lambda i: (i % seq_tiles, 0))
    blocks_per_tile = ROW_TILE // MOBA_BLOCK
    n_blocks = rows // MOBA_BLOCK
    transposed = lambda r: jax.ShapeDtypeStruct((n_blocks, HEAD_TILES, r, MOBA_BLOCK), jnp.bfloat16)
    transposed_spec = lambda r: pl.BlockSpec((blocks_per_tile, HEAD_TILES, r, MOBA_BLOCK),
                                             lambda i: (i, 0, 0, 0))
    out_shape = (
        transposed(LANES),
        jax.ShapeDtypeStruct((rows, ATTN_WIDTH), jnp.bfloat16),
        transposed(VT_ROWS),
        jax.ShapeDtypeStruct((rows, ATTN_WIDTH), jnp.float32),
        jax.ShapeDtypeStruct((rows, POOL_WIDTH), jnp.float32),
        jax.ShapeDtypeStruct((rows, POOL_WIDTH), jnp.float32),
        jax.ShapeDtypeStruct((n_blocks, 1, ATTN_WIDTH), jnp.float32),
    )
    return pl.pallas_call(
        _in_proj_kernel,
        grid=(n_tiles,),
        in_specs=[
            row_spec(D_MODEL),
            pl.BlockSpec((1, D_MODEL), lambda i: (0, 0)),
            pl.BlockSpec((D_MODEL, IN_WIDTH), lambda i: (0, 0)),
            tab_spec, tab_spec,
        ],
        out_specs=(
            transposed_spec(LANES), row_spec(ATTN_WIDTH), transposed_spec(VT_ROWS),
            row_spec(ATTN_WIDTH), row_spec(POOL_WIDTH), row_spec(POOL_WIDTH),
            pl.BlockSpec((blocks_per_tile, 1, ATTN_WIDTH), lambda i: (i, 0, 0)),
        ),
        out_shape=out_shape,
        compiler_params=pltpu.CompilerParams(
            dimension_semantics=("arbitrary",), vmem_limit_bytes=VMEM_LIMIT),
        name="in_proj",
    )(x2, gain, w_bf16, cos_t, sin_t)


def _select_bias_t(gate_t, n_past):
    slot = lax.broadcasted_iota(jnp.int32, gate_t.shape, 0)
    slot_f = slot.astype(jnp.float32)
    neg_inf = jnp.float32(-jnp.inf)
    g = jnp.where(slot < n_past, gate_t, neg_inf)
    bias = jnp.full(gate_t.shape, MASK_BIAS, dtype=jnp.float32)
    for _ in range(MOBA_TOPK):
        best = jnp.max(g, axis=0, keepdims=True)
        first = jnp.min(jnp.where(g == best, slot_f, float(gate_t.shape[0])), axis=0, keepdims=True)
        pick = (slot_f == first) & (best > neg_inf)
        bias = jnp.where(pick, 0.0, bias)
        g = jnp.where(pick, neg_inf, g)
    return bias


def _fold_rows(t, op):
    return op(t.reshape(t.shape[0] // SUBLANES, SUBLANES, t.shape[1]), axis=0)


def _moba_kernel(qt_ref, k_ref, vt_ref, onehot_ref, kmean_ref, ga_ref, o_ref, qaug_ref, s_ref):
    j = pl.program_id(2)
    blk = MOBA_BLOCK
    n_slots = kmean_ref.shape[0]
    n_groups = (j + KEY_UNROLL - 1) // KEY_UNROLL
    own_slot = n_groups * KEY_UNROLL

    qt = qt_ref[...].astype(jnp.float32)
    feat = lax.broadcasted_iota(jnp.int32, (LANES, blk), 0)
    km = kmean_ref[...]
    km_hi = km.astype(jnp.bfloat16)
    km_lo = (km - km_hi.astype(jnp.float32)).astype(jnp.bfloat16)
    key_pos = lax.broadcasted_iota(jnp.int32, (blk, blk), 0)
    qry_pos = lax.broadcasted_iota(jnp.int32, (blk, blk), 1)
    causal = key_pos <= qry_pos
    k_own = k_ref[pl.ds(pl.multiple_of(j * blk, blk), blk), :]

    m_init = []
    for h in range(HEADS_PER_TILE):
        qh = jnp.where((feat // HEAD_DIM) == h, qt, 0.0).astype(jnp.bfloat16)
        gate_t = (jnp.dot(km_hi, qh, preferred_element_type=jnp.float32)
                  + jnp.dot(km_lo, qh, preferred_element_type=jnp.float32))
        qaug_ref[h, :LANES, :] = qh
        qaug_ref[h, LANES:LANES + n_slots, :] = _select_bias_t(gate_t, j).astype(jnp.bfloat16)
        qaug_ref[h, LANES + n_slots:, :] = jnp.zeros((LANES - n_slots, blk), jnp.bfloat16)
        s = jnp.dot(k_own, qh, preferred_element_type=jnp.float32)
        s = jnp.where(causal, s, -jnp.inf)
        s_ref[h, own_slot] = s
        m_init.append(_fold_rows(s, jnp.max))

    def score_group(g, m_run):
        m_run = list(m_run)
        for i in range(KEY_UNROLL):
            n = g * KEY_UNROLL + i
            rows = pl.ds(pl.multiple_of(n * blk, blk), blk)
            k_aug = jnp.concatenate([k_ref[rows, :], onehot_ref[rows, :]], axis=1)
            for h in range(HEADS_PER_TILE):
                s = jnp.dot(k_aug, qaug_ref[h], preferred_element_type=jnp.float32)
                s_ref[h, n] = s
                m_run[h] = jnp.maximum(m_run[h], _fold_rows(s, jnp.max))
        return tuple(m_run)

    m_run = lax.fori_loop(0, n_groups, score_group, tuple(m_init))
    m_fin = [jnp.max(m, axis=0, keepdims=True) for m in m_run]

    def weighted_values(n, h):
        p = jnp.exp2(s_ref[h, n] - m_fin[h]).astype(jnp.bfloat16)
        r = jnp.dot(vt_ref[n], p, preferred_element_type=jnp.float32)
        return r[h * HEAD_DIM:(h + 1) * HEAD_DIM], r[LANES:LANES + SUBLANES]

    def value_group(g, carry):
        carry = list(carry)
        for i in range(KEY_UNROLL):
            for h in range(HEADS_PER_TILE):
                o, l = weighted_values(g * KEY_UNROLL + i, h)
                carry[h] = (carry[h][0] + o, carry[h][1] + l)
        return tuple(carry)

    init = []
    for h in range(HEADS_PER_TILE):
        p = jnp.exp2(s_ref[h, own_slot] - m_fin[h]).astype(jnp.bfloat16)
        r = jnp.dot(vt_ref[j], p, preferred_element_type=jnp.float32)
        init.append((r[h * HEAD_DIM:(h + 1) * HEAD_DIM], r[LANES:LANES + SUBLANES]))
    acc = lax.fori_loop(0, n_groups, value_group, tuple(init))

    out_t = jnp.concatenate([o / l[:1] for o, l in acc], axis=0)
    o_ref[...] = (out_t.T * _silu(ga_ref[...])).astype(o_ref.dtype)


def _moba_attention(qt, k, vt, onehot, kmean, gate_attn):
    batch, seq, _ = k.shape
    n_blocks = seq // MOBA_BLOCK
    slots = -(-n_blocks // KEY_UNROLL) * KEY_UNROLL + 1
    tile_spec = pl.BlockSpec((None, MOBA_BLOCK, LANES), lambda b, hp, j: (b, j, hp))
    return pl.pallas_call(
        _moba_kernel,
        grid=(batch, HEAD_TILES, n_blocks),
        in_specs=[
            pl.BlockSpec((None, None, None, LANES, MOBA_BLOCK), lambda b, hp, j: (b, j, hp, 0, 0)),
            pl.BlockSpec((None, seq, LANES), lambda b, hp, j: (b, 0, hp)),
            pl.BlockSpec((None, n_blocks, None, VT_ROWS, MOBA_BLOCK), lambda b, hp, j: (b, 0, hp, 0, 0)),
            pl.BlockSpec((seq, LANES), lambda b, hp, j: (0, 0)),
            pl.BlockSpec((None, n_blocks, LANES), lambda b, hp, j: (b, 0, hp)),
            tile_spec,
        ],
        out_specs=tile_spec,
        out_shape=jax.ShapeDtypeStruct((batch, seq, ATTN_WIDTH), jnp.bfloat16),
        scratch_shapes=[
            pltpu.VMEM((HEADS_PER_TILE, 2 * LANES, MOBA_BLOCK), jnp.bfloat16),
            pltpu.VMEM((HEADS_PER_TILE, slots, MOBA_BLOCK, MOBA_BLOCK), jnp.float32),
        ],
        compiler_params=pltpu.CompilerParams(
            dimension_semantics=("arbitrary", "arbitrary", "arbitrary"),
            vmem_limit_bytes=VMEM_LIMIT),
        name="moba_attention",
    )(qt, k, vt, onehot, kmean, gate_attn)


def _out_proj_kernel(apply_final_norm, x_ref, attn_ref, u_ref, uprev_ref, gp_ref, wpool_ref, bpool_ref,
                     pscale_ref, wout_ref, fgain_ref, o_ref, ext_ref):
    i = pl.program_id(1)
    halo = uprev_ref[...]
    ext_ref[:POOL_HALO, :] = jnp.where(i > 0, halo, jnp.zeros_like(halo))
    ext_ref[POOL_HALO:, :] = u_ref[...]
    pos = i * ROW_TILE + lax.broadcasted_iota(jnp.int32, (ROW_TILE, 1), 0)

    pooled_parts = []
    for g, window in enumerate(POOL_WINDOWS):
        sl = slice(g * POOL_GROUP, (g + 1) * POOL_GROUP)
        u_g = ext_ref[POOL_HALO:, sl]
        total = u_g
        for t in range(1, window):
            total = total + ext_ref[POOL_HALO - t:POOL_HALO - t + ROW_TILE, sl]
        count = jnp.minimum(pos + 1, window).astype(jnp.float32)
        pooled = total / count - u_g
        y = jnp.dot(pooled.astype(jnp.bfloat16), wpool_ref[g], preferred_element_type=jnp.float32)
        pooled_parts.append(y + bpool_ref[g])
    pool = jnp.concatenate(pooled_parts, axis=1) * pscale_ref[...]
    pool = (pool * _silu(gp_ref[...])).astype(jnp.bfloat16)

    mixed = jnp.concatenate([attn_ref[...], pool], axis=1)
    y = x_ref[...] + jnp.dot(mixed, wout_ref[...], preferred_element_type=jnp.float32)
    if apply_final_norm:
        inv = lax.rsqrt(jnp.mean(y * y, axis=-1, keepdims=True) + EPS)
        y = y * inv * fgain_ref[...]
    o_ref[...] = y


def _out_proj(x, attn, u, gate_pool, w_pool, b_pool, pool_scale, w_out, final_gain, apply_final_norm):
    batch, seq, _ = x.shape
    tiles = seq // ROW_TILE
    halo_per_tile = ROW_TILE // POOL_HALO
    row_spec = lambda w: pl.BlockSpec((None, ROW_TILE, w), lambda b, i: (b, i, 0))
    const2 = lambda shape: pl.BlockSpec(shape, lambda b, i: (0, 0))
    const3 = lambda shape: pl.BlockSpec(shape, lambda b, i: (0, 0, 0))
    return pl.pallas_call(
        functools.partial(_out_proj_kernel, apply_final_norm),
        grid=(batch, tiles),
        in_specs=[
            row_spec(D_MODEL),
            row_spec(ATTN_WIDTH),
            row_spec(POOL_WIDTH),
            pl.BlockSpec((None, POOL_HALO, POOL_WIDTH),
                         lambda b, i: (b, jnp.maximum(i * halo_per_tile - 1, 0), 0)),
            row_spec(POOL_WIDTH),
            const3((len(POOL_WINDOWS), POOL_GROUP, POOL_GROUP)),
            const3((len(POOL_WINDOWS), 1, POOL_GROUP)),
            const2((1, POOL_WIDTH)),
            const2((D_MODEL, D_MODEL)),
            const2((1, D_MODEL)),
        ],
        out_specs=row_spec(D_MODEL),
        out_shape=jax.ShapeDtypeStruct((batch, seq, D_MODEL), jnp.float32),
        scratch_shapes=[pltpu.VMEM((POOL_HALO + ROW_TILE, POOL_WIDTH), jnp.float32)],
        compiler_params=pltpu.CompilerParams(
            dimension_semantics=("arbitrary", "arbitrary"), vmem_limit_bytes=VMEM_LIMIT),
        name="pool_out_proj",
    )(x, attn, u, u, gate_pool, w_pool, b_pool, pool_scale, w_out, final_gain)


def _rope_tables(seq):
    pos = jnp.arange(seq, dtype=jnp.float32)
    inv_freq = 1.0 / (ROPE_THETA ** (jnp.arange(0, HEAD_DIM, 2, dtype=jnp.float32) / HEAD_DIM))
    ang = pos[:, None] * inv_freq[None, :]
    cos, sin = jnp.cos(ang), jnp.sin(ang)
    reps = LANES // (HEAD_DIM // 2)
    cos_t = jnp.tile(cos, (1, reps))
    sin_t = jnp.tile(jnp.concatenate([-sin, sin], axis=1), (1, HEADS_PER_TILE))
    return cos_t, sin_t


def kernel(x, norm_gain, w_in, w_pool, b_pool, pool_scale, w_out, final_gain):
    batch, seq, d_model = x.shape
    depth = w_in.shape[0]
    assert d_model == D_MODEL and seq % ROW_TILE == 0 and seq // MOBA_BLOCK <= LANES
    n_blocks = seq // MOBA_BLOCK
    cos_t, sin_t = _rope_tables(seq)
    onehot = (jnp.arange(seq)[:, None] // MOBA_BLOCK == jnp.arange(LANES)[None, :]).astype(jnp.bfloat16)

    for l in range(depth):
        qt, k, vt, gate_attn, u, gate_pool, kmean = _in_proj(
            x.reshape(batch * seq, d_model), norm_gain[l][None, :], w_in[l].astype(jnp.bfloat16),
            cos_t, sin_t, seq)
        shape3 = lambda t: t.reshape(batch, seq, t.shape[-1])
        per_block = lambda t: t.reshape(batch, n_blocks, HEAD_TILES, t.shape[-2], MOBA_BLOCK)
        attn = _moba_attention(per_block(qt), shape3(k), per_block(vt), onehot,
                               kmean.reshape(batch, n_blocks, ATTN_WIDTH), shape3(gate_attn))
        x = _out_proj(x, attn, shape3(u), shape3(gate_pool), w_pool[l].astype(jnp.bfloat16),
                      b_pool[l][:, None, :], pool_scale[l][None, :], w_out[l].astype(jnp.bfloat16),
                      final_gain[None, :], apply_final_norm=(l == depth - 1))
    return x
```

```python
import functools

import jax
import jax.numpy as jnp
from jax import lax
from jax.experimental import pallas as pl
from jax.experimental.pallas import tpu as pltpu

D_MODEL = 1024
ATTN_WIDTH = D_MODEL // 2
POOL_WIDTH = D_MODEL - ATTN_WIDTH
HEAD_DIM = 64
MOBA_BLOCK = 256
MOBA_TOPK = 3
POOL_WINDOWS = (2, 4, 8, 16)
POOL_GROUP = POOL_WIDTH // len(POOL_WINDOWS)
ROPE_THETA = 10000.0
EPS = 1e-6
IN_WIDTH = 4 * ATTN_WIDTH + 2 * POOL_WIDTH

LANES = 128
SUBLANES = 8
HEADS_PER_TILE = LANES // HEAD_DIM
HEAD_TILES = ATTN_WIDTH // LANES
ROW_TILE = 512
POOL_HALO = 16
MASK_BIAS = -1e30
KEY_UNROLL = 4
BF16_ROWS = 16
VT_ROWS = LANES + BF16_ROWS
Q_SCALE = HEAD_DIM ** -0.5 * 1.4426950408889634
VMEM_LIMIT = 56 * 1024 * 1024


def _silu(t):
    return t * (1.0 / (1.0 + jnp.exp(-t)))


def _rope(t, cos, sin_signed, first_half):
    swapped = jnp.where(first_half, pltpu.roll(t, LANES - HEAD_DIM // 2, axis=1),
                        pltpu.roll(t, HEAD_DIM // 2, axis=1))
    return t * cos + swapped * sin_signed


def _in_proj_kernel(x_ref, gain_ref, w_ref, cos_ref, sin_ref,
                    qt_ref, k_ref, vt_ref, ga_ref, u_ref, gp_ref, kmean_ref):
    x = x_ref[...]
    inv = lax.rsqrt(jnp.mean(x * x, axis=-1, keepdims=True) + EPS)
    h = (x * inv * gain_ref[...]).astype(jnp.bfloat16)

    def proj(c):
        return jnp.dot(h, w_ref[:, c * ATTN_WIDTH:(c + 1) * ATTN_WIDTH],
                       preferred_element_type=jnp.float32)

    cos = cos_ref[...]
    sin = sin_ref[...]
    lane = lax.broadcasted_iota(jnp.int32, (ROW_TILE, LANES), 1)
    first_half = (lane % HEAD_DIM) < (HEAD_DIM // 2)

    q = proj(0)
    k = proj(1)
    v = proj(2)
    for t in range(HEAD_TILES):
        sl = slice(t * LANES, (t + 1) * LANES)
        qr = _rope(q[:, sl], cos, sin, first_half) * Q_SCALE
        kr = _rope(k[:, sl], cos, sin, first_half)
        k_ref[:, sl] = kr.astype(k_ref.dtype)
        for b in range(ROW_TILE // MOBA_BLOCK):
            blk_rows = slice(b * MOBA_BLOCK, (b + 1) * MOBA_BLOCK)
            kmean_ref[b, :, sl] = jnp.sum(kr[blk_rows], axis=0, keepdims=True) * (1.0 / MOBA_BLOCK)
            qt_ref[b, t] = qr[blk_rows].T.astype(qt_ref.dtype)
            vt_ref[b, t, :LANES] = v[blk_rows, sl].T.astype(vt_ref.dtype)
            vt_ref[b, t, LANES:] = jnp.ones((BF16_ROWS, MOBA_BLOCK), vt_ref.dtype)
    ga_ref[...] = proj(3)
    u_ref[...] = proj(4)
    gp_ref[...] = proj(5)


def _in_proj(x2, gain, w_bf16, cos_t, sin_t, seq):
    rows = x2.shape[0]
    n_tiles = rows // ROW_TILE
    seq_tiles = seq // ROW_TILE
    row_spec = lambda w: pl.BlockSpec((ROW_TILE, w), lambda i: (i, 0))
    tab_spec = pl.BlockSpec((ROW_TILE, LANES), lambda i: (i % seq_tiles, 0))
    blocks_per_tile = ROW_TILE // MOBA_BLOCK
    n_blocks = rows // MOBA_BLOCK
    transposed = lambda r: jax.ShapeDtypeStruct((n_blocks, HEAD_TILES, r, MOBA_BLOCK), jnp.bfloat16)
    transposed_spec = lambda r: pl.BlockSpec((blocks_per_tile, HEAD_TILES, r, MOBA_BLOCK),
                                             lambda i: (i, 0, 0, 0))
    out_shape = (
        transposed(LANES),
        jax.ShapeDtypeStruct((rows, ATTN_WIDTH), jnp.bfloat16),
        transposed(VT_ROWS),
        jax.ShapeDtypeStruct((rows, ATTN_WIDTH), jnp.float32),
        jax.ShapeDtypeStruct((rows, POOL_WIDTH), jnp.float32),
        jax.ShapeDtypeStruct((rows, POOL_WIDTH), jnp.float32),
        jax.ShapeDtypeStruct((n_blocks, 1, ATTN_WIDTH), jnp.float32),
    )
    return pl.pallas_call(
        _in_proj_kernel,
        grid=(n_tiles,),
        in_specs=[
            row_spec(D_MODEL),
            pl.BlockSpec((1, D_MODEL), lambda i: (0, 0)),
            pl.BlockSpec((D_MODEL, IN_WIDTH), lambda i: (0, 0)),
            tab_spec, tab_spec,
        ],
        out_specs=(
            transposed_spec(LANES), row_spec(ATTN_WIDTH), transposed_spec(VT_ROWS),
            row_spec(ATTN_WIDTH), row_spec(POOL_WIDTH), row_spec(POOL_WIDTH),
            pl.BlockSpec((blocks_per_tile, 1, ATTN_WIDTH), lambda i: (i, 0, 0)),
        ),
        out_shape=out_shape,
        compiler_params=pltpu.CompilerParams(
            dimension_semantics=("arbitrary",), vmem_limit_bytes=VMEM_LIMIT),
        name="in_proj",
    )(x2, gain, w_bf16, cos_t, sin_t)


def _select_bias_t(gate_t, n_past):
    slot = lax.broadcasted_iota(jnp.int32, gate_t.shape, 0)
    slot_f = slot.astype(jnp.float32)
    neg_inf = jnp.float32(-jnp.inf)
    g = jnp.where(slot < n_past, gate_t, neg_inf)
    bias = jnp.full(gate_t.shape, MASK_BIAS, dtype=jnp.float32)
    for _ in range(MOBA_TOPK):
        best = jnp.max(g, axis=0, keepdims=True)
        first = jnp.min(jnp.where(g == best, slot_f, float(gate_t.shape[0])), axis=0, keepdims=True)
        pick = (slot_f == first) & (best > neg_inf)
        bias = jnp.where(pick, 0.0, bias)
        g = jnp.where(pick, neg_inf, g)
    return bias


def _fold_rows(t, op):
    return op(t.reshape(t.shape[0] // SUBLANES, SUBLANES, t.shape[1]), axis=0)


def _moba_kernel(qt_ref, k_ref, vt_ref, onehot_ref, kmean_ref, ga_ref, o_ref,
                 qaug_ref, s_ref, own_ref, mfin_ref):
    t = pl.program_id(2)
    blk = MOBA_BLOCK
    n_slots = kmean_ref.shape[0]
    j = jnp.minimum(t, n_slots - 1)
    j_prev = jnp.maximum(t - 1, 0)
    groups_new = jnp.where(t < n_slots, (t + KEY_UNROLL - 1) // KEY_UNROLL, 0)
    groups_old = (j_prev + KEY_UNROLL - 1) // KEY_UNROLL
    groups_both = jnp.minimum(groups_new, groups_old)

    @pl.when(t == 0)
    def _():
        own_ref[...] = jnp.zeros_like(own_ref)
        mfin_ref[...] = jnp.zeros_like(mfin_ref)

    m_prev = [mfin_ref[h, :1] for h in range(HEADS_PER_TILE)]

    def weighted_values(s, v_aug, h):
        p = jnp.exp2(s - m_prev[h]).astype(jnp.bfloat16)
        r = jnp.dot(v_aug, p, preferred_element_type=jnp.float32)
        return r[h * HEAD_DIM:(h + 1) * HEAD_DIM], r[LANES:LANES + SUBLANES]

    acc_init = [weighted_values(own_ref[h], vt_ref[j_prev], h) for h in range(HEADS_PER_TILE)]

    qt = qt_ref[...].astype(jnp.float32)
    feat = lax.broadcasted_iota(jnp.int32, (LANES, blk), 0)
    km = kmean_ref[...]
    km_hi = km.astype(jnp.bfloat16)
    km_lo = (km - km_hi.astype(jnp.float32)).astype(jnp.bfloat16)
    key_pos = lax.broadcasted_iota(jnp.int32, (blk, blk), 0)
    qry_pos = lax.broadcasted_iota(jnp.int32, (blk, blk), 1)
    causal = key_pos <= qry_pos
    k_own = k_ref[pl.ds(pl.multiple_of(j * blk, blk), blk), :]

    m_init = []
    for h in range(HEADS_PER_TILE):
        qh = jnp.where((feat // HEAD_DIM) == h, qt, 0.0).astype(jnp.bfloat16)
        gate_t = (jnp.dot(km_hi, qh, preferred_element_type=jnp.float32)
                  + jnp.dot(km_lo, qh, preferred_element_type=jnp.float32))
        qaug_ref[h, :LANES, :] = qh
        qaug_ref[h, LANES:LANES + n_slots, :] = _select_bias_t(gate_t, j).astype(jnp.bfloat16)
        qaug_ref[h, LANES + n_slots:, :] = jnp.zeros((LANES - n_slots, blk), jnp.bfloat16)
        s = jnp.dot(k_own, qh, preferred_element_type=jnp.float32)
        s = jnp.where(causal, s, -jnp.inf)
        own_ref[h] = s
        m_init.append(_fold_rows(s, jnp.max))

    def group_body(finish_old, score_new):
        def body(g, carry):
            m_run, acc = [list(c) for c in carry]
            for i in range(KEY_UNROLL):
                n = g * KEY_UNROLL + i
                rows = pl.ds(pl.multiple_of(n * blk, blk), blk)
                if score_new:
                    k_aug = jnp.concatenate([k_ref[rows, :], onehot_ref[rows, :]], axis=1)
                for h in range(HEADS_PER_TILE):
                    if finish_old:
                        o, l = weighted_values(s_ref[h, n], vt_ref[n], h)
                        acc[h] = (acc[h][0] + o, acc[h][1] + l)
                    if score_new:
                        s = jnp.dot(k_aug, qaug_ref[h], preferred_element_type=jnp.float32)
                        s_ref[h, n] = s
                        m_run[h] = jnp.maximum(m_run[h], _fold_rows(s, jnp.max))
            return tuple(m_run), tuple(acc)
        return body

    carry = (tuple(m_init), tuple(acc_init))
    carry = lax.fori_loop(0, groups_both, group_body(True, True), carry)
    carry = lax.fori_loop(groups_both, groups_new, group_body(False, True), carry)
    carry = lax.fori_loop(groups_both, groups_old, group_body(True, False), carry)
    m_run, acc = carry

    for h in range(HEADS_PER_TILE):
        mfin_ref[h] = jnp.broadcast_to(jnp.max(m_run[h], axis=0, keepdims=True), (SUBLANES, blk))

    @pl.when(t > 0)
    def _():
        out_t = jnp.concatenate([o / l[:1] for o, l in acc], axis=0)
        o_ref[...] = (out_t.T * _silu(ga_ref[...])).astype(o_ref.dtype)


def _moba_attention(qt, k, vt, onehot, kmean, gate_attn):
    batch, seq, _ = k.shape
    n_blocks = seq // MOBA_BLOCK
    slots = -(-(n_blocks - 1) // KEY_UNROLL) * KEY_UNROLL
    finished_spec = pl.BlockSpec((None, MOBA_BLOCK, LANES), lambda b, hp, t: (b, jnp.maximum(t - 1, 0), hp))
    return pl.pallas_call(
        _moba_kernel,
        grid=(batch, HEAD_TILES, n_blocks + 1),
        in_specs=[
            pl.BlockSpec((None, None, None, LANES, MOBA_BLOCK),
                         lambda b, hp, t: (b, jnp.minimum(t, n_blocks - 1), hp, 0, 0)),
            pl.BlockSpec((None, seq, LANES), lambda b, hp, t: (b, 0, hp)),
            pl.BlockSpec((None, n_blocks, None, VT_ROWS, MOBA_BLOCK), lambda b, hp, t: (b, 0, hp, 0, 0)),
            pl.BlockSpec((seq, LANES), lambda b, hp, t: (0, 0)),
            pl.BlockSpec((None, n_blocks, LANES), lambda b, hp, t: (b, 0, hp)),
            finished_spec,
        ],
        out_specs=finished_spec,
        out_shape=jax.ShapeDtypeStruct((batch, seq, ATTN_WIDTH), jnp.bfloat16),
        scratch_shapes=[
            pltpu.VMEM((HEADS_PER_TILE, 2 * LANES, MOBA_BLOCK), jnp.bfloat16),
            pltpu.VMEM((HEADS_PER_TILE, slots, MOBA_BLOCK, MOBA_BLOCK), jnp.float32),
            pltpu.VMEM((HEADS_PER_TILE, MOBA_BLOCK, MOBA_BLOCK), jnp.float32),
            pltpu.VMEM((HEADS_PER_TILE, SUBLANES, MOBA_BLOCK), jnp.float32),
        ],
        compiler_params=pltpu.CompilerParams(
            dimension_semantics=("arbitrary", "arbitrary", "arbitrary"),
            vmem_limit_bytes=VMEM_LIMIT),
        name="moba_attention",
    )(qt, k, vt, onehot, kmean, gate_attn)


def _out_proj_kernel(apply_final_norm, x_ref, attn_ref, u_ref, uprev_ref, gp_ref, wpool_ref, bpool_ref,
                     pscale_ref, wout_ref, fgain_ref, o_ref, ext_ref):
    i = pl.program_id(1)
    halo = uprev_ref[...]
    ext_ref[:POOL_HALO, :] = jnp.where(i > 0, halo, jnp.zeros_like(halo))
    ext_ref[POOL_HALO:, :] = u_ref[...]
    pos = i * ROW_TILE + lax.broadcasted_iota(jnp.int32, (ROW_TILE, 1), 0)

    pooled_parts = []
    for g, window in enumerate(POOL_WINDOWS):
        sl = slice(g * POOL_GROUP, (g + 1) * POOL_GROUP)
        u_g = ext_ref[POOL_HALO:, sl]
        total = u_g
        for t in range(1, window):
            total = total + ext_ref[POOL_HALO - t:POOL_HALO - t + ROW_TILE, sl]
        count = jnp.minimum(pos + 1, window).astype(jnp.float32)
        pooled = total / count - u_g
        y = jnp.dot(pooled.astype(jnp.bfloat16), wpool_ref[g], preferred_element_type=jnp.float32)
        pooled_parts.append(y + bpool_ref[g])
    pool = jnp.concatenate(pooled_parts, axis=1) * pscale_ref[...]
    pool = (pool * _silu(gp_ref[...])).astype(jnp.bfloat16)

    mixed = jnp.concatenate([attn_ref[...], pool], axis=1)
    y = x_ref[...] + jnp.dot(mixed, wout_ref[...], preferred_element_type=jnp.float32)
    if apply_final_norm:
        inv = lax.rsqrt(jnp.mean(y * y, axis=-1, keepdims=True) + EPS)
        y = y * inv * fgain_ref[...]
    o_ref[...] = y


def _out_proj(x, attn, u, gate_pool, w_pool, b_pool, pool_scale, w_out, final_gain, apply_final_norm):
    batch, seq, _ = x.shape
    tiles = seq // ROW_TILE
    halo_per_tile = ROW_TILE // POOL_HALO
    row_spec = lambda w: pl.BlockSpec((None, ROW_TILE, w), lambda b, i: (b, i, 0))
    const2 = lambda shape: pl.BlockSpec(shape, lambda b, i: (0, 0))
    const3 = lambda shape: pl.BlockSpec(shape, lambda b, i: (0, 0, 0))
    return pl.pallas_call(
        functools.partial(_out_proj_kernel, apply_final_norm),
        grid=(batch, tiles),
        in_specs=[
            row_spec(D_MODEL),
            row_spec(ATTN_WIDTH),
            row_spec(POOL_WIDTH),
            pl.BlockSpec((None, POOL_HALO, POOL_WIDTH),
                         lambda b, i: (b, jnp.maximum(i * halo_per_tile - 1, 0), 0)),
            row_spec(POOL_WIDTH),
            const3((len(POOL_WINDOWS), POOL_GROUP, POOL_GROUP)),
            const3((len(POOL_WINDOWS), 1, POOL_GROUP)),
            const2((1, POOL_WIDTH)),
            const2((D_MODEL, D_MODEL)),
            const2((1, D_MODEL)),
        ],
        out_specs=row_spec(D_MODEL),
        out_shape=jax.ShapeDtypeStruct((batch, seq, D_MODEL), jnp.float32),
        scratch_shapes=[pltpu.VMEM((POOL_HALO + ROW_TILE, POOL_WIDTH), jnp.float32)],
        compiler_params=pltpu.CompilerParams(
            dimension_semantics=("arbitrary", "arbitrary"), vmem_limit_bytes=VMEM_LIMIT),
        name="pool_out_proj",
    )(x, attn, u, u, gate_pool, w_pool, b_pool, pool_scale, w_out, final_gain)


def _rope_tables(seq):
    pos = jnp.arange(seq, dtype=jnp.float32)
    inv_freq = 1.0 / (ROPE_THETA ** (jnp.arange(0, HEAD_DIM, 2, dtype=jnp.float32) / HEAD_DIM))
    ang = pos[:, None] * inv_freq[None, :]
    cos, sin = jnp.cos(ang), jnp.sin(ang)
    reps = LANES // (HEAD_DIM // 2)
    cos_t = jnp.tile(cos, (1, reps))
    sin_t = jnp.tile(jnp.concatenate([-sin, sin], axis=1), (1, HEADS_PER_TILE))
    return cos_t, sin_t


def kernel(x, norm_gain, w_in, w_pool, b_pool, pool_scale, w_out, final_gain):
    batch, seq, d_model = x.shape
    depth = w_in.shape[0]
    assert d_model == D_MODEL and seq % ROW_TILE == 0 and seq // MOBA_BLOCK <= LANES
    n_blocks = seq // MOBA_BLOCK
    cos_t, sin_t = _rope_tables(seq)
    onehot = (jnp.arange(seq)[:, None] // MOBA_BLOCK == jnp.arange(LANES)[None, :]).astype(jnp.bfloat16)

    for l in range(depth):
        qt, k, vt, gate_attn, u, gate_pool, kmean = _in_proj(
            x.reshape(batch * seq, d_model), norm_gain[l][None, :], w_in[l].astype(jnp.bfloat16),
            cos_t, sin_t, seq)
        shape3 = lambda t: t.reshape(batch, seq, t.shape[-1])
        per_block = lambda t: t.reshape(batch, n_blocks, HEAD_TILES, t.shape[-2], MOBA_BLOCK)
        attn = _moba_attention(per_block(qt), shape3(k), per_block(vt), onehot,
                               kmean.reshape(batch, n_blocks, ATTN_WIDTH), shape3(gate_attn))
        x = _out_proj(x, attn, shape3(u), shape3(gate_pool), w_pool[l].astype(jnp.bfloat16),
                      b_pool[l][:, None, :], pool_scale[l][None, :], w_out[l].astype(jnp.bfloat16),
                      final_gain[None, :], apply_final_norm=(l == depth - 1))
    return x
```

```python
import functools

import jax
import jax.numpy as jnp
from jax import lax
from jax.experimental import pallas as pl
from jax.experimental.pallas import tpu as pltpu

D_MODEL = 1024
ATTN_WIDTH = D_MODEL // 2
POOL_WIDTH = D_MODEL - ATTN_WIDTH
HEAD_DIM = 64
MOBA_BLOCK = 256
MOBA_TOPK = 3
POOL_WINDOWS = (2, 4, 8, 16)
POOL_GROUP = POOL_WIDTH // len(POOL_WINDOWS)
ROPE_THETA = 10000.0
EPS = 1e-6
IN_WIDTH = 4 * ATTN_WIDTH + 2 * POOL_WIDTH

LANES = 128
SUBLANES = 8
HEADS_PER_TILE = LANES // HEAD_DIM
HEAD_TILES = ATTN_WIDTH // LANES
ROW_TILE = 512
POOL_HALO = 16
MASK_BIAS = -1e30
KEY_UNROLL = 8
BF16_ROWS = 16
VT_HEAD_ROWS = HEAD_DIM + BF16_ROWS
VT_ROWS = HEADS_PER_TILE * VT_HEAD_ROWS
Q_SCALE = HEAD_DIM ** -0.5 * 1.4426950408889634
VMEM_LIMIT = 56 * 1024 * 1024


def _silu(t):
    return t * (1.0 / (1.0 + jnp.exp(-t)))


def _rope(t, cos, sin_signed, first_half):
    swapped = jnp.where(first_half, pltpu.roll(t, LANES - HEAD_DIM // 2, axis=1),
                        pltpu.roll(t, HEAD_DIM // 2, axis=1))
    return t * cos + swapped * sin_signed


def _in_proj_kernel(x_ref, gain_ref, w_ref, cos_ref, sin_ref,
                    qt_ref, k_ref, vt_ref, ga_ref, u_ref, gp_ref, kmean_ref):
    x = x_ref[...]
    inv = lax.rsqrt(jnp.mean(x * x, axis=-1, keepdims=True) + EPS)
    h = (x * inv * gain_ref[...]).astype(jnp.bfloat16)

    def proj(c):
        return jnp.dot(h, w_ref[:, c * ATTN_WIDTH:(c + 1) * ATTN_WIDTH],
                       preferred_element_type=jnp.float32)

    cos = cos_ref[...]
    sin = sin_ref[...]
    lane = lax.broadcasted_iota(jnp.int32, (ROW_TILE, LANES), 1)
    first_half = (lane % HEAD_DIM) < (HEAD_DIM // 2)

    q = proj(0)
    k = proj(1)
    v = proj(2)
    for t in range(HEAD_TILES):
        sl = slice(t * LANES, (t + 1) * LANES)
        qr = _rope(q[:, sl], cos, sin, first_half) * Q_SCALE
        kr = _rope(k[:, sl], cos, sin, first_half)
        k_ref[:, sl] = kr.astype(k_ref.dtype)
        for b in range(ROW_TILE // MOBA_BLOCK):
            blk_rows = slice(b * MOBA_BLOCK, (b + 1) * MOBA_BLOCK)
            kmean_ref[b, :, sl] = jnp.sum(kr[blk_rows], axis=0, keepdims=True) * (1.0 / MOBA_BLOCK)
            qt_ref[b, t] = qr[blk_rows].T.astype(qt_ref.dtype)
            v_t = v[blk_rows, sl].T.astype(vt_ref.dtype)
            for head in range(HEADS_PER_TILE):
                base = head * VT_HEAD_ROWS
                vt_ref[b, t, base:base + HEAD_DIM] = v_t[head * HEAD_DIM:(head + 1) * HEAD_DIM]
                vt_ref[b, t, base + HEAD_DIM:base + VT_HEAD_ROWS] = jnp.ones((BF16_ROWS, MOBA_BLOCK),
                                                                             vt_ref.dtype)
    ga_ref[...] = proj(3)
    u_ref[...] = proj(4)
    gp_ref[...] = proj(5)


def _in_proj(x2, gain, w_bf16, cos_t, sin_t, seq):
    rows = x2.shape[0]
    n_tiles = rows // ROW_TILE
    seq_tiles = seq // ROW_TILE
    row_spec = lambda w: pl.BlockSpec((ROW_TILE, w), lambda i: (i, 0))
    tab_spec = pl.BlockSpec((ROW_TILE, LANES), lambda i: (i % seq_tiles, 0))
    blocks_per_tile = ROW_TILE // MOBA_BLOCK
    n_blocks = rows // MOBA_BLOCK
    transposed = lambda r: jax.ShapeDtypeStruct((n_blocks, HEAD_TILES, r, MOBA_BLOCK), jnp.bfloat16)
    transposed_spec = lambda r: pl.BlockSpec((blocks_per_tile, HEAD_TILES, r, MOBA_BLOCK),
                                             lambda i: (i, 0, 0, 0))
    out_shape = (
        transposed(LANES),
        jax.ShapeDtypeStruct((rows, ATTN_WIDTH), jnp.bfloat16),
        transposed(VT_ROWS),
        jax.ShapeDtypeStruct((rows, ATTN_WIDTH), jnp.float32),
        jax.ShapeDtypeStruct((rows, POOL_WIDTH), jnp.float32),
        jax.ShapeDtypeStruct((rows, POOL_WIDTH), jnp.float32),
        jax.ShapeDtypeStruct((n_blocks, 1, ATTN_WIDTH), jnp.float32),
    )
    return pl.pallas_call(
        _in_proj_kernel,
        grid=(n_tiles,),
        in_specs=[
            row_spec(D_MODEL),
            pl.BlockSpec((1, D_MODEL), lambda i: (0, 0)),
            pl.BlockSpec((D_MODEL, IN_WIDTH), lambda i: (0, 0)),
            tab_spec, tab_spec,
        ],
        out_specs=(
            transposed_spec(LANES), row_spec(ATTN_WIDTH), transposed_spec(VT_ROWS),
            row_spec(ATTN_WIDTH), row_spec(POOL_WIDTH), row_spec(POOL_WIDTH),
            pl.BlockSpec((blocks_per_tile, 1, ATTN_WIDTH), lambda i: (i, 0, 0)),
        ),
        out_shape=out_shape,
        compiler_params=pltpu.CompilerParams(
            dimension_semantics=("arbitrary",), vmem_limit_bytes=VMEM_LIMIT),
        name="in_proj",
    )(x2, gain, w_bf16, cos_t, sin_t)


def _select_bias_t(gate_t, n_past):
    slot = lax.broadcasted_iota(jnp.int32, gate_t.shape, 0)
    slot_f = slot.astype(jnp.float32)
    neg_inf = jnp.float32(-jnp.inf)
    g = jnp.where(slot < n_past, gate_t, neg_inf)
    bias = jnp.full(gate_t.shape, MASK_BIAS, dtype=jnp.float32)
    for _ in range(MOBA_TOPK):
        best = jnp.max(g, axis=0, keepdims=True)
        first = jnp.min(jnp.where(g == best, slot_f, float(gate_t.shape[0])), axis=0, keepdims=True)
        pick = (slot_f == first) & (best > neg_inf)
        bias = jnp.where(pick, 0.0, bias)
        g = jnp.where(pick, neg_inf, g)
    return bias


def _fold_rows(t, op):
    return op(t.reshape(t.shape[0] // SUBLANES, SUBLANES, t.shape[1]), axis=0)


def _moba_kernel(qt_ref, k_ref, vt_ref, onehot_ref, kmean_ref, ga_ref, o_ref,
                 qaug_ref, s_ref, own_ref, mfin_ref):
    t = pl.program_id(2)
    blk = MOBA_BLOCK
    n_slots = kmean_ref.shape[0]
    j = jnp.minimum(t, n_slots - 1)
    j_prev = jnp.maximum(t - 1, 0)
    groups_new = jnp.where(t < n_slots, (t + KEY_UNROLL - 1) // KEY_UNROLL, 0)
    groups_old = (j_prev + KEY_UNROLL - 1) // KEY_UNROLL
    groups_both = jnp.minimum(groups_new, groups_old)

    @pl.when(t == 0)
    def _():
        own_ref[...] = jnp.zeros_like(own_ref)
        mfin_ref[...] = jnp.zeros_like(mfin_ref)

    m_prev = [mfin_ref[h, :1] for h in range(HEADS_PER_TILE)]

    def weighted_values(s, n, h):
        p = jnp.exp2(s - m_prev[h]).astype(jnp.bfloat16)
        v_aug = vt_ref[n, h * VT_HEAD_ROWS:(h + 1) * VT_HEAD_ROWS, :]
        r = jnp.dot(v_aug, p, preferred_element_type=jnp.float32)
        return r[:HEAD_DIM], r[HEAD_DIM:HEAD_DIM + SUBLANES]

    acc_init = [weighted_values(own_ref[h], j_prev, h) for h in range(HEADS_PER_TILE)]

    qt = qt_ref[...].astype(jnp.float32)
    feat = lax.broadcasted_iota(jnp.int32, (LANES, blk), 0)
    km = kmean_ref[...]
    km_hi = km.astype(jnp.bfloat16)
    km_lo = (km - km_hi.astype(jnp.float32)).astype(jnp.bfloat16)
    key_pos = lax.broadcasted_iota(jnp.int32, (blk, blk), 0)
    qry_pos = lax.broadcasted_iota(jnp.int32, (blk, blk), 1)
    causal = key_pos <= qry_pos
    k_own = k_ref[pl.ds(pl.multiple_of(j * blk, blk), blk), :]

    m_init = []
    for h in range(HEADS_PER_TILE):
        qh = jnp.where((feat // HEAD_DIM) == h, qt, 0.0).astype(jnp.bfloat16)
        gate_t = (jnp.dot(km_hi, qh, preferred_element_type=jnp.float32)
                  + jnp.dot(km_lo, qh, preferred_element_type=jnp.float32))
        qaug_ref[h, :LANES, :] = qh
        qaug_ref[h, LANES:LANES + n_slots, :] = _select_bias_t(gate_t, j).astype(jnp.bfloat16)
        qaug_ref[h, LANES + n_slots:, :] = jnp.zeros((LANES - n_slots, blk), jnp.bfloat16)
        s = jnp.dot(k_own, qh, preferred_element_type=jnp.float32)
        s = jnp.where(causal, s, -jnp.inf)
        own_ref[h] = s
        m_init.append(_fold_rows(s, jnp.max))

    def group_body(finish_old, score_new):
        def body(g, carry):
            m_run, acc = [list(c) for c in carry]
            for i in range(KEY_UNROLL):
                n = g * KEY_UNROLL + i
                rows = pl.ds(pl.multiple_of(n * blk, blk), blk)
                if score_new:
                    k_aug = jnp.concatenate([k_ref[rows, :], onehot_ref[rows, :]], axis=1)
                for h in range(HEADS_PER_TILE):
                    if finish_old:
                        o, l = weighted_values(s_ref[h, n], n, h)
                        acc[h] = (acc[h][0] + o, acc[h][1] + l)
                    if score_new:
                        s = jnp.dot(k_aug, qaug_ref[h], preferred_element_type=jnp.float32)
                        s_ref[h, n] = s
                        m_run[h] = jnp.maximum(m_run[h], _fold_rows(s, jnp.max))
            return tuple(m_run), tuple(acc)
        return body

    carry = (tuple(m_init), tuple(acc_init))
    carry = lax.fori_loop(0, groups_both, group_body(True, True), carry)
    carry = lax.fori_loop(groups_both, groups_new, group_body(False, True), carry)
    carry = lax.fori_loop(groups_both, groups_old, group_body(True, False), carry)
    m_run, acc = carry

    for h in range(HEADS_PER_TILE):
        mfin_ref[h] = jnp.broadcast_to(jnp.max(m_run[h], axis=0, keepdims=True), (SUBLANES, blk))

    @pl.when(t > 0)
    def _():
        out_t = jnp.concatenate([o / l[:1] for o, l in acc], axis=0)
        o_ref[...] = (out_t.T * _silu(ga_ref[...])).astype(o_ref.dtype)


def _moba_attention(qt, k, vt, onehot, kmean, gate_attn):
    batch, seq, _ = k.shape
    n_blocks = seq // MOBA_BLOCK
    slots = -(-(n_blocks - 1) // KEY_UNROLL) * KEY_UNROLL
    finished_spec = pl.BlockSpec((None, MOBA_BLOCK, LANES), lambda b, hp, t: (b, jnp.maximum(t - 1, 0), hp))
    return pl.pallas_call(
        _moba_kernel,
        grid=(batch, HEAD_TILES, n_blocks + 1),
        in_specs=[
            pl.BlockSpec((None, None, None, LANES, MOBA_BLOCK),
                         lambda b, hp, t: (b, jnp.minimum(t, n_blocks - 1), hp, 0, 0)),
            pl.BlockSpec((None, seq, LANES), lambda b, hp, t: (b, 0, hp)),
            pl.BlockSpec((None, n_blocks, None, VT_ROWS, MOBA_BLOCK), lambda b, hp, t: (b, 0, hp, 0, 0)),
            pl.BlockSpec((seq, LANES), lambda b, hp, t: (0, 0)),
            pl.BlockSpec((None, n_blocks, LANES), lambda b, hp, t: (b, 0, hp)),
            finished_spec,
        ],
        out_specs=finished_spec,
        out_shape=jax.ShapeDtypeStruct((batch, seq, ATTN_WIDTH), jnp.bfloat16),
        scratch_shapes=[
            pltpu.VMEM((HEADS_PER_TILE, 2 * LANES, MOBA_BLOCK), jnp.bfloat16),
            pltpu.VMEM((HEADS_PER_TILE, slots, MOBA_BLOCK, MOBA_BLOCK), jnp.float32),
            pltpu.VMEM((HEADS_PER_TILE, MOBA_BLOCK, MOBA_BLOCK), jnp.float32),
            pltpu.VMEM((HEADS_PER_TILE, SUBLANES, MOBA_BLOCK), jnp.float32),
        ],
        compiler_params=pltpu.CompilerParams(
            dimension_semantics=("arbitrary", "arbitrary", "arbitrary"),
            vmem_limit_bytes=VMEM_LIMIT),
        name="moba_attention",
    )(qt, k, vt, onehot, kmean, gate_attn)


def _out_proj_kernel(apply_final_norm, x_ref, attn_ref, u_ref, uprev_ref, gp_ref, wpool_ref, bpool_ref,
                     pscale_ref, wout_ref, fgain_ref, o_ref, ext_ref):
    i = pl.program_id(1)
    halo = uprev_ref[...]
    ext_ref[:POOL_HALO, :] = jnp.where(i > 0, halo, jnp.zeros_like(halo))
    ext_ref[POOL_HALO:, :] = u_ref[...]
    pos = i * ROW_TILE + lax.broadcasted_iota(jnp.int32, (ROW_TILE, 1), 0)

    pooled_parts = []
    for g, window in enumerate(POOL_WINDOWS):
        sl = slice(g * POOL_GROUP, (g + 1) * POOL_GROUP)
        u_g = ext_ref[POOL_HALO:, sl]
        total = u_g
        for t in range(1, window):
            total = total + ext_ref[POOL_HALO - t:POOL_HALO - t + ROW_TILE, sl]
        count = jnp.minimum(pos + 1, window).astype(jnp.float32)
        pooled = total / count - u_g
        y = jnp.dot(pooled.astype(jnp.bfloat16), wpool_ref[g], preferred_element_type=jnp.float32)
        pooled_parts.append(y + bpool_ref[g])
    pool = jnp.concatenate(pooled_parts, axis=1) * pscale_ref[...]
    pool = (pool * _silu(gp_ref[...])).astype(jnp.bfloat16)

    mixed = jnp.concatenate([attn_ref[...], pool], axis=1)
    y = x_ref[...] + jnp.dot(mixed, wout_ref[...], preferred_element_type=jnp.float32)
    if apply_final_norm:
        inv = lax.rsqrt(jnp.mean(y * y, axis=-1, keepdims=True) + EPS)
        y = y * inv * fgain_ref[...]
    o_ref[...] = y


def _out_proj(x, attn, u, gate_pool, w_pool, b_pool, pool_scale, w_out, final_gain, apply_final_norm):
    batch, seq, _ = x.shape
    tiles = seq // ROW_TILE
    halo_per_tile = ROW_TILE // POOL_HALO
    row_spec = lambda w: pl.BlockSpec((None, ROW_TILE, w), lambda b, i: (b, i, 0))
    const2 = lambda shape: pl.BlockSpec(shape, lambda b, i: (0, 0))
    const3 = lambda shape: pl.BlockSpec(shape, lambda b, i: (0, 0, 0))
    return pl.pallas_call(
        functools.partial(_out_proj_kernel, apply_final_norm),
        grid=(batch, tiles),
        in_specs=[
            row_spec(D_MODEL),
            row_spec(ATTN_WIDTH),
            row_spec(POOL_WIDTH),
            pl.BlockSpec((None, POOL_HALO, POOL_WIDTH),
                         lambda b, i: (b, jnp.maximum(i * halo_per_tile - 1, 0), 0)),
            row_spec(POOL_WIDTH),
            const3((len(POOL_WINDOWS), POOL_GROUP, POOL_GROUP)),
            const3((len(POOL_WINDOWS), 1, POOL_GROUP)),
            const2((1, POOL_WIDTH)),
            const2((D_MODEL, D_MODEL)),
            const2((1, D_MODEL)),
        ],
        out_specs=row_spec(D_MODEL),
        out_shape=jax.ShapeDtypeStruct((batch, seq, D_MODEL), jnp.float32),
        scratch_shapes=[pltpu.VMEM((POOL_HALO + ROW_TILE, POOL_WIDTH), jnp.float32)],
        compiler_params=pltpu.CompilerParams(
            dimension_semantics=("arbitrary", "arbitrary"), vmem_limit_bytes=VMEM_LIMIT),
        name="pool_out_proj",
    )(x, attn, u, u, gate_pool, w_pool, b_pool, pool_scale, w_out, final_gain)


def _rope_tables(seq):
    pos = jnp.arange(seq, dtype=jnp.float32)
    inv_freq = 1.0 / (ROPE_THETA ** (jnp.arange(0, HEAD_DIM, 2, dtype=jnp.float32) / HEAD_DIM))
    ang = pos[:, None] * inv_freq[None, :]
    cos, sin = jnp.cos(ang), jnp.sin(ang)
    reps = LANES // (HEAD_DIM // 2)
    cos_t = jnp.tile(cos, (1, reps))
    sin_t = jnp.tile(jnp.concatenate([-sin, sin], axis=1), (1, HEADS_PER_TILE))
    return cos_t, sin_t


def kernel(x, norm_gain, w_in, w_pool, b_pool, pool_scale, w_out, final_gain):
    batch, seq, d_model = x.shape
    depth = w_in.shape[0]
    assert d_model == D_MODEL and seq % ROW_TILE == 0 and seq // MOBA_BLOCK <= LANES
    n_blocks = seq // MOBA_BLOCK
    cos_t, sin_t = _rope_tables(seq)
    onehot = (jnp.arange(seq)[:, None] // MOBA_BLOCK == jnp.arange(LANES)[None, :]).astype(jnp.bfloat16)

    for l in range(depth):
        qt, k, vt, gate_attn, u, gate_pool, kmean = _in_proj(
            x.reshape(batch * seq, d_model), norm_gain[l][None, :], w_in[l].astype(jnp.bfloat16),
            cos_t, sin_t, seq)
        shape3 = lambda t: t.reshape(batch, seq, t.shape[-1])
        per_block = lambda t: t.reshape(batch, n_blocks, HEAD_TILES, t.shape[-2], MOBA_BLOCK)
        attn = _moba_attention(per_block(qt), shape3(k), per_block(vt), onehot,
                               kmean.reshape(batch, n_blocks, ATTN_WIDTH), shape3(gate_attn))
        x = _out_proj(x, attn, shape3(u), shape3(gate_pool), w_pool[l].astype(jnp.bfloat16),
                      b_pool[l][:, None, :], pool_scale[l][None, :], w_out[l].astype(jnp.bfloat16),
                      final_gain[None, :], apply_final_norm=(l == depth - 1))
    return x
```

```python
import functools

import numpy as np

import jax
import jax.numpy as jnp
from jax import lax
from jax.experimental import pallas as pl
from jax.experimental.pallas import tpu as pltpu

D_MODEL = 1024
ATTN_WIDTH = D_MODEL // 2
POOL_WIDTH = D_MODEL - ATTN_WIDTH
HEAD_DIM = 64
MOBA_BLOCK = 256
MOBA_TOPK = 3
POOL_WINDOWS = (2, 4, 8, 16)
POOL_GROUP = POOL_WIDTH // len(POOL_WINDOWS)
ROPE_THETA = 10000.0
EPS = 1e-6
IN_WIDTH = 4 * ATTN_WIDTH + 2 * POOL_WIDTH

LANES = 128
SUBLANES = 8
HEADS_PER_TILE = LANES // HEAD_DIM
HEAD_TILES = ATTN_WIDTH // LANES
ROW_TILE = 512
POOL_HALO = 16
MASK_BIAS = -1e30
TILES_PER_ITER = 8
BF16_ROWS = 16
VT_HEAD_ROWS = HEAD_DIM + BF16_ROWS
VT_ROWS = HEADS_PER_TILE * VT_HEAD_ROWS
ACC_ROWS = HEAD_DIM + SUBLANES
Q_SCALE = HEAD_DIM ** -0.5 * 1.4426950408889634
VMEM_LIMIT = 56 * 1024 * 1024


def _silu(t):
    return t * (1.0 / (1.0 + jnp.exp(-t)))


def _rope(t, cos, sin_signed, first_half):
    swapped = jnp.where(first_half, pltpu.roll(t, LANES - HEAD_DIM // 2, axis=1),
                        pltpu.roll(t, HEAD_DIM // 2, axis=1))
    return t * cos + swapped * sin_signed


def _in_proj_kernel(x_ref, gain_ref, w_ref, cos_ref, sin_ref,
                    qt_ref, k_ref, vt_ref, ga_ref, u_ref, gp_ref, kmean_ref):
    x = x_ref[...]
    inv = lax.rsqrt(jnp.mean(x * x, axis=-1, keepdims=True) + EPS)
    h = (x * inv * gain_ref[...]).astype(jnp.bfloat16)

    def proj(c):
        return jnp.dot(h, w_ref[:, c * ATTN_WIDTH:(c + 1) * ATTN_WIDTH],
                       preferred_element_type=jnp.float32)

    cos = cos_ref[...]
    sin = sin_ref[...]
    lane = lax.broadcasted_iota(jnp.int32, (ROW_TILE, LANES), 1)
    first_half = (lane % HEAD_DIM) < (HEAD_DIM // 2)

    q = proj(0)
    k = proj(1)
    v = proj(2)
    for t in range(HEAD_TILES):
        sl = slice(t * LANES, (t + 1) * LANES)
        qr = _rope(q[:, sl], cos, sin, first_half) * Q_SCALE
        kr = _rope(k[:, sl], cos, sin, first_half)
        k_ref[:, sl] = kr.astype(k_ref.dtype)
        for b in range(ROW_TILE // MOBA_BLOCK):
            blk_rows = slice(b * MOBA_BLOCK, (b + 1) * MOBA_BLOCK)
            kmean_ref[b, :, sl] = jnp.sum(kr[blk_rows], axis=0, keepdims=True) * (1.0 / MOBA_BLOCK)
            qt_ref[b, t] = qr[blk_rows].T.astype(qt_ref.dtype)
            v_t = v[blk_rows, sl].T.astype(vt_ref.dtype)
            for head in range(HEADS_PER_TILE):
                base = head * VT_HEAD_ROWS
                vt_ref[b, t, base:base + HEAD_DIM] = v_t[head * HEAD_DIM:(head + 1) * HEAD_DIM]
                vt_ref[b, t, base + HEAD_DIM:base + VT_HEAD_ROWS] = jnp.ones((BF16_ROWS, MOBA_BLOCK),
                                                                             vt_ref.dtype)
    ga_ref[...] = proj(3)
    u_ref[...] = proj(4)
    gp_ref[...] = proj(5)


def _in_proj(x2, gain, w_bf16, cos_t, sin_t, seq):
    rows = x2.shape[0]
    n_tiles = rows // ROW_TILE
    seq_tiles = seq // ROW_TILE
    row_spec = lambda w: pl.BlockSpec((ROW_TILE, w), lambda i: (i, 0))
    tab_spec = pl.BlockSpec((ROW_TILE, LANES), lambda i: (i % seq_tiles, 0))
    blocks_per_tile = ROW_TILE // MOBA_BLOCK
    n_blocks = rows // MOBA_BLOCK
    transposed = lambda r: jax.ShapeDtypeStruct((n_blocks, HEAD_TILES, r, MOBA_BLOCK), jnp.bfloat16)
    transposed_spec = lambda r: pl.BlockSpec((blocks_per_tile, HEAD_TILES, r, MOBA_BLOCK),
                                             lambda i: (i, 0, 0, 0))
    out_shape = (
        transposed(LANES),
        jax.ShapeDtypeStruct((rows, ATTN_WIDTH), jnp.bfloat16),
        transposed(VT_ROWS),
        jax.ShapeDtypeStruct((rows, ATTN_WIDTH), jnp.float32),
        jax.ShapeDtypeStruct((rows, POOL_WIDTH), jnp.float32),
        jax.ShapeDtypeStruct((rows, POOL_WIDTH), jnp.float32),
        jax.ShapeDtypeStruct((n_blocks, 1, ATTN_WIDTH), jnp.float32),
    )
    return pl.pallas_call(
        _in_proj_kernel,
        grid=(n_tiles,),
        in_specs=[
            row_spec(D_MODEL),
            pl.BlockSpec((1, D_MODEL), lambda i: (0, 0)),
            pl.BlockSpec((D_MODEL, IN_WIDTH), lambda i: (0, 0)),
            tab_spec, tab_spec,
        ],
        out_specs=(
            transposed_spec(LANES), row_spec(ATTN_WIDTH), transposed_spec(VT_ROWS),
            row_spec(ATTN_WIDTH), row_spec(POOL_WIDTH), row_spec(POOL_WIDTH),
            pl.BlockSpec((blocks_per_tile, 1, ATTN_WIDTH), lambda i: (i, 0, 0)),
        ),
        out_shape=out_shape,
        compiler_params=pltpu.CompilerParams(
            dimension_semantics=("arbitrary",), vmem_limit_bytes=VMEM_LIMIT),
        name="in_proj",
    )(x2, gain, w_bf16, cos_t, sin_t)


def _tile_tables(n_blocks):
    pairs = [(t, n) for t in range(n_blocks) for n in range(t + 1)]
    return (np.asarray([p[0] for p in pairs], np.int32), np.asarray([p[1] for p in pairs], np.int32))


def _pass_lag(n_blocks):
    return -(-(TILES_PER_ITER + n_blocks - 1) // TILES_PER_ITER)


def _select_bias_t(gate_t, own):
    slot = lax.broadcasted_iota(jnp.int32, gate_t.shape, 0)
    slot_f = slot.astype(jnp.float32)
    neg_inf = jnp.float32(-jnp.inf)
    g = jnp.where(slot < own, gate_t, neg_inf)
    bias = jnp.where(slot == own, 0.0, MASK_BIAS)
    for _ in range(MOBA_TOPK):
        best = jnp.max(g, axis=0, keepdims=True)
        first = jnp.min(jnp.where(g == best, slot_f, float(gate_t.shape[0])), axis=0, keepdims=True)
        pick = (slot_f == first) & (best > neg_inf)
        bias = jnp.where(pick, 0.0, bias)
        g = jnp.where(pick, neg_inf, g)
    return bias


def _fold_rows(t, op):
    return op(t.reshape(t.shape[0] // SUBLANES, SUBLANES, t.shape[1]), axis=0)


def _moba_kernel(tq_ref, tn_ref, qt_ref, k_ref, vt_ref, onehot_ref, kmean_ref, o_ref,
                 qaug_ref, ring_ref, m_ref, acc_ref):
    blk = MOBA_BLOCK
    n_blocks = kmean_ref.shape[0]
    n_iters = tq_ref.shape[0] // TILES_PER_ITER
    lag = _pass_lag(n_blocks)

    feat = lax.broadcasted_iota(jnp.int32, (LANES, blk), 0)
    km = kmean_ref[...]
    km_hi = km.astype(jnp.bfloat16)
    km_lo = (km - km_hi.astype(jnp.float32)).astype(jnp.bfloat16)

    def prepare(t, carry):
        qt = qt_ref[t].astype(jnp.float32)
        for h in range(HEADS_PER_TILE):
            qh = jnp.where((feat // HEAD_DIM) == h, qt, 0.0).astype(jnp.bfloat16)
            gate_t = (jnp.dot(km_hi, qh, preferred_element_type=jnp.float32)
                      + jnp.dot(km_lo, qh, preferred_element_type=jnp.float32))
            qaug_ref[t, h, :LANES, :] = qh
            qaug_ref[t, h, LANES:LANES + n_blocks, :] = _select_bias_t(gate_t, t).astype(jnp.bfloat16)
            qaug_ref[t, h, LANES + n_blocks:, :] = jnp.zeros((LANES - n_blocks, blk), jnp.bfloat16)
            m_ref[t, h] = jnp.full((SUBLANES, blk), -jnp.inf, jnp.float32)
            acc_ref[t, h] = jnp.zeros((ACC_ROWS, blk), jnp.float32)
        return carry

    lax.fori_loop(0, n_blocks, prepare, 0)

    key_pos = lax.broadcasted_iota(jnp.int32, (blk, blk), 0)
    qry_pos = lax.broadcasted_iota(jnp.int32, (blk, blk), 1)
    causal = key_pos <= qry_pos

    def score_tile(tile, slot):
        t = tq_ref[tile]
        n = tn_ref[tile]
        rows = pl.ds(pl.multiple_of(n * blk, blk), blk)
        k_aug = jnp.concatenate([k_ref[rows, :], onehot_ref[rows, :]], axis=1)
        keep = jnp.logical_or(causal, n < t)
        for h in range(HEADS_PER_TILE):
            s = jnp.dot(k_aug, qaug_ref[t, h], preferred_element_type=jnp.float32)
            s = jnp.where(keep, s, -jnp.inf)
            ring_ref[slot, h] = s
            m_ref[t, h] = jnp.maximum(m_ref[t, h], _fold_rows(s, jnp.max))

    def value_tile(tile, slot, m_fin):
        t = tq_ref[tile]
        n = tn_ref[tile]
        for h in range(HEADS_PER_TILE):
            p = jnp.exp2(ring_ref[slot, h] - m_fin[h]).astype(jnp.bfloat16)
            v_aug = vt_ref[n, h * VT_HEAD_ROWS:(h + 1) * VT_HEAD_ROWS, :]
            r = jnp.dot(v_aug, p, preferred_element_type=jnp.float32)
            acc_ref[t, h] += r[:ACC_ROWS]

    def iteration(score, finish):
        def body(i, carry):
            base = lax.rem(i, lag) * TILES_PER_ITER
            if finish:
                done = (i - lag) * TILES_PER_ITER
                m_fin = [[jnp.max(m_ref[tq_ref[done + u], h], axis=0, keepdims=True)
                          for h in range(HEADS_PER_TILE)] for u in range(TILES_PER_ITER)]
            for u in range(TILES_PER_ITER):
                if finish:
                    value_tile(done + u, base + u, m_fin[u])
                if score:
                    score_tile(i * TILES_PER_ITER + u, base + u)
            return carry
        return body

    lax.fori_loop(0, lag, iteration(True, False), 0)
    lax.fori_loop(lag, n_iters, iteration(True, True), 0)
    lax.fori_loop(n_iters, n_iters + lag, iteration(False, True), 0)

    def finish_block(t, carry):
        heads = []
        for h in range(HEADS_PER_TILE):
            a = acc_ref[t, h]
            heads.append(a[:HEAD_DIM] / a[HEAD_DIM:HEAD_DIM + 1])
        out_t = jnp.concatenate(heads, axis=0)
        o_ref[pl.ds(pl.multiple_of(t * blk, blk), blk), :] = out_t.T.astype(o_ref.dtype)
        return carry

    lax.fori_loop(0, n_blocks, finish_block, 0)


def _moba_attention(qt, k, vt, onehot, kmean):
    batch, seq, _ = k.shape
    n_blocks = seq // MOBA_BLOCK
    tile_q, tile_n = _tile_tables(n_blocks)
    assert tile_q.shape[0] % TILES_PER_ITER == 0
    ring_tiles = _pass_lag(n_blocks) * TILES_PER_ITER
    once = dict(pipeline_mode=pl.Buffered(1))
    grid_spec = pltpu.PrefetchScalarGridSpec(
        num_scalar_prefetch=2,
        grid=(batch, HEAD_TILES),
        in_specs=[
            pl.BlockSpec((None, n_blocks, None, LANES, MOBA_BLOCK), lambda b, hp, tq, tn: (b, 0, hp, 0, 0),
                         **once),
            pl.BlockSpec((None, seq, LANES), lambda b, hp, tq, tn: (b, 0, hp), **once),
            pl.BlockSpec((None, n_blocks, None, VT_ROWS, MOBA_BLOCK), lambda b, hp, tq, tn: (b, 0, hp, 0, 0),
                         **once),
            pl.BlockSpec((seq, LANES), lambda b, hp, tq, tn: (0, 0), **once),
            pl.BlockSpec((None, n_blocks, LANES), lambda b, hp, tq, tn: (b, 0, hp)),
        ],
        out_specs=pl.BlockSpec((None, seq, LANES), lambda b, hp, tq, tn: (b, 0, hp)),
        scratch_shapes=[
            pltpu.VMEM((n_blocks, HEADS_PER_TILE, 2 * LANES, MOBA_BLOCK), jnp.bfloat16),
            pltpu.VMEM((ring_tiles, HEADS_PER_TILE, MOBA_BLOCK, MOBA_BLOCK), jnp.float32),
            pltpu.VMEM((n_blocks, HEADS_PER_TILE, SUBLANES, MOBA_BLOCK), jnp.float32),
            pltpu.VMEM((n_blocks, HEADS_PER_TILE, ACC_ROWS, MOBA_BLOCK), jnp.float32),
        ],
    )
    return pl.pallas_call(
        _moba_kernel,
        grid_spec=grid_spec,
        out_shape=jax.ShapeDtypeStruct((batch, seq, ATTN_WIDTH), jnp.bfloat16),
        compiler_params=pltpu.CompilerParams(
            dimension_semantics=("arbitrary", "arbitrary"), vmem_limit_bytes=VMEM_LIMIT),
        name="moba_attention",
    )(jnp.asarray(tile_q), jnp.asarray(tile_n), qt, k, vt, onehot, kmean)


def _out_proj_kernel(apply_final_norm, x_ref, attn_ref, ga_ref, u_ref, uprev_ref, gp_ref, wpool_ref,
                     bpool_ref, pscale_ref, wout_ref, fgain_ref, o_ref, ext_ref):
    i = pl.program_id(1)
    halo = uprev_ref[...]
    ext_ref[:POOL_HALO, :] = jnp.where(i > 0, halo, jnp.zeros_like(halo))
    ext_ref[POOL_HALO:, :] = u_ref[...]
    pos = i * ROW_TILE + lax.broadcasted_iota(jnp.int32, (ROW_TILE, 1), 0)

    pooled_parts = []
    for g, window in enumerate(POOL_WINDOWS):
        sl = slice(g * POOL_GROUP, (g + 1) * POOL_GROUP)
        u_g = ext_ref[POOL_HALO:, sl]
        total = u_g
        for t in range(1, window):
            total = total + ext_ref[POOL_HALO - t:POOL_HALO - t + ROW_TILE, sl]
        count = jnp.minimum(pos + 1, window).astype(jnp.float32)
        pooled = total / count - u_g
        y = jnp.dot(pooled.astype(jnp.bfloat16), wpool_ref[g], preferred_element_type=jnp.float32)
        pooled_parts.append(y + bpool_ref[g])
    pool = jnp.concatenate(pooled_parts, axis=1) * pscale_ref[...]
    pool = (pool * _silu(gp_ref[...])).astype(jnp.bfloat16)
    attn = (attn_ref[...].astype(jnp.float32) * _silu(ga_ref[...])).astype(jnp.bfloat16)

    mixed = jnp.concatenate([attn, pool], axis=1)
    y = x_ref[...] + jnp.dot(mixed, wout_ref[...], preferred_element_type=jnp.float32)
    if apply_final_norm:
        inv = lax.rsqrt(jnp.mean(y * y, axis=-1, keepdims=True) + EPS)
        y = y * inv * fgain_ref[...]
    o_ref[...] = y


def _out_proj(x, attn, gate_attn, u, gate_pool, w_pool, b_pool, pool_scale, w_out, final_gain,
              apply_final_norm):
    batch, seq, _ = x.shape
    tiles = seq // ROW_TILE
    halo_per_tile = ROW_TILE // POOL_HALO
    row_spec = lambda w: pl.BlockSpec((None, ROW_TILE, w), lambda b, i: (b, i, 0))
    const2 = lambda shape: pl.BlockSpec(shape, lambda b, i: (0, 0))
    const3 = lambda shape: pl.BlockSpec(shape, lambda b, i: (0, 0, 0))
    return pl.pallas_call(
        functools.partial(_out_proj_kernel, apply_final_norm),
        grid=(batch, tiles),
        in_specs=[
            row_spec(D_MODEL),
            row_spec(ATTN_WIDTH),
            row_spec(ATTN_WIDTH),
            row_spec(POOL_WIDTH),
            pl.BlockSpec((None, POOL_HALO, POOL_WIDTH),
                         lambda b, i: (b, jnp.maximum(i * halo_per_tile - 1, 0), 0)),
            row_spec(POOL_WIDTH),
            const3((len(POOL_WINDOWS), POOL_GROUP, POOL_GROUP)),
            const3((len(POOL_WINDOWS), 1, POOL_GROUP)),
            const2((1, POOL_WIDTH)),
            const2((D_MODEL, D_MODEL)),
            const2((1, D_MODEL)),
        ],
        out_specs=row_spec(D_MODEL),
        out_shape=jax.ShapeDtypeStruct((batch, seq, D_MODEL), jnp.float32),
        scratch_shapes=[pltpu.VMEM((POOL_HALO + ROW_TILE, POOL_WIDTH), jnp.float32)],
        compiler_params=pltpu.CompilerParams(
            dimension_semantics=("arbitrary", "arbitrary"), vmem_limit_bytes=VMEM_LIMIT),
        name="pool_out_proj",
    )(x, attn, gate_attn, u, u, gate_pool, w_pool, b_pool, pool_scale, w_out, final_gain)


def _rope_tables(seq):
    pos = jnp.arange(seq, dtype=jnp.float32)
    inv_freq = 1.0 / (ROPE_THETA ** (jnp.arange(0, HEAD_DIM, 2, dtype=jnp.float32) / HEAD_DIM))
    ang = pos[:, None] * inv_freq[None, :]
    cos, sin = jnp.cos(ang), jnp.sin(ang)
    reps = LANES // (HEAD_DIM // 2)
    cos_t = jnp.tile(cos, (1, reps))
    sin_t = jnp.tile(jnp.concatenate([-sin, sin], axis=1), (1, HEADS_PER_TILE))
    return cos_t, sin_t


def kernel(x, norm_gain, w_in, w_pool, b_pool, pool_scale, w_out, final_gain):
    batch, seq, d_model = x.shape
    depth = w_in.shape[0]
    assert d_model == D_MODEL and seq % ROW_TILE == 0 and seq // MOBA_BLOCK <= LANES
    n_blocks = seq // MOBA_BLOCK
    cos_t, sin_t = _rope_tables(seq)
    onehot = (jnp.arange(seq)[:, None] // MOBA_BLOCK == jnp.arange(LANES)[None, :]).astype(jnp.bfloat16)

    for l in range(depth):
        qt, k, vt, gate_attn, u, gate_pool, kmean = _in_proj(
            x.reshape(batch * seq, d_model), norm_gain[l][None, :], w_in[l].astype(jnp.bfloat16),
            cos_t, sin_t, seq)
        shape3 = lambda t: t.reshape(batch, seq, t.shape[-1])
        per_block = lambda t: t.reshape(batch, n_blocks, HEAD_TILES, t.shape[-2], MOBA_BLOCK)
        attn = _moba_attention(per_block(qt), shape3(k), per_block(vt), onehot,
                               kmean.reshape(batch, n_blocks, ATTN_WIDTH))
        x = _out_proj(x, attn, shape3(gate_attn), shape3(u), shape3(gate_pool),
                      w_pool[l].astype(jnp.bfloat16), b_pool[l][:, None, :], pool_scale[l][None, :],
                      w_out[l].astype(jnp.bfloat16), final_gain[None, :],
                      apply_final_norm=(l == depth - 1))
    return x
```

```python
import functools

import jax
import jax.numpy as jnp
from jax import lax
from jax.experimental import pallas as pl
from jax.experimental.pallas import tpu as pltpu

D_MODEL = 1024
ATTN_WIDTH = D_MODEL // 2
POOL_WIDTH = D_MODEL - ATTN_WIDTH
HEAD_DIM = 64
MOBA_BLOCK = 256
MOBA_TOPK = 3
POOL_WINDOWS = (2, 4, 8, 16)
POOL_GROUP = POOL_WIDTH // len(POOL_WINDOWS)
ROPE_THETA = 10000.0
EPS = 1e-6
IN_WIDTH = 4 * ATTN_WIDTH + 2 * POOL_WIDTH

LANES = 128
SUBLANES = 8
HEADS_PER_TILE = LANES // HEAD_DIM
HEAD_TILES = ATTN_WIDTH // LANES
ROW_TILE = 512
POOL_HALO = 16
MASK_BIAS = -1e30
KEY_UNROLL = 8
BF16_ROWS = 16
VT_HEAD_ROWS = HEAD_DIM + BF16_ROWS
VT_ROWS = HEADS_PER_TILE * VT_HEAD_ROWS
Q_SCALE = HEAD_DIM ** -0.5 * 1.4426950408889634
VMEM_LIMIT = 56 * 1024 * 1024


def _silu(t):
    return t * (1.0 / (1.0 + jnp.exp(-t)))


def _rope(t, cos, sin_signed, first_half):
    swapped = jnp.where(first_half, pltpu.roll(t, LANES - HEAD_DIM // 2, axis=1),
                        pltpu.roll(t, HEAD_DIM // 2, axis=1))
    return t * cos + swapped * sin_signed


def _in_proj_kernel(seq_tiles, x_ref, gain_ref, w_ref, cos_ref, sin_ref, wpool_ref, bpool_ref, pscale_ref,
                    qt_ref, k_ref, vt_ref, ga_ref, pool_ref, kmean_ref, ext_ref):
    x = x_ref[...]
    inv = lax.rsqrt(jnp.mean(x * x, axis=-1, keepdims=True) + EPS)
    h = (x * inv * gain_ref[...]).astype(jnp.bfloat16)

    def proj(c):
        return jnp.dot(h, w_ref[:, c * ATTN_WIDTH:(c + 1) * ATTN_WIDTH],
                       preferred_element_type=jnp.float32)

    cos = cos_ref[...]
    sin = sin_ref[...]
    lane = lax.broadcasted_iota(jnp.int32, (ROW_TILE, LANES), 1)
    first_half = (lane % HEAD_DIM) < (HEAD_DIM // 2)

    q = proj(0)
    k = proj(1)
    v = proj(2)
    for t in range(HEAD_TILES):
        sl = slice(t * LANES, (t + 1) * LANES)
        qr = _rope(q[:, sl], cos, sin, first_half) * Q_SCALE
        kr = _rope(k[:, sl], cos, sin, first_half)
        k_ref[:, sl] = kr.astype(k_ref.dtype)
        for b in range(ROW_TILE // MOBA_BLOCK):
            blk_rows = slice(b * MOBA_BLOCK, (b + 1) * MOBA_BLOCK)
            kmean_ref[b, :, sl] = jnp.sum(kr[blk_rows], axis=0, keepdims=True) * (1.0 / MOBA_BLOCK)
            qt_ref[b, t] = qr[blk_rows].T.astype(qt_ref.dtype)
            v_t = v[blk_rows, sl].T.astype(vt_ref.dtype)
            for head in range(HEADS_PER_TILE):
                base = head * VT_HEAD_ROWS
                vt_ref[b, t, base:base + HEAD_DIM] = v_t[head * HEAD_DIM:(head + 1) * HEAD_DIM]
                vt_ref[b, t, base + HEAD_DIM:base + VT_HEAD_ROWS] = jnp.ones((BF16_ROWS, MOBA_BLOCK),
                                                                             vt_ref.dtype)
    ga_ref[...] = proj(3)

    tile_in_seq = pl.program_id(0) % seq_tiles

    @pl.when(pl.program_id(0) == 0)
    def _():
        ext_ref[...] = jnp.zeros_like(ext_ref)

    history = ext_ref[ROW_TILE:, :]
    ext_ref[:POOL_HALO, :] = jnp.where(tile_in_seq > 0, history, jnp.zeros_like(history))
    ext_ref[POOL_HALO:, :] = proj(4)
    pos = tile_in_seq * ROW_TILE + lax.broadcasted_iota(jnp.int32, (ROW_TILE, 1), 0)
    gate_pool = _silu(proj(5))
    for g, window in enumerate(POOL_WINDOWS):
        sl = slice(g * POOL_GROUP, (g + 1) * POOL_GROUP)
        u_g = ext_ref[POOL_HALO:, sl]
        total = u_g
        for back in range(1, window):
            total = total + ext_ref[POOL_HALO - back:POOL_HALO - back + ROW_TILE, sl]
        count = jnp.minimum(pos + 1, window).astype(jnp.float32)
        pooled = total / count - u_g
        y = jnp.dot(pooled.astype(jnp.bfloat16), wpool_ref[g], preferred_element_type=jnp.float32)
        y = (y + bpool_ref[g]) * pscale_ref[:, sl]
        pool_ref[:, sl] = (y * gate_pool[:, sl]).astype(pool_ref.dtype)


def _in_proj(x2, gain, w_bf16, cos_t, sin_t, w_pool, b_pool, pool_scale, seq):
    rows = x2.shape[0]
    n_tiles = rows // ROW_TILE
    seq_tiles = seq // ROW_TILE
    row_spec = lambda w: pl.BlockSpec((ROW_TILE, w), lambda i: (i, 0))
    tab_spec = pl.BlockSpec((ROW_TILE, LANES), lambda i: (i % seq_tiles, 0))
    const2 = lambda shape: pl.BlockSpec(shape, lambda i: (0, 0))
    const3 = lambda shape: pl.BlockSpec(shape, lambda i: (0, 0, 0))
    blocks_per_tile = ROW_TILE // MOBA_BLOCK
    n_blocks = rows // MOBA_BLOCK
    transposed = lambda r: jax.ShapeDtypeStruct((n_blocks, HEAD_TILES, r, MOBA_BLOCK), jnp.bfloat16)
    transposed_spec = lambda r: pl.BlockSpec((blocks_per_tile, HEAD_TILES, r, MOBA_BLOCK),
                                             lambda i: (i, 0, 0, 0))
    out_shape = (
        transposed(LANES),
        jax.ShapeDtypeStruct((rows, ATTN_WIDTH), jnp.bfloat16),
        transposed(VT_ROWS),
        jax.ShapeDtypeStruct((rows, ATTN_WIDTH), jnp.float32),
        jax.ShapeDtypeStruct((rows, POOL_WIDTH), jnp.bfloat16),
        jax.ShapeDtypeStruct((n_blocks, 1, ATTN_WIDTH), jnp.float32),
    )
    return pl.pallas_call(
        functools.partial(_in_proj_kernel, seq_tiles),
        grid=(n_tiles,),
        in_specs=[
            row_spec(D_MODEL),
            const2((1, D_MODEL)),
            const2((D_MODEL, IN_WIDTH)),
            tab_spec, tab_spec,
            const3((len(POOL_WINDOWS), POOL_GROUP, POOL_GROUP)),
            const3((len(POOL_WINDOWS), 1, POOL_GROUP)),
            const2((1, POOL_WIDTH)),
        ],
        out_specs=(
            transposed_spec(LANES), row_spec(ATTN_WIDTH), transposed_spec(VT_ROWS),
            row_spec(ATTN_WIDTH), row_spec(POOL_WIDTH),
            pl.BlockSpec((blocks_per_tile, 1, ATTN_WIDTH), lambda i: (i, 0, 0)),
        ),
        out_shape=out_shape,
        scratch_shapes=[pltpu.VMEM((POOL_HALO + ROW_TILE, POOL_WIDTH), jnp.float32)],
        compiler_params=pltpu.CompilerParams(
            dimension_semantics=("arbitrary",), vmem_limit_bytes=VMEM_LIMIT),
        name="in_proj",
    )(x2, gain, w_bf16, cos_t, sin_t, w_pool, b_pool, pool_scale)


def _select_bias_t(gate_t, n_past):
    slot = lax.broadcasted_iota(jnp.int32, gate_t.shape, 0)
    slot_f = slot.astype(jnp.float32)
    neg_inf = jnp.float32(-jnp.inf)
    g = jnp.where(slot < n_past, gate_t, neg_inf)
    bias = jnp.full(gate_t.shape, MASK_BIAS, dtype=jnp.float32)
    for _ in range(MOBA_TOPK):
        best = jnp.max(g, axis=0, keepdims=True)
        first = jnp.min(jnp.where(g == best, slot_f, float(gate_t.shape[0])), axis=0, keepdims=True)
        pick = (slot_f == first) & (best > neg_inf)
        bias = jnp.where(pick, 0.0, bias)
        g = jnp.where(pick, neg_inf, g)
    return bias


def _fold_rows(t, op):
    return op(t.reshape(t.shape[0] // SUBLANES, SUBLANES, t.shape[1]), axis=0)


def _moba_kernel(qt_ref, k_ref, vt_ref, onehot_ref, kmean_ref, ga_ref, o_ref,
                 qaug_ref, s_ref, own_ref, mfin_ref):
    t = pl.program_id(2)
    blk = MOBA_BLOCK
    n_slots = kmean_ref.shape[0]
    j = jnp.minimum(t, n_slots - 1)
    j_prev = jnp.maximum(t - 1, 0)
    groups_new = jnp.where(t < n_slots, (t + KEY_UNROLL - 1) // KEY_UNROLL, 0)
    groups_old = (j_prev + KEY_UNROLL - 1) // KEY_UNROLL
    groups_both = jnp.minimum(groups_new, groups_old)

    @pl.when(t == 0)
    def _():
        own_ref[...] = jnp.zeros_like(own_ref)
        mfin_ref[...] = jnp.zeros_like(mfin_ref)

    m_prev = [mfin_ref[h, :1] for h in range(HEADS_PER_TILE)]

    def weighted_values(s, n, h):
        p = jnp.exp2(s - m_prev[h]).astype(jnp.bfloat16)
        v_aug = vt_ref[n, h * VT_HEAD_ROWS:(h + 1) * VT_HEAD_ROWS, :]
        r = jnp.dot(v_aug, p, preferred_element_type=jnp.float32)
        return r[:HEAD_DIM], r[HEAD_DIM:HEAD_DIM + SUBLANES]

    acc_init = [weighted_values(own_ref[h], j_prev, h) for h in range(HEADS_PER_TILE)]

    qt = qt_ref[...].astype(jnp.float32)
    feat = lax.broadcasted_iota(jnp.int32, (LANES, blk), 0)
    km = kmean_ref[...]
    km_hi = km.astype(jnp.bfloat16)
    km_lo = (km - km_hi.astype(jnp.float32)).astype(jnp.bfloat16)
    key_pos = lax.broadcasted_iota(jnp.int32, (blk, blk), 0)
    qry_pos = lax.broadcasted_iota(jnp.int32, (blk, blk), 1)
    causal = key_pos <= qry_pos
    k_own = k_ref[pl.ds(pl.multiple_of(j * blk, blk), blk), :]

    m_init = []
    for h in range(HEADS_PER_TILE):
        qh = jnp.where((feat // HEAD_DIM) == h, qt, 0.0).astype(jnp.bfloat16)
        gate_t = (jnp.dot(km_hi, qh, preferred_element_type=jnp.float32)
                  + jnp.dot(km_lo, qh, preferred_element_type=jnp.float32))
        qaug_ref[h, :LANES, :] = qh
        qaug_ref[h, LANES:LANES + n_slots, :] = _select_bias_t(gate_t, j).astype(jnp.bfloat16)
        qaug_ref[h, LANES + n_slots:, :] = jnp.zeros((LANES - n_slots, blk), jnp.bfloat16)
        s = jnp.dot(k_own, qh, preferred_element_type=jnp.float32)
        s = jnp.where(causal, s, -jnp.inf)
        own_ref[h] = s
        m_init.append(_fold_rows(s, jnp.max))

    def group_body(finish_old, score_new):
        def body(g, carry):
            m_run, acc = [list(c) for c in carry]
            for i in range(KEY_UNROLL):
                n = g * KEY_UNROLL + i
                rows = pl.ds(pl.multiple_of(n * blk, blk), blk)
                if score_new:
                    k_aug = jnp.concatenate([k_ref[rows, :], onehot_ref[rows, :]], axis=1)
                for h in range(HEADS_PER_TILE):
                    if finish_old:
                        o, l = weighted_values(s_ref[h, n], n, h)
                        acc[h] = (acc[h][0] + o, acc[h][1] + l)
                    if score_new:
                        s = jnp.dot(k_aug, qaug_ref[h], preferred_element_type=jnp.float32)
                        s_ref[h, n] = s
                        m_run[h] = jnp.maximum(m_run[h], _fold_rows(s, jnp.max))
            return tuple(m_run), tuple(acc)
        return body

    carry = (tuple(m_init), tuple(acc_init))
    carry = lax.fori_loop(0, groups_both, group_body(True, True), carry)
    carry = lax.fori_loop(groups_both, groups_new, group_body(False, True), carry)
    carry = lax.fori_loop(groups_both, groups_old, group_body(True, False), carry)
    m_run, acc = carry

    for h in range(HEADS_PER_TILE):
        mfin_ref[h] = jnp.broadcast_to(jnp.max(m_run[h], axis=0, keepdims=True), (SUBLANES, blk))

    @pl.when(t > 0)
    def _():
        out_t = jnp.concatenate([o / l[:1] for o, l in acc], axis=0)
        o_ref[...] = (out_t.T * _silu(ga_ref[...])).astype(o_ref.dtype)


def _moba_attention(qt, k, vt, onehot, kmean, gate_attn):
    batch, seq, _ = k.shape
    n_blocks = seq // MOBA_BLOCK
    slots = -(-(n_blocks - 1) // KEY_UNROLL) * KEY_UNROLL
    finished_spec = pl.BlockSpec((None, MOBA_BLOCK, LANES), lambda b, hp, t: (b, jnp.maximum(t - 1, 0), hp))
    return pl.pallas_call(
        _moba_kernel,
        grid=(batch, HEAD_TILES, n_blocks + 1),
        in_specs=[
            pl.BlockSpec((None, None, None, LANES, MOBA_BLOCK),
                         lambda b, hp, t: (b, jnp.minimum(t, n_blocks - 1), hp, 0, 0)),
            pl.BlockSpec((None, seq, LANES), lambda b, hp, t: (b, 0, hp)),
            pl.BlockSpec((None, n_blocks, None, VT_ROWS, MOBA_BLOCK), lambda b, hp, t: (b, 0, hp, 0, 0)),
            pl.BlockSpec((seq, LANES), lambda b, hp, t: (0, 0)),
            pl.BlockSpec((None, n_blocks, LANES), lambda b, hp, t: (b, 0, hp)),
            finished_spec,
        ],
        out_specs=finished_spec,
        out_shape=jax.ShapeDtypeStruct((batch, seq, ATTN_WIDTH), jnp.bfloat16),
        scratch_shapes=[
            pltpu.VMEM((HEADS_PER_TILE, 2 * LANES, MOBA_BLOCK), jnp.bfloat16),
            pltpu.VMEM((HEADS_PER_TILE, slots, MOBA_BLOCK, MOBA_BLOCK), jnp.float32),
            pltpu.VMEM((HEADS_PER_TILE, MOBA_BLOCK, MOBA_BLOCK), jnp.float32),
            pltpu.VMEM((HEADS_PER_TILE, SUBLANES, MOBA_BLOCK), jnp.float32),
        ],
        compiler_params=pltpu.CompilerParams(
            dimension_semantics=("arbitrary", "arbitrary", "arbitrary"),
            vmem_limit_bytes=VMEM_LIMIT),
        name="moba_attention",
    )(qt, k, vt, onehot, kmean, gate_attn)


def _out_proj_kernel(apply_final_norm, x_ref, attn_ref, pool_ref, wout_ref, fgain_ref, o_ref):
    mixed = jnp.concatenate([attn_ref[...], pool_ref[...]], axis=1)
    y = x_ref[...] + jnp.dot(mixed, wout_ref[...], preferred_element_type=jnp.float32)
    if apply_final_norm:
        inv = lax.rsqrt(jnp.mean(y * y, axis=-1, keepdims=True) + EPS)
        y = y * inv * fgain_ref[...]
    o_ref[...] = y


def _out_proj(x2, attn, pool, w_out, final_gain, apply_final_norm):
    rows = x2.shape[0]
    row_spec = lambda w: pl.BlockSpec((ROW_TILE, w), lambda i: (i, 0))
    return pl.pallas_call(
        functools.partial(_out_proj_kernel, apply_final_norm),
        grid=(rows // ROW_TILE,),
        in_specs=[
            row_spec(D_MODEL), row_spec(ATTN_WIDTH), row_spec(POOL_WIDTH),
            pl.BlockSpec((D_MODEL, D_MODEL), lambda i: (0, 0)),
            pl.BlockSpec((1, D_MODEL), lambda i: (0, 0)),
        ],
        out_specs=row_spec(D_MODEL),
        out_shape=jax.ShapeDtypeStruct((rows, D_MODEL), jnp.float32),
        compiler_params=pltpu.CompilerParams(
            dimension_semantics=("arbitrary",), vmem_limit_bytes=VMEM_LIMIT),
        name="out_proj",
    )(x2, attn, pool, w_out, final_gain)


def _rope_tables(seq):
    pos = jnp.arange(seq, dtype=jnp.float32)
    inv_freq = 1.0 / (ROPE_THETA ** (jnp.arange(0, HEAD_DIM, 2, dtype=jnp.float32) / HEAD_DIM))
    ang = pos[:, None] * inv_freq[None, :]
    cos, sin = jnp.cos(ang), jnp.sin(ang)
    reps = LANES // (HEAD_DIM // 2)
    cos_t = jnp.tile(cos, (1, reps))
    sin_t = jnp.tile(jnp.concatenate([-sin, sin], axis=1), (1, HEADS_PER_TILE))
    return cos_t, sin_t


def kernel(x, norm_gain, w_in, w_pool, b_pool, pool_scale, w_out, final_gain):
    batch, seq, d_model = x.shape
    depth = w_in.shape[0]
    assert d_model == D_MODEL and seq % ROW_TILE == 0 and seq // MOBA_BLOCK <= LANES
    n_blocks = seq // MOBA_BLOCK
    cos_t, sin_t = _rope_tables(seq)
    onehot = (jnp.arange(seq)[:, None] // MOBA_BLOCK == jnp.arange(LANES)[None, :]).astype(jnp.bfloat16)

    x2 = x.reshape(batch * seq, d_model)
    for l in range(depth):
        qt, k, vt, gate_attn, pool, kmean = _in_proj(
            x2, norm_gain[l][None, :], w_in[l].astype(jnp.bfloat16), cos_t, sin_t,
            w_pool[l].astype(jnp.bfloat16), b_pool[l][:, None, :], pool_scale[l][None, :], seq)
        shape3 = lambda t: t.reshape(batch, seq, t.shape[-1])
        per_block = lambda t: t.reshape(batch, n_blocks, HEAD_TILES, t.shape[-2], MOBA_BLOCK)
        attn = _moba_attention(per_block(qt), shape3(k), per_block(vt), onehot,
                               kmean.reshape(batch, n_blocks, ATTN_WIDTH), shape3(gate_attn))
        x2 = _out_proj(x2, attn.reshape(batch * seq, ATTN_WIDTH), pool, w_out[l].astype(jnp.bfloat16),
                       final_gain[None, :], apply_final_norm=(l == depth - 1))
    return x2.reshape(batch, seq, d_model)
```

```python
import functools

import numpy as np

import jax
import jax.numpy as jnp
from jax import lax
from jax.experimental import pallas as pl
from jax.experimental.pallas import tpu as pltpu

D_MODEL = 1024
ATTN_WIDTH = D_MODEL // 2
POOL_WIDTH = D_MODEL - ATTN_WIDTH
HEAD_DIM = 64
MOBA_BLOCK = 256
MOBA_TOPK = 3
POOL_WINDOWS = (2, 4, 8, 16)
POOL_GROUP = POOL_WIDTH // len(POOL_WINDOWS)
ROPE_THETA = 10000.0
EPS = 1e-6
IN_WIDTH = 4 * ATTN_WIDTH + 2 * POOL_WIDTH

LANES = 128
SUBLANES = 8
HEADS_PER_TILE = LANES // HEAD_DIM
HEAD_TILES = ATTN_WIDTH // LANES
ROW_TILE = 512
POOL_HALO = 16
MASK_BIAS = -1e30
KEY_UNROLL = 8
BF16_ROWS = 16
VT_HEAD_ROWS = HEAD_DIM + BF16_ROWS
VT_ROWS = HEADS_PER_TILE * VT_HEAD_ROWS
Q_SCALE = HEAD_DIM ** -0.5 * 1.4426950408889634
VMEM_LIMIT = 56 * 1024 * 1024


def _silu(t):
    return t * (1.0 / (1.0 + jnp.exp(-t)))


def _rope(t, cos, sin_signed, first_half):
    swapped = jnp.where(first_half, pltpu.roll(t, LANES - HEAD_DIM // 2, axis=1),
                        pltpu.roll(t, HEAD_DIM // 2, axis=1))
    return t * cos + swapped * sin_signed


def _in_proj_kernel(x_ref, gain_ref, w_ref, cos_ref, sin_ref,
                    qt_ref, k_ref, vt_ref, ga_ref, u_ref, gp_ref, kmean_ref):
    x = x_ref[...]
    inv = lax.rsqrt(jnp.mean(x * x, axis=-1, keepdims=True) + EPS)
    h = (x * inv * gain_ref[...]).astype(jnp.bfloat16)

    def proj(c):
        return jnp.dot(h, w_ref[:, c * ATTN_WIDTH:(c + 1) * ATTN_WIDTH],
                       preferred_element_type=jnp.float32)

    cos = cos_ref[...]
    sin = sin_ref[...]
    lane = lax.broadcasted_iota(jnp.int32, (ROW_TILE, LANES), 1)
    first_half = (lane % HEAD_DIM) < (HEAD_DIM // 2)

    q = proj(0)
    k = proj(1)
    v = proj(2)
    for t in range(HEAD_TILES):
        sl = slice(t * LANES, (t + 1) * LANES)
        qr = _rope(q[:, sl], cos, sin, first_half) * Q_SCALE
        kr = _rope(k[:, sl], cos, sin, first_half)
        k_ref[:, sl] = kr.astype(k_ref.dtype)
        for b in range(ROW_TILE // MOBA_BLOCK):
            blk_rows = slice(b * MOBA_BLOCK, (b + 1) * MOBA_BLOCK)
            kmean_ref[b, :, sl] = jnp.sum(kr[blk_rows], axis=0, keepdims=True) * (1.0 / MOBA_BLOCK)
            qt_ref[b, t] = qr[blk_rows].T.astype(qt_ref.dtype)
            v_t = v[blk_rows, sl].T.astype(vt_ref.dtype)
            for head in range(HEADS_PER_TILE):
                base = head * VT_HEAD_ROWS
                vt_ref[b, t, base:base + HEAD_DIM] = v_t[head * HEAD_DIM:(head + 1) * HEAD_DIM]
                vt_ref[b, t, base + HEAD_DIM:base + VT_HEAD_ROWS] = jnp.ones((BF16_ROWS, MOBA_BLOCK),
                                                                             vt_ref.dtype)
    ga_ref[...] = proj(3)
    u_ref[...] = proj(4)
    gp_ref[...] = proj(5)


def _in_proj(x2, gain, w_bf16, cos_t, sin_t, seq):
    rows = x2.shape[0]
    n_tiles = rows // ROW_TILE
    seq_tiles = seq // ROW_TILE
    row_spec = lambda w: pl.BlockSpec((ROW_TILE, w), lambda i: (i, 0))
    tab_spec = pl.BlockSpec((ROW_TILE, LANES), lambda i: (i % seq_tiles, 0))
    blocks_per_tile = ROW_TILE // MOBA_BLOCK
    n_blocks = rows // MOBA_BLOCK
    transposed = lambda r: jax.ShapeDtypeStruct((n_blocks, HEAD_TILES, r, MOBA_BLOCK), jnp.bfloat16)
    transposed_spec = lambda r: pl.BlockSpec((blocks_per_tile, HEAD_TILES, r, MOBA_BLOCK),
                                             lambda i: (i, 0, 0, 0))
    out_shape = (
        transposed(LANES),
        jax.ShapeDtypeStruct((rows, ATTN_WIDTH), jnp.bfloat16),
        transposed(VT_ROWS),
        jax.ShapeDtypeStruct((rows, ATTN_WIDTH), jnp.float32),
        jax.ShapeDtypeStruct((rows, POOL_WIDTH), jnp.float32),
        jax.ShapeDtypeStruct((rows, POOL_WIDTH), jnp.float32),
        jax.ShapeDtypeStruct((n_blocks, 1, ATTN_WIDTH), jnp.float32),
    )
    return pl.pallas_call(
        _in_proj_kernel,
        grid=(n_tiles,),
        in_specs=[
            row_spec(D_MODEL),
            pl.BlockSpec((1, D_MODEL), lambda i: (0, 0)),
            pl.BlockSpec((D_MODEL, IN_WIDTH), lambda i: (0, 0)),
            tab_spec, tab_spec,
        ],
        out_specs=(
            transposed_spec(LANES), row_spec(ATTN_WIDTH), transposed_spec(VT_ROWS),
            row_spec(ATTN_WIDTH), row_spec(POOL_WIDTH), row_spec(POOL_WIDTH),
            pl.BlockSpec((blocks_per_tile, 1, ATTN_WIDTH), lambda i: (i, 0, 0)),
        ),
        out_shape=out_shape,
        compiler_params=pltpu.CompilerParams(
            dimension_semantics=("arbitrary",), vmem_limit_bytes=VMEM_LIMIT),
        name="in_proj",
    )(x2, gain, w_bf16, cos_t, sin_t)


def _select_bias_t(gate_t, n_past):
    slot = lax.broadcasted_iota(jnp.int32, gate_t.shape, 0)
    slot_f = slot.astype(jnp.float32)
    neg_inf = jnp.float32(-jnp.inf)
    g = jnp.where(slot < n_past, gate_t, neg_inf)
    bias = jnp.full(gate_t.shape, MASK_BIAS, dtype=jnp.float32)
    for _ in range(MOBA_TOPK):
        best = jnp.max(g, axis=0, keepdims=True)
        first = jnp.min(jnp.where(g == best, slot_f, float(gate_t.shape[0])), axis=0, keepdims=True)
        pick = (slot_f == first) & (best > neg_inf)
        bias = jnp.where(pick, 0.0, bias)
        g = jnp.where(pick, neg_inf, g)
    return bias


def _fold_rows(t, op):
    return op(t.reshape(t.shape[0] // SUBLANES, SUBLANES, t.shape[1]), axis=0)


def _moba_kernel(qt_ref, k_ref, vt_ref, onehot_ref, kmean_ref, ga_ref, o_ref,
                 qaug_ref, s_ref, own_ref, mfin_ref):
    t = pl.program_id(2)
    blk = MOBA_BLOCK
    n_slots = kmean_ref.shape[0]
    j = jnp.minimum(t, n_slots - 1)
    j_prev = jnp.maximum(t - 1, 0)
    groups_new = jnp.where(t < n_slots, (t + KEY_UNROLL - 1) // KEY_UNROLL, 0)
    groups_old = (j_prev + KEY_UNROLL - 1) // KEY_UNROLL
    groups_both = jnp.minimum(groups_new, groups_old)

    @pl.when(t == 0)
    def _():
        own_ref[...] = jnp.zeros_like(own_ref)
        mfin_ref[...] = jnp.zeros_like(mfin_ref)

    m_prev = [mfin_ref[h, :1] for h in range(HEADS_PER_TILE)]

    def weighted_values(s, n, h):
        p = jnp.exp2(s - m_prev[h]).astype(jnp.bfloat16)
        v_aug = vt_ref[n, h * VT_HEAD_ROWS:(h + 1) * VT_HEAD_ROWS, :]
        r = jnp.dot(v_aug, p, preferred_element_type=jnp.float32)
        return r[:HEAD_DIM], r[HEAD_DIM:HEAD_DIM + SUBLANES]

    acc_init = [weighted_values(own_ref[h], j_prev, h) for h in range(HEADS_PER_TILE)]

    qt = qt_ref[...].astype(jnp.float32)
    feat = lax.broadcasted_iota(jnp.int32, (LANES, blk), 0)
    km = kmean_ref[:, 0, :]
    km_hi = km.astype(jnp.bfloat16)
    km_lo = (km - km_hi.astype(jnp.float32)).astype(jnp.bfloat16)
    key_pos = lax.broadcasted_iota(jnp.int32, (blk, blk), 0)
    qry_pos = lax.broadcasted_iota(jnp.int32, (blk, blk), 1)
    causal = key_pos <= qry_pos
    k_own = k_ref[pl.ds(pl.multiple_of(j * blk, blk), blk), :]

    m_init = []
    for h in range(HEADS_PER_TILE):
        qh = jnp.where((feat // HEAD_DIM) == h, qt, 0.0).astype(jnp.bfloat16)
        gate_t = (jnp.dot(km_hi, qh, preferred_element_type=jnp.float32)
                  + jnp.dot(km_lo, qh, preferred_element_type=jnp.float32))
        qaug_ref[h, :LANES, :] = qh
        qaug_ref[h, LANES:LANES + n_slots, :] = _select_bias_t(gate_t, j).astype(jnp.bfloat16)
        qaug_ref[h, LANES + n_slots:, :] = jnp.zeros((LANES - n_slots, blk), jnp.bfloat16)
        s = jnp.dot(k_own, qh, preferred_element_type=jnp.float32)
        s = jnp.where(causal, s, -jnp.inf)
        own_ref[h] = s
        m_init.append(_fold_rows(s, jnp.max))

    def group_body(finish_old, score_new):
        def body(g, carry):
            m_run, acc = [list(c) for c in carry]
            for i in range(KEY_UNROLL):
                n = g * KEY_UNROLL + i
                rows = pl.ds(pl.multiple_of(n * blk, blk), blk)
                if score_new:
                    k_aug = jnp.concatenate([k_ref[rows, :], onehot_ref[rows, :]], axis=1)
                for h in range(HEADS_PER_TILE):
                    if finish_old:
                        o, l = weighted_values(s_ref[h, n], n, h)
                        acc[h] = (acc[h][0] + o, acc[h][1] + l)
                    if score_new:
                        s = jnp.dot(k_aug, qaug_ref[h], preferred_element_type=jnp.float32)
                        s_ref[h, n] = s
                        m_run[h] = jnp.maximum(m_run[h], _fold_rows(s, jnp.max))
            return tuple(m_run), tuple(acc)
        return body

    carry = (tuple(m_init), tuple(acc_init))
    carry = lax.fori_loop(0, groups_both, group_body(True, True), carry)
    carry = lax.fori_loop(groups_both, groups_new, group_body(False, True), carry)
    carry = lax.fori_loop(groups_both, groups_old, group_body(True, False), carry)
    m_run, acc = carry

    for h in range(HEADS_PER_TILE):
        mfin_ref[h] = jnp.broadcast_to(jnp.max(m_run[h], axis=0, keepdims=True), (SUBLANES, blk))

    @pl.when(t > 0)
    def _():
        out_t = jnp.concatenate([o / l[:1] for o, l in acc], axis=0)
        o_ref[...] = (out_t.T * _silu(ga_ref[...])).astype(o_ref.dtype)


def _moba_attention(qt, k, vt, onehot, kmean, gate_attn):
    batch, seq, _ = k.shape
    n_blocks = seq // MOBA_BLOCK
    slots = -(-(n_blocks - 1) // KEY_UNROLL) * KEY_UNROLL
    finished_spec = pl.BlockSpec((None, MOBA_BLOCK, LANES), lambda b, hp, t: (b, jnp.maximum(t - 1, 0), hp))
    return pl.pallas_call(
        _moba_kernel,
        grid=(batch, HEAD_TILES, n_blocks + 1),
        in_specs=[
            pl.BlockSpec((None, None, None, LANES, MOBA_BLOCK),
                         lambda b, hp, t: (b, jnp.minimum(t, n_blocks - 1), hp, 0, 0)),
            pl.BlockSpec((None, seq, LANES), lambda b, hp, t: (b, 0, hp)),
            pl.BlockSpec((None, n_blocks, None, VT_ROWS, MOBA_BLOCK), lambda b, hp, t: (b, 0, hp, 0, 0)),
            pl.BlockSpec((seq, LANES), lambda b, hp, t: (0, 0)),
            pl.BlockSpec((n_blocks, 1, LANES), lambda b, hp, t: (b, 0, hp)),
            finished_spec,
        ],
        out_specs=finished_spec,
        out_shape=jax.ShapeDtypeStruct((batch, seq, ATTN_WIDTH), jnp.bfloat16),
        scratch_shapes=[
            pltpu.VMEM((HEADS_PER_TILE, 2 * LANES, MOBA_BLOCK), jnp.bfloat16),
            pltpu.VMEM((HEADS_PER_TILE, slots, MOBA_BLOCK, MOBA_BLOCK), jnp.float32),
            pltpu.VMEM((HEADS_PER_TILE, MOBA_BLOCK, MOBA_BLOCK), jnp.float32),
            pltpu.VMEM((HEADS_PER_TILE, SUBLANES, MOBA_BLOCK), jnp.float32),
        ],
        compiler_params=pltpu.CompilerParams(
            dimension_semantics=("arbitrary", "arbitrary", "arbitrary"),
            vmem_limit_bytes=VMEM_LIMIT),
        name="moba_attention",
    )(qt, k, vt, onehot, kmean, gate_attn)


def _out_proj_kernel(apply_final_norm, x_ref, attn_ref, u_ref, uprev_ref, gp_ref, wpool_ref, bpool_ref,
                     pscale_ref, wout_ref, fgain_ref, o_ref, ext_ref):
    i = pl.program_id(1)
    halo = uprev_ref[...]
    ext_ref[:POOL_HALO, :] = jnp.where(i > 0, halo, jnp.zeros_like(halo))
    ext_ref[POOL_HALO:, :] = u_ref[...]
    pos = i * ROW_TILE + lax.broadcasted_iota(jnp.int32, (ROW_TILE, 1), 0)

    pooled_parts = []
    for g, window in enumerate(POOL_WINDOWS):
        sl = slice(g * POOL_GROUP, (g + 1) * POOL_GROUP)
        u_g = ext_ref[POOL_HALO:, sl]
        total = u_g
        for t in range(1, window):
            total = total + ext_ref[POOL_HALO - t:POOL_HALO - t + ROW_TILE, sl]
        count = jnp.minimum(pos + 1, window).astype(jnp.float32)
        pooled = total / count - u_g
        y = jnp.dot(pooled.astype(jnp.bfloat16), wpool_ref[g], preferred_element_type=jnp.float32)
        pooled_parts.append(y + bpool_ref[g])
    pool = jnp.concatenate(pooled_parts, axis=1) * pscale_ref[...]
    pool = (pool * _silu(gp_ref[...])).astype(jnp.bfloat16)

    mixed = jnp.concatenate([attn_ref[...], pool], axis=1)
    y = x_ref[...] + jnp.dot(mixed, wout_ref[...], preferred_element_type=jnp.float32)
    if apply_final_norm:
        inv = lax.rsqrt(jnp.mean(y * y, axis=-1, keepdims=True) + EPS)
        y = y * inv * fgain_ref[...]
    o_ref[...] = y


def _out_proj(x, attn, u, gate_pool, w_pool, b_pool, pool_scale, w_out, final_gain, apply_final_norm):
    batch, seq, _ = x.shape
    tiles = seq // ROW_TILE
    halo_per_tile = ROW_TILE // POOL_HALO
    row_spec = lambda w: pl.BlockSpec((None, ROW_TILE, w), lambda b, i: (b, i, 0))
    const2 = lambda shape: pl.BlockSpec(shape, lambda b, i: (0, 0))
    const3 = lambda shape: pl.BlockSpec(shape, lambda b, i: (0, 0, 0))
    return pl.pallas_call(
        functools.partial(_out_proj_kernel, apply_final_norm),
        grid=(batch, tiles),
        in_specs=[
            row_spec(D_MODEL),
            row_spec(ATTN_WIDTH),
            row_spec(POOL_WIDTH),
            pl.BlockSpec((None, POOL_HALO, POOL_WIDTH),
                         lambda b, i: (b, jnp.maximum(i * halo_per_tile - 1, 0), 0)),
            row_spec(POOL_WIDTH),
            const3((len(POOL_WINDOWS), POOL_GROUP, POOL_GROUP)),
            const3((len(POOL_WINDOWS), 1, POOL_GROUP)),
            const2((1, POOL_WIDTH)),
            const2((D_MODEL, D_MODEL)),
            const2((1, D_MODEL)),
        ],
        out_specs=row_spec(D_MODEL),
        out_shape=jax.ShapeDtypeStruct((batch, seq, D_MODEL), jnp.float32),
        scratch_shapes=[pltpu.VMEM((POOL_HALO + ROW_TILE, POOL_WIDTH), jnp.float32)],
        compiler_params=pltpu.CompilerParams(
            dimension_semantics=("arbitrary", "arbitrary"), vmem_limit_bytes=VMEM_LIMIT),
        name="pool_out_proj",
    )(x, attn, u, u, gate_pool, w_pool, b_pool, pool_scale, w_out, final_gain)


def _position_tables(seq):
    pos = np.arange(seq, dtype=np.float64)
    inv_freq = 1.0 / (ROPE_THETA ** (np.arange(0, HEAD_DIM, 2, dtype=np.float64) / HEAD_DIM))
    ang = pos[:, None] * inv_freq[None, :]
    cos, sin = np.cos(ang), np.sin(ang)
    cos_t = np.tile(cos, (1, LANES // (HEAD_DIM // 2)))
    sin_t = np.tile(np.concatenate([-sin, sin], axis=1), (1, HEADS_PER_TILE))
    onehot = np.arange(seq)[:, None] // MOBA_BLOCK == np.arange(LANES)[None, :]
    return (jnp.asarray(cos_t, jnp.float32), jnp.asarray(sin_t, jnp.float32),
            jnp.asarray(onehot, jnp.bfloat16))


def kernel(x, norm_gain, w_in, w_pool, b_pool, pool_scale, w_out, final_gain):
    batch, seq, d_model = x.shape
    depth = w_in.shape[0]
    assert d_model == D_MODEL and seq % ROW_TILE == 0 and seq // MOBA_BLOCK <= LANES
    n_blocks = seq // MOBA_BLOCK
    cos_t, sin_t, onehot = _position_tables(seq)

    for l in range(depth):
        qt, k, vt, gate_attn, u, gate_pool, kmean = _in_proj(
            x.reshape(batch * seq, d_model), norm_gain[l][None, :], w_in[l].astype(jnp.bfloat16),
            cos_t, sin_t, seq)
        shape3 = lambda t: t.reshape(batch, seq, t.shape[-1])
        per_block = lambda t: t.reshape(batch, n_blocks, HEAD_TILES, t.shape[-2], MOBA_BLOCK)
        attn = _moba_attention(per_block(qt), shape3(k), per_block(vt), onehot, kmean, shape3(gate_attn))
        x = _out_proj(x, attn, shape3(u), shape3(gate_pool), w_pool[l].astype(jnp.bfloat16),
                      b_pool[l][:, None, :], pool_scale[l][None, :], w_out[l].astype(jnp.bfloat16),
                      final_gain[None, :], apply_final_norm=(l == depth - 1))
    return x
```

```python
import functools

import numpy as np

import jax
import jax.numpy as jnp
from jax import lax
from jax.experimental import pallas as pl
from jax.experimental.pallas import tpu as pltpu

D_MODEL = 1024
ATTN_WIDTH = D_MODEL // 2
POOL_WIDTH = D_MODEL - ATTN_WIDTH
HEAD_DIM = 64
MOBA_BLOCK = 256
MOBA_TOPK = 3
POOL_WINDOWS = (2, 4, 8, 16)
POOL_GROUP = POOL_WIDTH // len(POOL_WINDOWS)
ROPE_THETA = 10000.0
EPS = 1e-6
IN_WIDTH = 4 * ATTN_WIDTH + 2 * POOL_WIDTH

LANES = 128
SUBLANES = 8
HEADS_PER_TILE = LANES // HEAD_DIM
HEAD_TILES = ATTN_WIDTH // LANES
ROW_TILE = 512
POOL_HALO = 16
MASK_BIAS = -1e30
KEY_UNROLL = 8
BF16_ROWS = 16
VT_HEAD_ROWS = HEAD_DIM + BF16_ROWS
VT_ROWS = HEADS_PER_TILE * VT_HEAD_ROWS
Q_SCALE = HEAD_DIM ** -0.5 * 1.4426950408889634
VMEM_LIMIT = 56 * 1024 * 1024


def _silu(t):
    return t * (1.0 / (1.0 + jnp.exp(-t)))


def _rope(t, cos, sin_signed, first_half):
    swapped = jnp.where(first_half, pltpu.roll(t, LANES - HEAD_DIM // 2, axis=1),
                        pltpu.roll(t, HEAD_DIM // 2, axis=1))
    return t * cos + swapped * sin_signed


def _in_proj_kernel(x_ref, gain_ref, w_ref, cos_ref, sin_ref,
                    qt_ref, k_ref, vt_ref, ga_ref, u_ref, gp_ref, kmean_ref):
    x = x_ref[...]
    inv = lax.rsqrt(jnp.mean(x * x, axis=-1, keepdims=True) + EPS)
    h = (x * inv * gain_ref[...]).astype(jnp.bfloat16)

    def proj(c):
        return jnp.dot(h, w_ref[:, c * ATTN_WIDTH:(c + 1) * ATTN_WIDTH],
                       preferred_element_type=jnp.float32)

    cos = cos_ref[...]
    sin = sin_ref[...]
    lane = lax.broadcasted_iota(jnp.int32, (ROW_TILE, LANES), 1)
    first_half = (lane % HEAD_DIM) < (HEAD_DIM // 2)

    q = proj(0)
    k = proj(1)
    v = proj(2)
    for t in range(HEAD_TILES):
        sl = slice(t * LANES, (t + 1) * LANES)
        qr = _rope(q[:, sl], cos, sin, first_half) * Q_SCALE
        kr = _rope(k[:, sl], cos, sin, first_half)
        k_ref[:, sl] = kr.astype(k_ref.dtype)
        for b in range(ROW_TILE // MOBA_BLOCK):
            blk_rows = slice(b * MOBA_BLOCK, (b + 1) * MOBA_BLOCK)
            kmean_ref[b, :, sl] = jnp.sum(kr[blk_rows], axis=0, keepdims=True) * (1.0 / MOBA_BLOCK)
            qt_ref[b, t] = qr[blk_rows].T.astype(qt_ref.dtype)
            v_t = v[blk_rows, sl].T.astype(vt_ref.dtype)
            for head in range(HEADS_PER_TILE):
                base = head * VT_HEAD_ROWS
                vt_ref[b, t, base:base + HEAD_DIM] = v_t[head * HEAD_DIM:(head + 1) * HEAD_DIM]
                vt_ref[b, t, base + HEAD_DIM:base + VT_HEAD_ROWS] = jnp.ones((BF16_ROWS, MOBA_BLOCK),
                                                                             vt_ref.dtype)
    ga_ref[...] = proj(3)
    u_ref[...] = proj(4)
    gp_ref[...] = proj(5)


def _in_proj(x2, gain, w_bf16, cos_t, sin_t, seq):
    rows = x2.shape[0]
    n_tiles = rows // ROW_TILE
    seq_tiles = seq // ROW_TILE
    row_spec = lambda w: pl.BlockSpec((ROW_TILE, w), lambda i: (i, 0))
    tab_spec = pl.BlockSpec((ROW_TILE, LANES), lambda i: (i % seq_tiles, 0))
    blocks_per_tile = ROW_TILE // MOBA_BLOCK
    n_blocks = rows // MOBA_BLOCK
    transposed = lambda r: jax.ShapeDtypeStruct((n_blocks, HEAD_TILES, r, MOBA_BLOCK), jnp.bfloat16)
    transposed_spec = lambda r: pl.BlockSpec((blocks_per_tile, HEAD_TILES, r, MOBA_BLOCK),
                                             lambda i: (i, 0, 0, 0))
    out_shape = (
        transposed(LANES),
        jax.ShapeDtypeStruct((rows, ATTN_WIDTH), jnp.bfloat16),
        transposed(VT_ROWS),
        jax.ShapeDtypeStruct((rows, ATTN_WIDTH), jnp.float32),
        jax.ShapeDtypeStruct((rows, POOL_WIDTH), jnp.float32),
        jax.ShapeDtypeStruct((rows, POOL_WIDTH), jnp.float32),
        jax.ShapeDtypeStruct((n_blocks, 1, ATTN_WIDTH), jnp.float32),
    )
    return pl.pallas_call(
        _in_proj_kernel,
        grid=(n_tiles,),
        in_specs=[
            row_spec(D_MODEL),
            pl.BlockSpec((1, D_MODEL), lambda i: (0, 0)),
            pl.BlockSpec((D_MODEL, IN_WIDTH), lambda i: (0, 0)),
            tab_spec, tab_spec,
        ],
        out_specs=(
            transposed_spec(LANES), row_spec(ATTN_WIDTH), transposed_spec(VT_ROWS),
            row_spec(ATTN_WIDTH), row_spec(POOL_WIDTH), row_spec(POOL_WIDTH),
            pl.BlockSpec((blocks_per_tile, 1, ATTN_WIDTH), lambda i: (i, 0, 0)),
        ),
        out_shape=out_shape,
        compiler_params=pltpu.CompilerParams(
            dimension_semantics=("arbitrary",), vmem_limit_bytes=VMEM_LIMIT),
        name="in_proj",
    )(x2, gain, w_bf16, cos_t, sin_t)


def _select_bias_t(gate_t, n_past):
    slot = lax.broadcasted_iota(jnp.int32, gate_t.shape, 0)
    slot_f = slot.astype(jnp.float32)
    neg_inf = jnp.float32(-jnp.inf)
    g = jnp.where(slot < n_past, gate_t, neg_inf)
    bias = jnp.full(gate_t.shape, MASK_BIAS, dtype=jnp.float32)
    for _ in range(MOBA_TOPK):
        best = jnp.max(g, axis=0, keepdims=True)
        first = jnp.min(jnp.where(g == best, slot_f, float(gate_t.shape[0])), axis=0, keepdims=True)
        pick = (slot_f == first) & (best > neg_inf)
        bias = jnp.where(pick, 0.0, bias)
        g = jnp.where(pick, neg_inf, g)
    return bias


def _fold_rows(t, op):
    return op(t.reshape(t.shape[0] // SUBLANES, SUBLANES, t.shape[1]), axis=0)


def _moba_kernel(qt_ref, k_ref, vt_ref, onehot_ref, kmean_ref, ga_ref, o_ref,
                 qaug_ref, s_ref, own_ref, mfin_ref):
    t = pl.program_id(2)
    blk = MOBA_BLOCK
    n_slots = kmean_ref.shape[0]
    j = jnp.minimum(t, n_slots - 1)
    j_prev = jnp.maximum(t - 1, 0)
    groups_new = jnp.where(t < n_slots, (t + KEY_UNROLL - 1) // KEY_UNROLL, 0)
    groups_old = (j_prev + KEY_UNROLL - 1) // KEY_UNROLL
    groups_both = jnp.minimum(groups_new, groups_old)

    @pl.when(t == 0)
    def _():
        own_ref[...] = jnp.zeros_like(own_ref)
        mfin_ref[...] = jnp.zeros_like(mfin_ref)

    m_prev = [mfin_ref[h, :1] for h in range(HEADS_PER_TILE)]

    def weighted_values(s, n, h):
        p = jnp.exp2(s - m_prev[h]).astype(jnp.bfloat16)
        v_aug = vt_ref[n, h * VT_HEAD_ROWS:(h + 1) * VT_HEAD_ROWS, :]
        r = jnp.dot(v_aug, p, preferred_element_type=jnp.float32)
        return r[:HEAD_DIM], r[HEAD_DIM:HEAD_DIM + SUBLANES]

    acc_init = [weighted_values(own_ref[h], j_prev, h) for h in range(HEADS_PER_TILE)]

    qt = qt_ref[...].astype(jnp.float32)
    feat = lax.broadcasted_iota(jnp.int32, (LANES, blk), 0)
    km = kmean_ref[:, 0, :]
    km_hi = km.astype(jnp.bfloat16)
    km_lo = (km - km_hi.astype(jnp.float32)).astype(jnp.bfloat16)
    key_pos = lax.broadcasted_iota(jnp.int32, (blk, blk), 0)
    qry_pos = lax.broadcasted_iota(jnp.int32, (blk, blk), 1)
    causal = key_pos <= qry_pos
    k_own = k_ref[pl.ds(pl.multiple_of(j * blk, blk), blk), :]

    m_init = []
    for h in range(HEADS_PER_TILE):
        qh = jnp.where((feat // HEAD_DIM) == h, qt, 0.0).astype(jnp.bfloat16)
        gate_t = (jnp.dot(km_hi, qh, preferred_element_type=jnp.float32)
                  + jnp.dot(km_lo, qh, preferred_element_type=jnp.float32))
        qaug_ref[h, :LANES, :] = qh
        qaug_ref[h, LANES:LANES + n_slots, :] = _select_bias_t(gate_t, j).astype(jnp.bfloat16)
        qaug_ref[h, LANES + n_slots:, :] = jnp.zeros((LANES - n_slots, blk), jnp.bfloat16)
        s = jnp.dot(k_own, qh, preferred_element_type=jnp.float32)
        s = jnp.where(causal, s, -jnp.inf)
        own_ref[h] = s
        m_init.append(_fold_rows(s, jnp.max))

    def group_body(finish_old, score_new):
        def body(g, carry):
            m_run, acc = [list(c) for c in carry]
            for i in range(KEY_UNROLL):
                n = g * KEY_UNROLL + i
                rows = pl.ds(pl.multiple_of(n * blk, blk), blk)
                if score_new:
                    k_aug = jnp.concatenate([k_ref[rows, :], onehot_ref[rows, :]], axis=1)
                for h in range(HEADS_PER_TILE):
                    if finish_old:
                        o, l = weighted_values(s_ref[h, n, :blk, :], n, h)
                        acc[h] = (acc[h][0] + o, acc[h][1] + l)
                    if score_new:
                        s = jnp.dot(k_aug, qaug_ref[h], preferred_element_type=jnp.float32)
                        s_ref[h, n, :blk, :] = s
                        m_run[h] = jnp.maximum(m_run[h], _fold_rows(s, jnp.max))
            return tuple(m_run), tuple(acc)
        return body

    carry = (tuple(m_init), tuple(acc_init))
    carry = lax.fori_loop(0, groups_both, group_body(True, True), carry)
    carry = lax.fori_loop(groups_both, groups_new, group_body(False, True), carry)
    carry = lax.fori_loop(groups_both, groups_old, group_body(True, False), carry)
    m_run, acc = carry

    for h in range(HEADS_PER_TILE):
        mfin_ref[h] = jnp.broadcast_to(jnp.max(m_run[h], axis=0, keepdims=True), (SUBLANES, blk))

    @pl.when(t > 0)
    def _():
        out_t = jnp.concatenate([o / l[:1] for o, l in acc], axis=0)
        o_ref[...] = (out_t.T * _silu(ga_ref[...])).astype(o_ref.dtype)


def _moba_attention(qt, k, vt, onehot, kmean, gate_attn):
    batch, seq, _ = k.shape
    n_blocks = seq // MOBA_BLOCK
    slots = -(-(n_blocks - 1) // KEY_UNROLL) * KEY_UNROLL
    finished_spec = pl.BlockSpec((None, MOBA_BLOCK, LANES), lambda b, hp, t: (b, jnp.maximum(t - 1, 0), hp))
    return pl.pallas_call(
        _moba_kernel,
        grid=(batch, HEAD_TILES, n_blocks + 1),
        in_specs=[
            pl.BlockSpec((None, None, None, LANES, MOBA_BLOCK),
                         lambda b, hp, t: (b, jnp.minimum(t, n_blocks - 1), hp, 0, 0)),
            pl.BlockSpec((None, seq, LANES), lambda b, hp, t: (b, 0, hp)),
            pl.BlockSpec((None, n_blocks, None, VT_ROWS, MOBA_BLOCK), lambda b, hp, t: (b, 0, hp, 0, 0)),
            pl.BlockSpec((seq, LANES), lambda b, hp, t: (0, 0)),
            pl.BlockSpec((n_blocks, 1, LANES), lambda b, hp, t: (b, 0, hp)),
            finished_spec,
        ],
        out_specs=finished_spec,
        out_shape=jax.ShapeDtypeStruct((batch, seq, ATTN_WIDTH), jnp.bfloat16),
        scratch_shapes=[
            pltpu.VMEM((HEADS_PER_TILE, 2 * LANES, MOBA_BLOCK), jnp.bfloat16),
            pltpu.VMEM((HEADS_PER_TILE, slots, MOBA_BLOCK + SUBLANES, MOBA_BLOCK), jnp.float32),
            pltpu.VMEM((HEADS_PER_TILE, MOBA_BLOCK, MOBA_BLOCK), jnp.float32),
            pltpu.VMEM((HEADS_PER_TILE, SUBLANES, MOBA_BLOCK), jnp.float32),
        ],
        compiler_params=pltpu.CompilerParams(
            dimension_semantics=("arbitrary", "arbitrary", "arbitrary"),
            vmem_limit_bytes=VMEM_LIMIT),
        name="moba_attention",
    )(qt, k, vt, onehot, kmean, gate_attn)


def _out_proj_kernel(apply_final_norm, x_ref, attn_ref, u_ref, uprev_ref, gp_ref, wpool_ref, bpool_ref,
                     pscale_ref, wout_ref, fgain_ref, o_ref, ext_ref):
    i = pl.program_id(1)
    halo = uprev_ref[...]
    ext_ref[:POOL_HALO, :] = jnp.where(i > 0, halo, jnp.zeros_like(halo))
    ext_ref[POOL_HALO:, :] = u_ref[...]
    pos = i * ROW_TILE + lax.broadcasted_iota(jnp.int32, (ROW_TILE, 1), 0)

    pooled_parts = []
    for g, window in enumerate(POOL_WINDOWS):
        sl = slice(g * POOL_GROUP, (g + 1) * POOL_GROUP)
        u_g = ext_ref[POOL_HALO:, sl]
        total = u_g
        for t in range(1, window):
            total = total + ext_ref[POOL_HALO - t:POOL_HALO - t + ROW_TILE, sl]
        count = jnp.minimum(pos + 1, window).astype(jnp.float32)
        pooled = total / count - u_g
        y = jnp.dot(pooled.astype(jnp.bfloat16), wpool_ref[g], preferred_element_type=jnp.float32)
        pooled_parts.append(y + bpool_ref[g])
    pool = jnp.concatenate(pooled_parts, axis=1) * pscale_ref[...]
    pool = (pool * _silu(gp_ref[...])).astype(jnp.bfloat16)

    mixed = jnp.concatenate([attn_ref[...], pool], axis=1)
    y = x_ref[...] + jnp.dot(mixed, wout_ref[...], preferred_element_type=jnp.float32)
    if apply_final_norm:
        inv = lax.rsqrt(jnp.mean(y * y, axis=-1, keepdims=True) + EPS)
        y = y * inv * fgain_ref[...]
    o_ref[...] = y


def _out_proj(x, attn, u, gate_pool, w_pool, b_pool, pool_scale, w_out, final_gain, apply_final_norm):
    batch, seq, _ = x.shape
    tiles = seq // ROW_TILE
    halo_per_tile = ROW_TILE // POOL_HALO
    row_spec = lambda w: pl.BlockSpec((None, ROW_TILE, w), lambda b, i: (b, i, 0))
    const2 = lambda shape: pl.BlockSpec(shape, lambda b, i: (0, 0))
    const3 = lambda shape: pl.BlockSpec(shape, lambda b, i: (0, 0, 0))
    return pl.pallas_call(
        functools.partial(_out_proj_kernel, apply_final_norm),
        grid=(batch, tiles),
        in_specs=[
            row_spec(D_MODEL),
            row_spec(ATTN_WIDTH),
            row_spec(POOL_WIDTH),
            pl.BlockSpec((None, POOL_HALO, POOL_WIDTH),
                         lambda b, i: (b, jnp.maximum(i * halo_per_tile - 1, 0), 0)),
            row_spec(POOL_WIDTH),
            const3((len(POOL_WINDOWS), POOL_GROUP, POOL_GROUP)),
            const3((len(POOL_WINDOWS), 1, POOL_GROUP)),
            const2((1, POOL_WIDTH)),
            const2((D_MODEL, D_MODEL)),
            const2((1, D_MODEL)),
        ],
        out_specs=row_spec(D_MODEL),
        out_shape=jax.ShapeDtypeStruct((batch, seq, D_MODEL), jnp.float32),
        scratch_shapes=[pltpu.VMEM((POOL_HALO + ROW_TILE, POOL_WIDTH), jnp.float32)],
        compiler_params=pltpu.CompilerParams(
            dimension_semantics=("arbitrary", "arbitrary"), vmem_limit_bytes=VMEM_LIMIT),
        name="pool_out_proj",
    )(x, attn, u, u, gate_pool, w_pool, b_pool, pool_scale, w_out, final_gain)


def _position_tables(seq):
    pos = np.arange(seq, dtype=np.float64)
    inv_freq = 1.0 / (ROPE_THETA ** (np.arange(0, HEAD_DIM, 2, dtype=np.float64) / HEAD_DIM))
    ang = pos[:, None] * inv_freq[None, :]
    cos, sin = np.cos(ang), np.sin(ang)
    cos_t = np.tile(cos, (1, LANES // (HEAD_DIM // 2)))
    sin_t = np.tile(np.concatenate([-sin, sin], axis=1), (1, HEADS_PER_TILE))
    onehot = np.arange(seq)[:, None] // MOBA_BLOCK == np.arange(LANES)[None, :]
    return (jnp.asarray(cos_t, jnp.float32), jnp.asarray(sin_t, jnp.float32),
            jnp.asarray(onehot, jnp.bfloat16))


def kernel(x, norm_gain, w_in, w_pool, b_pool, pool_scale, w_out, final_gain):
    batch, seq, d_model = x.shape
    depth = w_in.shape[0]
    assert d_model == D_MODEL and seq % ROW_TILE == 0 and seq // MOBA_BLOCK <= LANES
    n_blocks = seq // MOBA_BLOCK
    cos_t, sin_t, onehot = _position_tables(seq)

    for l in range(depth):
        qt, k, vt, gate_attn, u, gate_pool, kmean = _in_proj(
            x.reshape(batch * seq, d_model), norm_gain[l][None, :], w_in[l].astype(jnp.bfloat16),
            cos_t, sin_t, seq)
        shape3 = lambda t: t.reshape(batch, seq, t.shape[-1])
        per_block = lambda t: t.reshape(batch, n_blocks, HEAD_TILES, t.shape[-2], MOBA_BLOCK)
        attn = _moba_attention(per_block(qt), shape3(k), per_block(vt), onehot, kmean, shape3(gate_attn))
        x = _out_proj(x, attn, shape3(u), shape3(gate_pool), w_pool[l].astype(jnp.bfloat16),
                      b_pool[l][:, None, :], pool_scale[l][None, :], w_out[l].astype(jnp.bfloat16),
                      final_gain[None, :], apply_final_norm=(l == depth - 1))
    return x
```

```python
import functools

import numpy as np

import jax
import jax.numpy as jnp
from jax import lax
from jax.experimental import pallas as pl
from jax.experimental.pallas import tpu as pltpu

D_MODEL = 1024
ATTN_WIDTH = D_MODEL // 2
POOL_WIDTH = D_MODEL - ATTN_WIDTH
HEAD_DIM = 64
MOBA_BLOCK = 256
MOBA_TOPK = 3
POOL_WINDOWS = (2, 4, 8, 16)
POOL_GROUP = POOL_WIDTH // len(POOL_WINDOWS)
ROPE_THETA = 10000.0
EPS = 1e-6
IN_WIDTH = 4 * ATTN_WIDTH + 2 * POOL_WIDTH

LANES = 128
SUBLANES = 8
HEADS_PER_TILE = LANES // HEAD_DIM
HEAD_TILES = ATTN_WIDTH // LANES
ROW_TILE = 512
POOL_HALO = 16
MASK_BIAS = -1e30
TILES_PER_ITER = 8
BF16_ROWS = 16
VT_HEAD_ROWS = HEAD_DIM + BF16_ROWS
VT_ROWS = HEADS_PER_TILE * VT_HEAD_ROWS
ACC_ROWS = HEAD_DIM + SUBLANES
Q_SCALE = HEAD_DIM ** -0.5 * 1.4426950408889634
VMEM_LIMIT = 56 * 1024 * 1024


def _silu(t):
    return t * (1.0 / (1.0 + jnp.exp(-t)))


def _rope(t, cos, sin_signed, first_half):
    swapped = jnp.where(first_half, pltpu.roll(t, LANES - HEAD_DIM // 2, axis=1),
                        pltpu.roll(t, HEAD_DIM // 2, axis=1))
    return t * cos + swapped * sin_signed


def _in_proj_kernel(x_ref, gain_ref, w_ref, cos_ref, sin_ref,
                    qt_ref, k_ref, vt_ref, ga_ref, u_ref, gp_ref, kmean_ref):
    x = x_ref[...]
    inv = lax.rsqrt(jnp.mean(x * x, axis=-1, keepdims=True) + EPS)
    h = (x * inv * gain_ref[...]).astype(jnp.bfloat16)

    def proj(c):
        return jnp.dot(h, w_ref[:, c * ATTN_WIDTH:(c + 1) * ATTN_WIDTH],
                       preferred_element_type=jnp.float32)

    cos = cos_ref[...]
    sin = sin_ref[...]
    lane = lax.broadcasted_iota(jnp.int32, (ROW_TILE, LANES), 1)
    first_half = (lane % HEAD_DIM) < (HEAD_DIM // 2)

    q = proj(0)
    k = proj(1)
    v = proj(2)
    for t in range(HEAD_TILES):
        sl = slice(t * LANES, (t + 1) * LANES)
        qr = _rope(q[:, sl], cos, sin, first_half) * Q_SCALE
        kr = _rope(k[:, sl], cos, sin, first_half)
        k_ref[:, sl] = kr.astype(k_ref.dtype)
        for b in range(ROW_TILE // MOBA_BLOCK):
            blk_rows = slice(b * MOBA_BLOCK, (b + 1) * MOBA_BLOCK)
            kmean_ref[b, :, sl] = jnp.sum(kr[blk_rows], axis=0, keepdims=True) * (1.0 / MOBA_BLOCK)
            qt_ref[b, t] = qr[blk_rows].T.astype(qt_ref.dtype)
            v_t = v[blk_rows, sl].T.astype(vt_ref.dtype)
            for head in range(HEADS_PER_TILE):
                base = head * VT_HEAD_ROWS
                vt_ref[b, t, base:base + HEAD_DIM] = v_t[head * HEAD_DIM:(head + 1) * HEAD_DIM]
                vt_ref[b, t, base + HEAD_DIM:base + VT_HEAD_ROWS] = jnp.ones((BF16_ROWS, MOBA_BLOCK),
                                                                             vt_ref.dtype)
    ga_ref[...] = proj(3)
    u_ref[...] = proj(4)
    gp_ref[...] = proj(5)


def _in_proj(x2, gain, w_bf16, cos_t, sin_t, seq):
    rows = x2.shape[0]
    n_tiles = rows // ROW_TILE
    seq_tiles = seq // ROW_TILE
    row_spec = lambda w: pl.BlockSpec((ROW_TILE, w), lambda i: (i, 0))
    tab_spec = pl.BlockSpec((ROW_TILE, LANES), lambda i: (i % seq_tiles, 0))
    blocks_per_tile = ROW_TILE // MOBA_BLOCK
    n_blocks = rows // MOBA_BLOCK
    transposed = lambda r: jax.ShapeDtypeStruct((n_blocks, HEAD_TILES, r, MOBA_BLOCK), jnp.bfloat16)
    transposed_spec = lambda r: pl.BlockSpec((blocks_per_tile, HEAD_TILES, r, MOBA_BLOCK),
                                             lambda i: (i, 0, 0, 0))
    out_shape = (
        transposed(LANES),
        jax.ShapeDtypeStruct((rows, ATTN_WIDTH), jnp.bfloat16),
        transposed(VT_ROWS),
        jax.ShapeDtypeStruct((rows, ATTN_WIDTH), jnp.float32),
        jax.ShapeDtypeStruct((rows, POOL_WIDTH), jnp.float32),
        jax.ShapeDtypeStruct((rows, POOL_WIDTH), jnp.float32),
        jax.ShapeDtypeStruct((n_blocks, 1, ATTN_WIDTH), jnp.float32),
    )
    return pl.pallas_call(
        _in_proj_kernel,
        grid=(n_tiles,),
        in_specs=[
            row_spec(D_MODEL),
            pl.BlockSpec((1, D_MODEL), lambda i: (0, 0)),
            pl.BlockSpec((D_MODEL, IN_WIDTH), lambda i: (0, 0)),
            tab_spec, tab_spec,
        ],
        out_specs=(
            transposed_spec(LANES), row_spec(ATTN_WIDTH), transposed_spec(VT_ROWS),
            row_spec(ATTN_WIDTH), row_spec(POOL_WIDTH), row_spec(POOL_WIDTH),
            pl.BlockSpec((blocks_per_tile, 1, ATTN_WIDTH), lambda i: (i, 0, 0)),
        ),
        out_shape=out_shape,
        compiler_params=pltpu.CompilerParams(
            dimension_semantics=("arbitrary",), vmem_limit_bytes=VMEM_LIMIT),
        name="in_proj",
    )(x2, gain, w_bf16, cos_t, sin_t)


def _tile_tables(n_blocks):
    pairs = [(t, n) for t in range(n_blocks) for n in range(t + 1)]
    return (np.asarray([p[0] for p in pairs], np.int32), np.asarray([p[1] for p in pairs], np.int32))


def _pass_lag(n_blocks):
    return -(-(TILES_PER_ITER + n_blocks - 1) // TILES_PER_ITER)


def _select_bias_t(gate_t, own):
    slot = lax.broadcasted_iota(jnp.int32, gate_t.shape, 0)
    slot_f = slot.astype(jnp.float32)
    neg_inf = jnp.float32(-jnp.inf)
    g = jnp.where(slot < own, gate_t, neg_inf)
    bias = jnp.where(slot == own, 0.0, MASK_BIAS)
    for _ in range(MOBA_TOPK):
        best = jnp.max(g, axis=0, keepdims=True)
        first = jnp.min(jnp.where(g == best, slot_f, float(gate_t.shape[0])), axis=0, keepdims=True)
        pick = (slot_f == first) & (best > neg_inf)
        bias = jnp.where(pick, 0.0, bias)
        g = jnp.where(pick, neg_inf, g)
    return bias


def _fold_rows(t, op):
    return op(t.reshape(t.shape[0] // SUBLANES, SUBLANES, t.shape[1]), axis=0)


def _moba_kernel(tq_ref, tn_ref, qt_ref, k_ref, vt_ref, onehot_ref, kmean_ref, o_ref,
                 qaug_ref, ring_ref, m_ref, acc_ref):
    blk = MOBA_BLOCK
    n_blocks = kmean_ref.shape[0]
    n_iters = tq_ref.shape[0] // TILES_PER_ITER
    lag = _pass_lag(n_blocks)

    feat = lax.broadcasted_iota(jnp.int32, (LANES, blk), 0)
    km = kmean_ref[:, 0, :]
    km_hi = km.astype(jnp.bfloat16)
    km_lo = (km - km_hi.astype(jnp.float32)).astype(jnp.bfloat16)

    def prepare(t, carry):
        qt = qt_ref[t].astype(jnp.float32)
        for h in range(HEADS_PER_TILE):
            qh = jnp.where((feat // HEAD_DIM) == h, qt, 0.0).astype(jnp.bfloat16)
            gate_t = (jnp.dot(km_hi, qh, preferred_element_type=jnp.float32)
                      + jnp.dot(km_lo, qh, preferred_element_type=jnp.float32))
            qaug_ref[t, h, :LANES, :] = qh
            qaug_ref[t, h, LANES:LANES + n_blocks, :] = _select_bias_t(gate_t, t).astype(jnp.bfloat16)
            qaug_ref[t, h, LANES + n_blocks:, :] = jnp.zeros((LANES - n_blocks, blk), jnp.bfloat16)
            m_ref[t, h] = jnp.full((SUBLANES, blk), -jnp.inf, jnp.float32)
            acc_ref[t, h] = jnp.zeros((ACC_ROWS, blk), jnp.float32)
        return carry

    lax.fori_loop(0, n_blocks, prepare, 0)

    key_pos = lax.broadcasted_iota(jnp.int32, (blk, blk), 0)
    qry_pos = lax.broadcasted_iota(jnp.int32, (blk, blk), 1)
    causal = key_pos <= qry_pos

    def score_tile(tile, slot):
        t = tq_ref[tile]
        n = tn_ref[tile]
        rows = pl.ds(pl.multiple_of(n * blk, blk), blk)
        k_aug = jnp.concatenate([k_ref[rows, :], onehot_ref[rows, :]], axis=1)
        keep = jnp.logical_or(causal, n < t)
        for h in range(HEADS_PER_TILE):
            s = jnp.dot(k_aug, qaug_ref[t, h], preferred_element_type=jnp.float32)
            s = jnp.where(keep, s, -jnp.inf)
            ring_ref[slot, h] = s
            m_ref[t, h] = jnp.maximum(m_ref[t, h], _fold_rows(s, jnp.max))

    def value_tile(tile, slot, m_fin):
        t = tq_ref[tile]
        n = tn_ref[tile]
        for h in range(HEADS_PER_TILE):
            p = jnp.exp2(ring_ref[slot, h] - m_fin[h]).astype(jnp.bfloat16)
            v_aug = vt_ref[n, h * VT_HEAD_ROWS:(h + 1) * VT_HEAD_ROWS, :]
            r = jnp.dot(v_aug, p, preferred_element_type=jnp.float32)
            acc_ref[t, h] += r[:ACC_ROWS]

    def iteration(score, finish):
        def body(i, carry):
            base = lax.rem(i, lag) * TILES_PER_ITER
            if finish:
                done = (i - lag) * TILES_PER_ITER
                m_fin = [[jnp.max(m_ref[tq_ref[done + u], h], axis=0, keepdims=True)
                          for h in range(HEADS_PER_TILE)] for u in range(TILES_PER_ITER)]
            for u in range(TILES_PER_ITER):
                if finish:
                    value_tile(done + u, base + u, m_fin[u])
                if score:
                    score_tile(i * TILES_PER_ITER + u, base + u)
            return carry
        return body

    lax.fori_loop(0, lag, iteration(True, False), 0)
    lax.fori_loop(lag, n_iters, iteration(True, True), 0)
    lax.fori_loop(n_iters, n_iters + lag, iteration(False, True), 0)

    def finish_block(t, carry):
        heads = []
        for h in range(HEADS_PER_TILE):
            a = acc_ref[t, h]
            heads.append(a[:HEAD_DIM] / a[HEAD_DIM:HEAD_DIM + 1])
        out_t = jnp.concatenate(heads, axis=0)
        o_ref[pl.ds(pl.multiple_of(t * blk, blk), blk), :] = out_t.T.astype(o_ref.dtype)
        return carry

    lax.fori_loop(0, n_blocks, finish_block, 0)


def _moba_attention(qt, k, vt, onehot, kmean):
    batch, seq, _ = k.shape
    n_blocks = seq // MOBA_BLOCK
    tile_q, tile_n = _tile_tables(n_blocks)
    assert tile_q.shape[0] % TILES_PER_ITER == 0
    ring_tiles = _pass_lag(n_blocks) * TILES_PER_ITER
    once = dict(pipeline_mode=pl.Buffered(1))
    grid_spec = pltpu.PrefetchScalarGridSpec(
        num_scalar_prefetch=2,
        grid=(batch, HEAD_TILES),
        in_specs=[
            pl.BlockSpec((None, n_blocks, None, LANES, MOBA_BLOCK), lambda b, hp, tq, tn: (b, 0, hp, 0, 0)),
            pl.BlockSpec((None, seq, LANES), lambda b, hp, tq, tn: (b, 0, hp)),
            pl.BlockSpec((None, n_blocks, None, VT_ROWS, MOBA_BLOCK), lambda b, hp, tq, tn: (b, 0, hp, 0, 0)),
            pl.BlockSpec((seq, LANES), lambda b, hp, tq, tn: (0, 0), **once),
            pl.BlockSpec((n_blocks, 1, LANES), lambda b, hp, tq, tn: (b, 0, hp)),
        ],
        out_specs=pl.BlockSpec((None, seq, LANES), lambda b, hp, tq, tn: (b, 0, hp)),
        scratch_shapes=[
            pltpu.VMEM((n_blocks, HEADS_PER_TILE, 2 * LANES, MOBA_BLOCK), jnp.bfloat16),
            pltpu.VMEM((ring_tiles, HEADS_PER_TILE, MOBA_BLOCK, MOBA_BLOCK), jnp.float32),
            pltpu.VMEM((n_blocks, HEADS_PER_TILE, SUBLANES, MOBA_BLOCK), jnp.float32),
            pltpu.VMEM((n_blocks, HEADS_PER_TILE, ACC_ROWS, MOBA_BLOCK), jnp.float32),
        ],
    )
    return pl.pallas_call(
        _moba_kernel,
        grid_spec=grid_spec,
        out_shape=jax.ShapeDtypeStruct((batch, seq, ATTN_WIDTH), jnp.bfloat16),
        compiler_params=pltpu.CompilerParams(
            dimension_semantics=("arbitrary", "arbitrary"), vmem_limit_bytes=VMEM_LIMIT),
        name="moba_attention",
    )(jnp.asarray(tile_q), jnp.asarray(tile_n), qt, k, vt, onehot, kmean)


def _out_proj_kernel(apply_final_norm, x_ref, attn_ref, ga_ref, u_ref, uprev_ref, gp_ref, wpool_ref,
                     bpool_ref, pscale_ref, wout_ref, fgain_ref, o_ref, ext_ref):
    i = pl.program_id(1)
    halo = uprev_ref[...]
    ext_ref[:POOL_HALO, :] = jnp.where(i > 0, halo, jnp.zeros_like(halo))
    ext_ref[POOL_HALO:, :] = u_ref[...]
    pos = i * ROW_TILE + lax.broadcasted_iota(jnp.int32, (ROW_TILE, 1), 0)

    pooled_parts = []
    for g, window in enumerate(POOL_WINDOWS):
        sl = slice(g * POOL_GROUP, (g + 1) * POOL_GROUP)
        u_g = ext_ref[POOL_HALO:, sl]
        total = u_g
        for t in range(1, window):
            total = total + ext_ref[POOL_HALO - t:POOL_HALO - t + ROW_TILE, sl]
        count = jnp.minimum(pos + 1, window).astype(jnp.float32)
        pooled = total / count - u_g
        y = jnp.dot(pooled.astype(jnp.bfloat16), wpool_ref[g], preferred_element_type=jnp.float32)
        pooled_parts.append(y + bpool_ref[g])
    pool = jnp.concatenate(pooled_parts, axis=1) * pscale_ref[...]
    pool = (pool * _silu(gp_ref[...])).astype(jnp.bfloat16)
    attn = (attn_ref[...].astype(jnp.float32) * _silu(ga_ref[...])).astype(jnp.bfloat16)

    mixed = jnp.concatenate([attn, pool], axis=1)
    y = x_ref[...] + jnp.dot(mixed, wout_ref[...], preferred_element_type=jnp.float32)
    if apply_final_norm:
        inv = lax.rsqrt(jnp.mean(y * y, axis=-1, keepdims=True) + EPS)
        y = y * inv * fgain_ref[...]
    o_ref[...] = y


def _out_proj(x, attn, gate_attn, u, gate_pool, w_pool, b_pool, pool_scale, w_out, final_gain,
              apply_final_norm):
    batch, seq, _ = x.shape
    tiles = seq // ROW_TILE
    halo_per_tile = ROW_TILE // POOL_HALO
    row_spec = lambda w: pl.BlockSpec((None, ROW_TILE, w), lambda b, i: (b, i, 0))
    const2 = lambda shape: pl.BlockSpec(shape, lambda b, i: (0, 0))
    const3 = lambda shape: pl.BlockSpec(shape, lambda b, i: (0, 0, 0))
    return pl.pallas_call(
        functools.partial(_out_proj_kernel, apply_final_norm),
        grid=(batch, tiles),
        in_specs=[
            row_spec(D_MODEL),
            row_spec(ATTN_WIDTH),
            row_spec(ATTN_WIDTH),
            row_spec(POOL_WIDTH),
            pl.BlockSpec((None, POOL_HALO, POOL_WIDTH),
                         lambda b, i: (b, jnp.maximum(i * halo_per_tile - 1, 0), 0)),
            row_spec(POOL_WIDTH),
            const3((len(POOL_WINDOWS), POOL_GROUP, POOL_GROUP)),
            const3((len(POOL_WINDOWS), 1, POOL_GROUP)),
            const2((1, POOL_WIDTH)),
            const2((D_MODEL, D_MODEL)),
            const2((1, D_MODEL)),
        ],
        out_specs=row_spec(D_MODEL),
        out_shape=jax.ShapeDtypeStruct((batch, seq, D_MODEL), jnp.float32),
        scratch_shapes=[pltpu.VMEM((POOL_HALO + ROW_TILE, POOL_WIDTH), jnp.float32)],
        compiler_params=pltpu.CompilerParams(
            dimension_semantics=("arbitrary", "arbitrary"), vmem_limit_bytes=VMEM_LIMIT),
        name="pool_out_proj",
    )(x, attn, gate_attn, u, u, gate_pool, w_pool, b_pool, pool_scale, w_out, final_gain)


def _position_tables(seq):
    pos = np.arange(seq, dtype=np.float64)
    inv_freq = 1.0 / (ROPE_THETA ** (np.arange(0, HEAD_DIM, 2, dtype=np.float64) / HEAD_DIM))
    ang = pos[:, None] * inv_freq[None, :]
    cos, sin = np.cos(ang), np.sin(ang)
    cos_t = np.tile(cos, (1, LANES // (HEAD_DIM // 2)))
    sin_t = np.tile(np.concatenate([-sin, sin], axis=1), (1, HEADS_PER_TILE))
    onehot = np.arange(seq)[:, None] // MOBA_BLOCK == np.arange(LANES)[None, :]
    return (jnp.asarray(cos_t, jnp.float32), jnp.asarray(sin_t, jnp.float32),
            jnp.asarray(onehot, jnp.bfloat16))


def kernel(x, norm_gain, w_in, w_pool, b_pool, pool_scale, w_out, final_gain):
    batch, seq, d_model = x.shape
    depth = w_in.shape[0]
    assert d_model == D_MODEL and seq % ROW_TILE == 0 and seq // MOBA_BLOCK <= LANES
    n_blocks = seq // MOBA_BLOCK
    cos_t, sin_t, onehot = _position_tables(seq)

    for l in range(depth):
        qt, k, vt, gate_attn, u, gate_pool, kmean = _in_proj(
            x.reshape(batch * seq, d_model), norm_gain[l][None, :], w_in[l].astype(jnp.bfloat16),
            cos_t, sin_t, seq)
        shape3 = lambda t: t.reshape(batch, seq, t.shape[-1])
        per_block = lambda t: t.reshape(batch, n_blocks, HEAD_TILES, t.shape[-2], MOBA_BLOCK)
        attn = _moba_attention(per_block(qt), shape3(k), per_block(vt), onehot, kmean)
        x = _out_proj(x, attn, shape3(gate_attn), shape3(u), shape3(gate_pool),
                      w_pool[l].astype(jnp.bfloat16), b_pool[l][:, None, :], pool_scale[l][None, :],
                      w_out[l].astype(jnp.bfloat16), final_gain[None, :],
                      apply_final_norm=(l == depth - 1))
    return x
```

```python
import functools

import numpy as np

import jax
import jax.numpy as jnp
from jax import lax
from jax.experimental import pallas as pl
from jax.experimental.pallas import tpu as pltpu

D_MODEL = 1024
ATTN_WIDTH = D_MODEL // 2
POOL_WIDTH = D_MODEL - ATTN_WIDTH
HEAD_DIM = 64
MOBA_BLOCK = 256
MOBA_TOPK = 3
POOL_WINDOWS = (2, 4, 8, 16)
POOL_GROUP = POOL_WIDTH // len(POOL_WINDOWS)
ROPE_THETA = 10000.0
EPS = 1e-6
IN_WIDTH = 4 * ATTN_WIDTH + 2 * POOL_WIDTH

LANES = 128
SUBLANES = 8
HEADS_PER_TILE = LANES // HEAD_DIM
HEAD_TILES = ATTN_WIDTH // LANES
ROW_TILE = 512
POOL_HALO = 16
MASK_BIAS = -1e30
KEY_UNROLL = 8
BF16_ROWS = 16
VT_HEAD_ROWS = HEAD_DIM + BF16_ROWS
VT_ROWS = HEADS_PER_TILE * VT_HEAD_ROWS
Q_SCALE = HEAD_DIM ** -0.5 * 1.4426950408889634
VMEM_LIMIT = 56 * 1024 * 1024


def _silu(t):
    return t * (1.0 / (1.0 + jnp.exp(-t)))


def _rope(t, cos, sin_signed, first_half):
    swapped = jnp.where(first_half, pltpu.roll(t, LANES - HEAD_DIM // 2, axis=1),
                        pltpu.roll(t, HEAD_DIM // 2, axis=1))
    return t * cos + swapped * sin_signed


def _in_proj_kernel(x_ref, gain_ref, w_ref, cos_ref, sin_ref,
                    qt_ref, k_ref, vt_ref, ga_ref, u_ref, gp_ref, kmean_ref):
    x = x_ref[...]
    inv = lax.rsqrt(jnp.mean(x * x, axis=-1, keepdims=True) + EPS)
    h = (x * inv * gain_ref[...]).astype(jnp.bfloat16)

    def proj(c):
        return jnp.dot(h, w_ref[:, c * ATTN_WIDTH:(c + 1) * ATTN_WIDTH],
                       preferred_element_type=jnp.float32)

    cos = cos_ref[...]
    sin = sin_ref[...]
    lane = lax.broadcasted_iota(jnp.int32, (ROW_TILE, LANES), 1)
    first_half = (lane % HEAD_DIM) < (HEAD_DIM // 2)

    q = proj(0)
    k = proj(1)
    v = proj(2)
    for t in range(HEAD_TILES):
        sl = slice(t * LANES, (t + 1) * LANES)
        qr = _rope(q[:, sl], cos, sin, first_half) * Q_SCALE
        kr = _rope(k[:, sl], cos, sin, first_half)
        k_ref[:, sl] = kr.astype(k_ref.dtype)
        for b in range(ROW_TILE // MOBA_BLOCK):
            blk_rows = slice(b * MOBA_BLOCK, (b + 1) * MOBA_BLOCK)
            kmean_ref[b, :, sl] = jnp.sum(kr[blk_rows], axis=0, keepdims=True) * (1.0 / MOBA_BLOCK)
            qt_ref[b, t] = qr[blk_rows].T.astype(qt_ref.dtype)
            v_t = v[blk_rows, sl].T.astype(vt_ref.dtype)
            for head in range(HEADS_PER_TILE):
                base = head * VT_HEAD_ROWS
                vt_ref[b, t, base:base + HEAD_DIM] = v_t[head * HEAD_DIM:(head + 1) * HEAD_DIM]
                vt_ref[b, t, base + HEAD_DIM:base + VT_HEAD_ROWS] = jnp.ones((BF16_ROWS, MOBA_BLOCK),
                                                                             vt_ref.dtype)
    ga_ref[...] = proj(3)
    u_ref[...] = proj(4)
    gp_ref[...] = proj(5)


def _in_proj(x2, gain, w_bf16, cos_t, sin_t, seq):
    rows = x2.shape[0]
    n_tiles = rows // ROW_TILE
    seq_tiles = seq // ROW_TILE
    row_spec = lambda w: pl.BlockSpec((ROW_TILE, w), lambda i: (i, 0))
    tab_spec = pl.BlockSpec((ROW_TILE, LANES), lambda i: (i % seq_tiles, 0))
    blocks_per_tile = ROW_TILE // MOBA_BLOCK
    n_blocks = rows // MOBA_BLOCK
    transposed = lambda r: jax.ShapeDtypeStruct((n_blocks, HEAD_TILES, r, MOBA_BLOCK), jnp.bfloat16)
    transposed_spec = lambda r: pl.BlockSpec((blocks_per_tile, HEAD_TILES, r, MOBA_BLOCK),
                                             lambda i: (i, 0, 0, 0))
    out_shape = (
        transposed(LANES),
        jax.ShapeDtypeStruct((rows, ATTN_WIDTH), jnp.bfloat16),
        transposed(VT_ROWS),
        jax.ShapeDtypeStruct((rows, ATTN_WIDTH), jnp.float32),
        jax.ShapeDtypeStruct((rows, POOL_WIDTH), jnp.float32),
        jax.ShapeDtypeStruct((rows, POOL_WIDTH), jnp.float32),
        jax.ShapeDtypeStruct((n_blocks, 1, ATTN_WIDTH), jnp.float32),
    )
    return pl.pallas_call(
        _in_proj_kernel,
        grid=(n_tiles,),
        in_specs=[
            row_spec(D_MODEL),
            pl.BlockSpec((1, D_MODEL), lambda i: (0, 0)),
            pl.BlockSpec((D_MODEL, IN_WIDTH), lambda i: (0, 0)),
            tab_spec, tab_spec,
        ],
        out_specs=(
            transposed_spec(LANES), row_spec(ATTN_WIDTH), transposed_spec(VT_ROWS),
            row_spec(ATTN_WIDTH), row_spec(POOL_WIDTH), row_spec(POOL_WIDTH),
            pl.BlockSpec((blocks_per_tile, 1, ATTN_WIDTH), lambda i: (i, 0, 0)),
        ),
        out_shape=out_shape,
        compiler_params=pltpu.CompilerParams(
            dimension_semantics=("arbitrary",), vmem_limit_bytes=VMEM_LIMIT),
        name="in_proj",
    )(x2, gain, w_bf16, cos_t, sin_t)


def _select_bias_t(gate_t, n_past):
    slot = lax.broadcasted_iota(jnp.int32, gate_t.shape, 0)
    slot_f = slot.astype(jnp.float32)
    neg_inf = jnp.float32(-jnp.inf)
    g = jnp.where(slot < n_past, gate_t, neg_inf)
    bias = jnp.full(gate_t.shape, MASK_BIAS, dtype=jnp.float32)
    for _ in range(MOBA_TOPK):
        best = jnp.max(g, axis=0, keepdims=True)
        first = jnp.min(jnp.where(g == best, slot_f, float(gate_t.shape[0])), axis=0, keepdims=True)
        pick = (slot_f == first) & (best > neg_inf)
        bias = jnp.where(pick, 0.0, bias)
        g = jnp.where(pick, neg_inf, g)
    return bias


def _fold_rows(t, op):
    return op(t.reshape(t.shape[0] // SUBLANES, SUBLANES, t.shape[1]), axis=0)


def _moba_kernel(qt_ref, k_ref, vt_ref, onehot_ref, kmean_ref, ga_ref, o_ref, qaug_ref, s_ref, own_ref):
    blk = MOBA_BLOCK
    n_blocks = kmean_ref.shape[0]
    heads = range(HEADS_PER_TILE)

    feat = lax.broadcasted_iota(jnp.int32, (LANES, blk), 0)
    km = kmean_ref[:, 0, :]
    km_hi = km.astype(jnp.bfloat16)
    km_lo = (km - km_hi.astype(jnp.float32)).astype(jnp.bfloat16)
    key_pos = lax.broadcasted_iota(jnp.int32, (blk, blk), 0)
    qry_pos = lax.broadcasted_iota(jnp.int32, (blk, blk), 1)
    causal = key_pos <= qry_pos

    def block_rows(j):
        return pl.ds(pl.multiple_of(j * blk, blk), blk)

    def prepare(j):
        qt = qt_ref[j].astype(jnp.float32)
        k_own = k_ref[block_rows(j), :]
        m_init = []
        for h in heads:
            qh = jnp.where((feat // HEAD_DIM) == h, qt, 0.0).astype(jnp.bfloat16)
            gate_t = (jnp.dot(km_hi, qh, preferred_element_type=jnp.float32)
                      + jnp.dot(km_lo, qh, preferred_element_type=jnp.float32))
            qaug_ref[h, :LANES, :] = qh
            qaug_ref[h, LANES:LANES + n_blocks, :] = _select_bias_t(gate_t, j).astype(jnp.bfloat16)
            qaug_ref[h, LANES + n_blocks:, :] = jnp.zeros((LANES - n_blocks, blk), jnp.bfloat16)
            s = jnp.dot(k_own, qh, preferred_element_type=jnp.float32)
            s = jnp.where(causal, s, -jnp.inf)
            own_ref[h] = s
            m_init.append(_fold_rows(s, jnp.max))
        return tuple(m_init)

    def weighted_values(s, n, h, m_fin):
        p = jnp.exp2(s - m_fin).astype(jnp.bfloat16)
        v_aug = vt_ref[n, h * VT_HEAD_ROWS:(h + 1) * VT_HEAD_ROWS, :]
        r = jnp.dot(v_aug, p, preferred_element_type=jnp.float32)
        return r[:HEAD_DIM], r[HEAD_DIM:HEAD_DIM + SUBLANES]

    def own_values(j, m_fin):
        return tuple(weighted_values(own_ref[h], j, h, m_fin[h]) for h in heads)

    def column_max(m_run):
        return tuple(jnp.max(m, axis=0, keepdims=True) for m in m_run)

    def emit(j, acc):
        out_t = jnp.concatenate([o / l[:1] for o, l in acc], axis=0)
        rows = block_rows(j)
        o_ref[rows, :] = (out_t.T * _silu(ga_ref[rows, :])).astype(o_ref.dtype)

    def group_body(finish_old, score_new, m_prev):
        def body(g, carry):
            m_run, acc = [list(c) for c in carry]
            for i in range(KEY_UNROLL):
                n = g * KEY_UNROLL + i
                if score_new:
                    k_aug = jnp.concatenate([k_ref[block_rows(n), :], onehot_ref[block_rows(n), :]], axis=1)
                for h in heads:
                    if finish_old:
                        o, l = weighted_values(s_ref[h, n], n, h, m_prev[h])
                        acc[h] = (acc[h][0] + o, acc[h][1] + l)
                    if score_new:
                        s = jnp.dot(k_aug, qaug_ref[h], preferred_element_type=jnp.float32)
                        s_ref[h, n] = s
                        m_run[h] = jnp.maximum(m_run[h], _fold_rows(s, jnp.max))
            return tuple(m_run), tuple(acc)
        return body

    def groups(n_past):
        return (n_past + KEY_UNROLL - 1) // KEY_UNROLL

    m_fin0 = column_max(prepare(0))
    acc0 = own_values(0, m_fin0)
    m_init1 = prepare(1)

    def step(t, carry):
        m_init, acc, m_prev = carry
        both = groups(t - 1)
        inner = (m_init, acc)
        inner = lax.fori_loop(0, both, group_body(True, True, m_prev), inner)
        inner = lax.fori_loop(both, groups(t), group_body(False, True, m_prev), inner)
        m_run, acc = inner
        m_fin = column_max(m_run)
        emit(t - 1, acc)
        acc_next = own_values(t, m_fin)
        m_next = prepare(jnp.minimum(t + 1, n_blocks - 1))
        return m_next, acc_next, m_fin

    _, acc, m_prev = lax.fori_loop(1, n_blocks, step, (m_init1, acc0, m_fin0))

    last = n_blocks - 1
    unused_max = tuple(jnp.zeros((SUBLANES, blk), jnp.float32) for _ in heads)
    _, acc = lax.fori_loop(0, groups(last), group_body(True, False, m_prev), (unused_max, acc))
    emit(last, acc)


def _moba_attention(qt, k, vt, onehot, kmean, gate_attn):
    batch, seq, _ = k.shape
    n_blocks = seq // MOBA_BLOCK
    slots = -(-(n_blocks - 1) // KEY_UNROLL) * KEY_UNROLL
    per_block = lambda r: pl.BlockSpec((None, n_blocks, None, r, MOBA_BLOCK), lambda b, hp: (b, 0, hp, 0, 0))
    per_row = pl.BlockSpec((None, seq, LANES), lambda b, hp: (b, 0, hp))
    return pl.pallas_call(
        _moba_kernel,
        grid=(batch, HEAD_TILES),
        in_specs=[
            per_block(LANES),
            per_row,
            per_block(VT_ROWS),
            pl.BlockSpec((seq, LANES), lambda b, hp: (0, 0), pipeline_mode=pl.Buffered(1)),
            pl.BlockSpec((n_blocks, 1, LANES), lambda b, hp: (b, 0, hp)),
            per_row,
        ],
        out_specs=per_row,
        out_shape=jax.ShapeDtypeStruct((batch, seq, ATTN_WIDTH), jnp.bfloat16),
        scratch_shapes=[
            pltpu.VMEM((HEADS_PER_TILE, 2 * LANES, MOBA_BLOCK), jnp.bfloat16),
            pltpu.VMEM((HEADS_PER_TILE, slots, MOBA_BLOCK, MOBA_BLOCK), jnp.float32),
            pltpu.VMEM((HEADS_PER_TILE, MOBA_BLOCK, MOBA_BLOCK), jnp.float32),
        ],
        compiler_params=pltpu.CompilerParams(
            dimension_semantics=("arbitrary", "arbitrary"), vmem_limit_bytes=VMEM_LIMIT),
        name="moba_attention",
    )(qt, k, vt, onehot, kmean, gate_attn)


def _out_proj_kernel(apply_final_norm, x_ref, attn_ref, u_ref, uprev_ref, gp_ref, wpool_ref, bpool_ref,
                     pscale_ref, wout_ref, fgain_ref, o_ref, ext_ref):
    i = pl.program_id(1)
    halo = uprev_ref[...]
    ext_ref[:POOL_HALO, :] = jnp.where(i > 0, halo, jnp.zeros_like(halo))
    ext_ref[POOL_HALO:, :] = u_ref[...]
    pos = i * ROW_TILE + lax.broadcasted_iota(jnp.int32, (ROW_TILE, 1), 0)

    pooled_parts = []
    for g, window in enumerate(POOL_WINDOWS):
        sl = slice(g * POOL_GROUP, (g + 1) * POOL_GROUP)
        u_g = ext_ref[POOL_HALO:, sl]
        total = u_g
        for t in range(1, window):
            total = total + ext_ref[POOL_HALO - t:POOL_HALO - t + ROW_TILE, sl]
        count = jnp.minimum(pos + 1, window).astype(jnp.float32)
        pooled = total / count - u_g
        y = jnp.dot(pooled.astype(jnp.bfloat16), wpool_ref[g], preferred_element_type=jnp.float32)
        pooled_parts.append(y + bpool_ref[g])
    pool = jnp.concatenate(pooled_parts, axis=1) * pscale_ref[...]
    pool = (pool * _silu(gp_ref[...])).astype(jnp.bfloat16)

    mixed = jnp.concatenate([attn_ref[...], pool], axis=1)
    y = x_ref[...] + jnp.dot(mixed, wout_ref[...], preferred_element_type=jnp.float32)
    if apply_final_norm:
        inv = lax.rsqrt(jnp.mean(y * y, axis=-1, keepdims=True) + EPS)
        y = y * inv * fgain_ref[...]
    o_ref[...] = y


def _out_proj(x, attn, u, gate_pool, w_pool, b_pool, pool_scale, w_out, final_gain, apply_final_norm):
    batch, seq, _ = x.shape
    tiles = seq // ROW_TILE
    halo_per_tile = ROW_TILE // POOL_HALO
    row_spec = lambda w: pl.BlockSpec((None, ROW_TILE, w), lambda b, i: (b, i, 0))
    const2 = lambda shape: pl.BlockSpec(shape, lambda b, i: (0, 0))
    const3 = lambda shape: pl.BlockSpec(shape, lambda b, i: (0, 0, 0))
    return pl.pallas_call(
        functools.partial(_out_proj_kernel, apply_final_norm),
        grid=(batch, tiles),
        in_specs=[
            row_spec(D_MODEL),
            row_spec(ATTN_WIDTH),
            row_spec(POOL_WIDTH),
            pl.BlockSpec((None, POOL_HALO, POOL_WIDTH),
                         lambda b, i: (b, jnp.maximum(i * halo_per_tile - 1, 0), 0)),
            row_spec(POOL_WIDTH),
            const3((len(POOL_WINDOWS), POOL_GROUP, POOL_GROUP)),
            const3((len(POOL_WINDOWS), 1, POOL_GROUP)),
            const2((1, POOL_WIDTH)),
            const2((D_MODEL, D_MODEL)),
            const2((1, D_MODEL)),
        ],
        out_specs=row_spec(D_MODEL),
        out_shape=jax.ShapeDtypeStruct((batch, seq, D_MODEL), jnp.float32),
        scratch_shapes=[pltpu.VMEM((POOL_HALO + ROW_TILE, POOL_WIDTH), jnp.float32)],
        compiler_params=pltpu.CompilerParams(
            dimension_semantics=("arbitrary", "arbitrary"), vmem_limit_bytes=VMEM_LIMIT),
        name="pool_out_proj",
    )(x, attn, u, u, gate_pool, w_pool, b_pool, pool_scale, w_out, final_gain)


def _position_tables(seq):
    pos = np.arange(seq, dtype=np.float64)
    inv_freq = 1.0 / (ROPE_THETA ** (np.arange(0, HEAD_DIM, 2, dtype=np.float64) / HEAD_DIM))
    ang = pos[:, None] * inv_freq[None, :]
    cos, sin = np.cos(ang), np.sin(ang)
    cos_t = np.tile(cos, (1, LANES // (HEAD_DIM // 2)))
    sin_t = np.tile(np.concatenate([-sin, sin], axis=1), (1, HEADS_PER_TILE))
    onehot = np.arange(seq)[:, None] // MOBA_BLOCK == np.arange(LANES)[None, :]
    return (jnp.asarray(cos_t, jnp.float32), jnp.asarray(sin_t, jnp.float32),
            jnp.asarray(onehot, jnp.bfloat16))


def kernel(x, norm_gain, w_in, w_pool, b_pool, pool_scale, w_out, final_gain):
    batch, seq, d_model = x.shape
    depth = w_in.shape[0]
    assert d_model == D_MODEL and seq % ROW_TILE == 0 and 2 <= seq // MOBA_BLOCK <= LANES
    n_blocks = seq // MOBA_BLOCK
    cos_t, sin_t, onehot = _position_tables(seq)

    for l in range(depth):
        qt, k, vt, gate_attn, u, gate_pool, kmean = _in_proj(
            x.reshape(batch * seq, d_model), norm_gain[l][None, :], w_in[l].astype(jnp.bfloat16),
            cos_t, sin_t, seq)
        shape3 = lambda t: t.reshape(batch, seq, t.shape[-1])
        per_block = lambda t: t.reshape(batch, n_blocks, HEAD_TILES, t.shape[-2], MOBA_BLOCK)
        attn = _moba_attention(per_block(qt), shape3(k), per_block(vt), onehot, kmean, shape3(gate_attn))
        x = _out_proj(x, attn, shape3(u), shape3(gate_pool), w_pool[l].astype(jnp.bfloat16),
                      b_pool[l][:, None, :], pool_scale[l][None, :], w_out[l].astype(jnp.bfloat16),
                      final_gain[None, :], apply_final_norm=(l == depth - 1))
    return x
```

```python
import functools

import numpy as np

import jax
import jax.numpy as jnp
from jax import lax
from jax.experimental import pallas as pl
from jax.experimental.pallas import tpu as pltpu

D_MODEL = 1024
ATTN_WIDTH = D_MODEL // 2
POOL_WIDTH = D_MODEL - ATTN_WIDTH
HEAD_DIM = 64
MOBA_BLOCK = 256
MOBA_TOPK = 3
POOL_WINDOWS = (2, 4, 8, 16)
POOL_GROUP = POOL_WIDTH // len(POOL_WINDOWS)
ROPE_THETA = 10000.0
EPS = 1e-6
IN_WIDTH = 4 * ATTN_WIDTH + 2 * POOL_WIDTH

LANES = 128
SUBLANES = 8
HEADS_PER_TILE = LANES // HEAD_DIM
HEAD_TILES = ATTN_WIDTH // LANES
ROW_TILE = 512
POOL_HALO = 16
MASK_BIAS = -1e30
KEY_UNROLL = 8
HALF_UNROLL = KEY_UNROLL // 2
BF16_ROWS = 16
VT_HEAD_ROWS = HEAD_DIM + BF16_ROWS
VT_ROWS = HEADS_PER_TILE * VT_HEAD_ROWS
Q_SCALE = HEAD_DIM ** -0.5 * 1.4426950408889634
VMEM_LIMIT = 56 * 1024 * 1024


def _silu(t):
    return t * (1.0 / (1.0 + jnp.exp(-t)))


def _rope(t, cos, sin_signed, first_half):
    swapped = jnp.where(first_half, pltpu.roll(t, LANES - HEAD_DIM // 2, axis=1),
                        pltpu.roll(t, HEAD_DIM // 2, axis=1))
    return t * cos + swapped * sin_signed


def _in_proj_kernel(x_ref, gain_ref, w_ref, cos_ref, sin_ref,
                    qt_ref, k_ref, vt_ref, ga_ref, u_ref, gp_ref, kmean_ref):
    x = x_ref[...]
    inv = lax.rsqrt(jnp.mean(x * x, axis=-1, keepdims=True) + EPS)
    h = (x * inv * gain_ref[...]).astype(jnp.bfloat16)

    def proj(c):
        return jnp.dot(h, w_ref[:, c * ATTN_WIDTH:(c + 1) * ATTN_WIDTH],
                       preferred_element_type=jnp.float32)

    cos = cos_ref[...]
    sin = sin_ref[...]
    lane = lax.broadcasted_iota(jnp.int32, (ROW_TILE, LANES), 1)
    first_half = (lane % HEAD_DIM) < (HEAD_DIM // 2)

    q = proj(0)
    k = proj(1)
    v = proj(2)
    for t in range(HEAD_TILES):
        sl = slice(t * LANES, (t + 1) * LANES)
        qr = _rope(q[:, sl], cos, sin, first_half) * Q_SCALE
        kr = _rope(k[:, sl], cos, sin, first_half)
        k_ref[:, sl] = kr.astype(k_ref.dtype)
        for b in range(ROW_TILE // MOBA_BLOCK):
            blk_rows = slice(b * MOBA_BLOCK, (b + 1) * MOBA_BLOCK)
            kmean_ref[b, :, sl] = jnp.sum(kr[blk_rows], axis=0, keepdims=True) * (1.0 / MOBA_BLOCK)
            qt_ref[b, t] = qr[blk_rows].T.astype(qt_ref.dtype)
            v_t = v[blk_rows, sl].T.astype(vt_ref.dtype)
            for head in range(HEADS_PER_TILE):
                base = head * VT_HEAD_ROWS
                vt_ref[b, t, base:base + HEAD_DIM] = v_t[head * HEAD_DIM:(head + 1) * HEAD_DIM]
                vt_ref[b, t, base + HEAD_DIM:base + VT_HEAD_ROWS] = jnp.ones((BF16_ROWS, MOBA_BLOCK),
                                                                             vt_ref.dtype)
    ga_ref[...] = proj(3)
    u_ref[...] = proj(4)
    gp_ref[...] = proj(5)


def _in_proj(x2, gain, w_bf16, cos_t, sin_t, seq):
    rows = x2.shape[0]
    n_tiles = rows // ROW_TILE
    seq_tiles = seq // ROW_TILE
    row_spec = lambda w: pl.BlockSpec((ROW_TILE, w), lambda i: (i, 0))
    tab_spec = pl.BlockSpec((ROW_TILE, LANES), lambda i: (i % seq_tiles, 0))
    blocks_per_tile = ROW_TILE // MOBA_BLOCK
    n_blocks = rows // MOBA_BLOCK
    transposed = lambda r: jax.ShapeDtypeStruct((n_blocks, HEAD_TILES, r, MOBA_BLOCK), jnp.bfloat16)
    transposed_spec = lambda r: pl.BlockSpec((blocks_per_tile, HEAD_TILES, r, MOBA_BLOCK),
                                             lambda i: (i, 0, 0, 0))
    out_shape = (
        transposed(LANES),
        jax.ShapeDtypeStruct((rows, ATTN_WIDTH), jnp.bfloat16),
        transposed(VT_ROWS),
        jax.ShapeDtypeStruct((rows, ATTN_WIDTH), jnp.float32),
        jax.ShapeDtypeStruct((rows, POOL_WIDTH), jnp.float32),
        jax.ShapeDtypeStruct((rows, POOL_WIDTH), jnp.float32),
        jax.ShapeDtypeStruct((n_blocks, 1, ATTN_WIDTH), jnp.float32),
    )
    return pl.pallas_call(
        _in_proj_kernel,
        grid=(n_tiles,),
        in_specs=[
            row_spec(D_MODEL),
            pl.BlockSpec((1, D_MODEL), lambda i: (0, 0)),
            pl.BlockSpec((D_MODEL, IN_WIDTH), lambda i: (0, 0)),
            tab_spec, tab_spec,
        ],
        out_specs=(
            transposed_spec(LANES), row_spec(ATTN_WIDTH), transposed_spec(VT_ROWS),
            row_spec(ATTN_WIDTH), row_spec(POOL_WIDTH), row_spec(POOL_WIDTH),
            pl.BlockSpec((blocks_per_tile, 1, ATTN_WIDTH), lambda i: (i, 0, 0)),
        ),
        out_shape=out_shape,
        compiler_params=pltpu.CompilerParams(
            dimension_semantics=("arbitrary",), vmem_limit_bytes=VMEM_LIMIT),
        name="in_proj",
    )(x2, gain, w_bf16, cos_t, sin_t)


def _select_bias_t(gate_t, n_past):
    slot = lax.broadcasted_iota(jnp.int32, gate_t.shape, 0)
    slot_f = slot.astype(jnp.float32)
    neg_inf = jnp.float32(-jnp.inf)
    g = jnp.where(slot < n_past, gate_t, neg_inf)
    bias = jnp.full(gate_t.shape, MASK_BIAS, dtype=jnp.float32)
    for _ in range(MOBA_TOPK):
        best = jnp.max(g, axis=0, keepdims=True)
        first = jnp.min(jnp.where(g == best, slot_f, float(gate_t.shape[0])), axis=0, keepdims=True)
        pick = (slot_f == first) & (best > neg_inf)
        bias = jnp.where(pick, 0.0, bias)
        g = jnp.where(pick, neg_inf, g)
    return bias


def _visited_tiles(n_past):
    rest = n_past % KEY_UNROLL
    whole = n_past - rest
    if isinstance(n_past, int):
        return whole + (0 if rest == 0 else HALF_UNROLL if rest <= HALF_UNROLL else KEY_UNROLL)
    return whole + jnp.where(rest == 0, 0, jnp.where(rest <= HALF_UNROLL, HALF_UNROLL, KEY_UNROLL))


def _fold_rows(t, op):
    return op(t.reshape(t.shape[0] // SUBLANES, SUBLANES, t.shape[1]), axis=0)


def _moba_kernel(qt_ref, k_ref, vt_ref, onehot_ref, kmean_ref, ga_ref, o_ref, qaug_ref, s_ref, own_ref):
    blk = MOBA_BLOCK
    n_blocks = kmean_ref.shape[0]
    heads = range(HEADS_PER_TILE)

    feat = lax.broadcasted_iota(jnp.int32, (LANES, blk), 0)
    km = kmean_ref[:, 0, :]
    km_hi = km.astype(jnp.bfloat16)
    km_lo = (km - km_hi.astype(jnp.float32)).astype(jnp.bfloat16)
    key_pos = lax.broadcasted_iota(jnp.int32, (blk, blk), 0)
    qry_pos = lax.broadcasted_iota(jnp.int32, (blk, blk), 1)
    causal = key_pos <= qry_pos

    def block_rows(j):
        return pl.ds(pl.multiple_of(j * blk, blk), blk)

    def prepare(j):
        qt = qt_ref[j].astype(jnp.float32)
        k_own = k_ref[block_rows(j), :]
        m_init = []
        for h in heads:
            qh = jnp.where((feat // HEAD_DIM) == h, qt, 0.0).astype(jnp.bfloat16)
            gate_t = (jnp.dot(km_hi, qh, preferred_element_type=jnp.float32)
                      + jnp.dot(km_lo, qh, preferred_element_type=jnp.float32))
            qaug_ref[h, :LANES, :] = qh
            qaug_ref[h, LANES:LANES + n_blocks, :] = _select_bias_t(gate_t, j).astype(jnp.bfloat16)
            qaug_ref[h, LANES + n_blocks:, :] = jnp.zeros((LANES - n_blocks, blk), jnp.bfloat16)
            s = jnp.dot(k_own, qh, preferred_element_type=jnp.float32)
            s = jnp.where(causal, s, -jnp.inf)
            own_ref[h] = s
            m_init.append(_fold_rows(s, jnp.max))
        return tuple(m_init)

    def weighted_values(s, n, h, m_fin):
        p = jnp.exp2(s - m_fin).astype(jnp.bfloat16)
        v_aug = vt_ref[n, h * VT_HEAD_ROWS:(h + 1) * VT_HEAD_ROWS, :]
        r = jnp.dot(v_aug, p, preferred_element_type=jnp.float32)
        return r[:HEAD_DIM], r[HEAD_DIM:HEAD_DIM + SUBLANES]

    def own_values(j, m_fin):
        return tuple(weighted_values(own_ref[h], j, h, m_fin[h]) for h in heads)

    def column_max(m_run):
        return tuple(jnp.max(m, axis=0, keepdims=True) for m in m_run)

    def emit(j, acc):
        out_t = jnp.concatenate([o / l[:1] for o, l in acc], axis=0)
        rows = block_rows(j)
        o_ref[rows, :] = (out_t.T * _silu(ga_ref[rows, :])).astype(o_ref.dtype)

    def group_body(finish_old, score_new, m_prev, unroll=KEY_UNROLL, first=0):
        def body(g, carry):
            m_run, acc = [list(c) for c in carry]
            for i in range(unroll):
                n = first + g * unroll + i
                if score_new:
                    k_aug = jnp.concatenate([k_ref[block_rows(n), :], onehot_ref[block_rows(n), :]], axis=1)
                for h in heads:
                    if finish_old:
                        o, l = weighted_values(s_ref[h, n], n, h, m_prev[h])
                        acc[h] = (acc[h][0] + o, acc[h][1] + l)
                    if score_new:
                        s = jnp.dot(k_aug, qaug_ref[h], preferred_element_type=jnp.float32)
                        s_ref[h, n] = s
                        m_run[h] = jnp.maximum(m_run[h], _fold_rows(s, jnp.max))
            return tuple(m_run), tuple(acc)
        return body

    m_fin0 = column_max(prepare(0))
    acc0 = own_values(0, m_fin0)
    m_init1 = prepare(1)

    def step(t, carry):
        m_init, acc, m_prev = carry
        old_tiles = _visited_tiles(t - 1)
        whole = old_tiles // KEY_UNROLL
        half = (old_tiles - whole * KEY_UNROLL) // HALF_UNROLL
        extra = (_visited_tiles(t) - old_tiles) // HALF_UNROLL
        inner = (m_init, acc)
        inner = lax.fori_loop(0, whole, group_body(True, True, m_prev), inner)
        inner = lax.fori_loop(0, half, group_body(True, True, m_prev, HALF_UNROLL, whole * KEY_UNROLL), inner)
        inner = lax.fori_loop(0, extra, group_body(False, True, m_prev, HALF_UNROLL, old_tiles), inner)
        m_run, acc = inner
        m_fin = column_max(m_run)
        emit(t - 1, acc)
        acc_next = own_values(t, m_fin)
        m_next = prepare(jnp.minimum(t + 1, n_blocks - 1))
        return m_next, acc_next, m_fin

    _, acc, m_prev = lax.fori_loop(1, n_blocks, step, (m_init1, acc0, m_fin0))

    last = n_blocks - 1
    last_tiles = _visited_tiles(last)
    whole = last_tiles // KEY_UNROLL
    unused_max = tuple(jnp.zeros((SUBLANES, blk), jnp.float32) for _ in heads)
    inner = lax.fori_loop(0, whole, group_body(True, False, m_prev), (unused_max, acc))
    if last_tiles % KEY_UNROLL:
        inner = group_body(True, False, m_prev, HALF_UNROLL, whole * KEY_UNROLL)(0, inner)
    emit(last, inner[1])


def _moba_attention(qt, k, vt, onehot, kmean, gate_attn):
    batch, seq, _ = k.shape
    n_blocks = seq // MOBA_BLOCK
    slots = _visited_tiles(n_blocks - 1)
    per_block = lambda r: pl.BlockSpec((None, n_blocks, None, r, MOBA_BLOCK), lambda b, hp: (b, 0, hp, 0, 0))
    per_row = pl.BlockSpec((None, seq, LANES), lambda b, hp: (b, 0, hp))
    return pl.pallas_call(
        _moba_kernel,
        grid=(batch, HEAD_TILES),
        in_specs=[
            per_block(LANES),
            per_row,
            per_block(VT_ROWS),
            pl.BlockSpec((seq, LANES), lambda b, hp: (0, 0), pipeline_mode=pl.Buffered(1)),
            pl.BlockSpec((n_blocks, 1, LANES), lambda b, hp: (b, 0, hp)),
            per_row,
        ],
        out_specs=per_row,
        out_shape=jax.ShapeDtypeStruct((batch, seq, ATTN_WIDTH), jnp.bfloat16),
        scratch_shapes=[
            pltpu.VMEM((HEADS_PER_TILE, 2 * LANES, MOBA_BLOCK), jnp.bfloat16),
            pltpu.VMEM((HEADS_PER_TILE, slots, MOBA_BLOCK, MOBA_BLOCK), jnp.float32),
            pltpu.VMEM((HEADS_PER_TILE, MOBA_BLOCK, MOBA_BLOCK), jnp.float32),
        ],
        compiler_params=pltpu.CompilerParams(
            dimension_semantics=("arbitrary", "arbitrary"), vmem_limit_bytes=VMEM_LIMIT),
        name="moba_attention",
    )(qt, k, vt, onehot, kmean, gate_attn)


def _out_proj_kernel(apply_final_norm, x_ref, attn_ref, u_ref, uprev_ref, gp_ref, wpool_ref, bpool_ref,
                     pscale_ref, wout_ref, fgain_ref, o_ref, ext_ref):
    i = pl.program_id(1)
    halo = uprev_ref[...]
    ext_ref[:POOL_HALO, :] = jnp.where(i > 0, halo, jnp.zeros_like(halo))
    ext_ref[POOL_HALO:, :] = u_ref[...]
    pos = i * ROW_TILE + lax.broadcasted_iota(jnp.int32, (ROW_TILE, 1), 0)

    pooled_parts = []
    for g, window in enumerate(POOL_WINDOWS):
        sl = slice(g * POOL_GROUP, (g + 1) * POOL_GROUP)
        u_g = ext_ref[POOL_HALO:, sl]
        total = u_g
        for t in range(1, window):
            total = total + ext_ref[POOL_HALO - t:POOL_HALO - t + ROW_TILE, sl]
        count = jnp.minimum(pos + 1, window).astype(jnp.float32)
        pooled = total / count - u_g
        y = jnp.dot(pooled.astype(jnp.bfloat16), wpool_ref[g], preferred_element_type=jnp.float32)
        pooled_parts.append(y + bpool_ref[g])
    pool = jnp.concatenate(pooled_parts, axis=1) * pscale_ref[...]
    pool = (pool * _silu(gp_ref[...])).astype(jnp.bfloat16)

    mixed = jnp.concatenate([attn_ref[...], pool], axis=1)
    y = x_ref[...] + jnp.dot(mixed, wout_ref[...], preferred_element_type=jnp.float32)
    if apply_final_norm:
        inv = lax.rsqrt(jnp.mean(y * y, axis=-1, keepdims=True) + EPS)
        y = y * inv * fgain_ref[...]
    o_ref[...] = y


def _out_proj(x, attn, u, gate_pool, w_pool, b_pool, pool_scale, w_out, final_gain, apply_final_norm):
    batch, seq, _ = x.shape
    tiles = seq // ROW_TILE
    halo_per_tile = ROW_TILE // POOL_HALO
    row_spec = lambda w: pl.BlockSpec((None, ROW_TILE, w), lambda b, i: (b, i, 0))
    const2 = lambda shape: pl.BlockSpec(shape, lambda b, i: (0, 0))
    const3 = lambda shape: pl.BlockSpec(shape, lambda b, i: (0, 0, 0))
    return pl.pallas_call(
        functools.partial(_out_proj_kernel, apply_final_norm),
        grid=(batch, tiles),
        in_specs=[
            row_spec(D_MODEL),
            row_spec(ATTN_WIDTH),
            row_spec(POOL_WIDTH),
            pl.BlockSpec((None, POOL_HALO, POOL_WIDTH),
                         lambda b, i: (b, jnp.maximum(i * halo_per_tile - 1, 0), 0)),
            row_spec(POOL_WIDTH),
            const3((len(POOL_WINDOWS), POOL_GROUP, POOL_GROUP)),
            const3((len(POOL_WINDOWS), 1, POOL_GROUP)),
            const2((1, POOL_WIDTH)),
            const2((D_MODEL, D_MODEL)),
            const2((1, D_MODEL)),
        ],
        out_specs=row_spec(D_MODEL),
        out_shape=jax.ShapeDtypeStruct((batch, seq, D_MODEL), jnp.float32),
        scratch_shapes=[pltpu.VMEM((POOL_HALO + ROW_TILE, POOL_WIDTH), jnp.float32)],
        compiler_params=pltpu.CompilerParams(
            dimension_semantics=("arbitrary", "arbitrary"), vmem_limit_bytes=VMEM_LIMIT),
        name="pool_out_proj",
    )(x, attn, u, u, gate_pool, w_pool, b_pool, pool_scale, w_out, final_gain)


def _position_tables(seq):
    pos = np.arange(seq, dtype=np.float64)
    inv_freq = 1.0 / (ROPE_THETA ** (np.arange(0, HEAD_DIM, 2, dtype=np.float64) / HEAD_DIM))
    ang = pos[:, None] * inv_freq[None, :]
    cos, sin = np.cos(ang), np.sin(ang)
    cos_t = np.tile(cos, (1, LANES // (HEAD_DIM // 2)))
    sin_t = np.tile(np.concatenate([-sin, sin], axis=1), (1, HEADS_PER_TILE))
    onehot = np.arange(seq)[:, None] // MOBA_BLOCK == np.arange(LANES)[None, :]
    return (jnp.asarray(cos_t, jnp.float32), jnp.asarray(sin_t, jnp.float32),
            jnp.asarray(onehot, jnp.bfloat16))


def kernel(x, norm_gain, w_in, w_pool, b_pool, pool_scale, w_out, final_gain):
    batch, seq, d_model = x.shape
    depth = w_in.shape[0]
    assert d_model == D_MODEL and seq % ROW_TILE == 0 and 2 <= seq // MOBA_BLOCK <= LANES
    n_blocks = seq // MOBA_BLOCK
    cos_t, sin_t, onehot = _position_tables(seq)

    for l in range(depth):
        qt, k, vt, gate_attn, u, gate_pool, kmean = _in_proj(
            x.reshape(batch * seq, d_model), norm_gain[l][None, :], w_in[l].astype(jnp.bfloat16),
            cos_t, sin_t, seq)
        shape3 = lambda t: t.reshape(batch, seq, t.shape[-1])
        per_block = lambda t: t.reshape(batch, n_blocks, HEAD_TILES, t.shape[-2], MOBA_BLOCK)
        attn = _moba_attention(per_block(qt), shape3(k), per_block(vt), onehot, kmean, shape3(gate_attn))
        x = _out_proj(x, attn, shape3(u), shape3(gate_pool), w_pool[l].astype(jnp.bfloat16),
                      b_pool[l][:, None, :], pool_scale[l][None, :], w_out[l].astype(jnp.bfloat16),
                      final_gain[None, :], apply_final_norm=(l == depth - 1))
    return x
```

```python
import functools

import numpy as np

import jax
import jax.numpy as jnp
from jax import lax
from jax.experimental import pallas as pl
from jax.experimental.pallas import tpu as pltpu

D_MODEL = 1024
ATTN_WIDTH = D_MODEL // 2
POOL_WIDTH = D_MODEL - ATTN_WIDTH
HEAD_DIM = 64
MOBA_BLOCK = 256
MOBA_TOPK = 3
POOL_WINDOWS = (2, 4, 8, 16)
POOL_GROUP = POOL_WIDTH // len(POOL_WINDOWS)
ROPE_THETA = 10000.0
EPS = 1e-6
IN_WIDTH = 4 * ATTN_WIDTH + 2 * POOL_WIDTH

LANES = 128
SUBLANES = 8
HEADS_PER_TILE = LANES // HEAD_DIM
HEAD_TILES = ATTN_WIDTH // LANES
ROW_TILE = 512
POOL_HALO = 16
MASK_BIAS = -1e30
KEY_UNROLL = 8
HALF_UNROLL = KEY_UNROLL // 2
BF16_ROWS = 16
VT_HEAD_ROWS = HEAD_DIM + BF16_ROWS
VT_ROWS = HEADS_PER_TILE * VT_HEAD_ROWS
Q_SCALE = HEAD_DIM ** -0.5 * 1.4426950408889634
VMEM_LIMIT = 56 * 1024 * 1024


def _silu(t):
    return t * (1.0 / (1.0 + jnp.exp(-t)))


def _rope(t, cos, sin_signed, first_half):
    swapped = jnp.where(first_half, pltpu.roll(t, LANES - HEAD_DIM // 2, axis=1),
                        pltpu.roll(t, HEAD_DIM // 2, axis=1))
    return t * cos + swapped * sin_signed


def _in_proj_kernel(x_ref, gain_ref, w_ref, cos_ref, sin_ref,
                    qt_ref, k_ref, vt_ref, ga_ref, u_ref, gp_ref, kmean_ref):
    x = x_ref[...]
    inv = lax.rsqrt(jnp.mean(x * x, axis=-1, keepdims=True) + EPS)
    h = (x * inv * gain_ref[...]).astype(jnp.bfloat16)

    def proj(c):
        return jnp.dot(h, w_ref[:, c * ATTN_WIDTH:(c + 1) * ATTN_WIDTH],
                       preferred_element_type=jnp.float32)

    cos = cos_ref[...]
    sin = sin_ref[...]
    lane = lax.broadcasted_iota(jnp.int32, (ROW_TILE, LANES), 1)
    first_half = (lane % HEAD_DIM) < (HEAD_DIM // 2)

    q = proj(0)
    k = proj(1)
    v = proj(2)
    for t in range(HEAD_TILES):
        sl = slice(t * LANES, (t + 1) * LANES)
        qr = _rope(q[:, sl], cos, sin, first_half) * Q_SCALE
        kr = _rope(k[:, sl], cos, sin, first_half)
        k_ref[:, sl] = kr.astype(k_ref.dtype)
        for b in range(ROW_TILE // MOBA_BLOCK):
            blk_rows = slice(b * MOBA_BLOCK, (b + 1) * MOBA_BLOCK)
            kmean_ref[b, :, sl] = jnp.sum(kr[blk_rows], axis=0, keepdims=True) * (1.0 / MOBA_BLOCK)
            qt_ref[b, t] = qr[blk_rows].T.astype(qt_ref.dtype)
            v_t = v[blk_rows, sl].T.astype(vt_ref.dtype)
            for head in range(HEADS_PER_TILE):
                base = head * VT_HEAD_ROWS
                vt_ref[b, t, base:base + HEAD_DIM] = v_t[head * HEAD_DIM:(head + 1) * HEAD_DIM]
                vt_ref[b, t, base + HEAD_DIM:base + VT_HEAD_ROWS] = jnp.ones((BF16_ROWS, MOBA_BLOCK),
                                                                             vt_ref.dtype)
    ga_ref[...] = proj(3)
    u_ref[...] = proj(4)
    gp_ref[...] = proj(5)


def _in_proj(x2, gain, w_bf16, cos_t, sin_t, seq):
    rows = x2.shape[0]
    n_tiles = rows // ROW_TILE
    seq_tiles = seq // ROW_TILE
    row_spec = lambda w: pl.BlockSpec((ROW_TILE, w), lambda i: (i, 0))
    tab_spec = pl.BlockSpec((ROW_TILE, LANES), lambda i: (i % seq_tiles, 0))
    blocks_per_tile = ROW_TILE // MOBA_BLOCK
    n_blocks = rows // MOBA_BLOCK
    transposed = lambda r: jax.ShapeDtypeStruct((n_blocks, HEAD_TILES, r, MOBA_BLOCK), jnp.bfloat16)
    transposed_spec = lambda r: pl.BlockSpec((blocks_per_tile, HEAD_TILES, r, MOBA_BLOCK),
                                             lambda i: (i, 0, 0, 0))
    out_shape = (
        transposed(LANES),
        jax.ShapeDtypeStruct((rows, ATTN_WIDTH), jnp.bfloat16),
        transposed(VT_ROWS),
        jax.ShapeDtypeStruct((rows, ATTN_WIDTH), jnp.float32),
        jax.ShapeDtypeStruct((rows, POOL_WIDTH), jnp.float32),
        jax.ShapeDtypeStruct((rows, POOL_WIDTH), jnp.float32),
        jax.ShapeDtypeStruct((n_blocks, 1, ATTN_WIDTH), jnp.float32),
    )
    return pl.pallas_call(
        _in_proj_kernel,
        grid=(n_tiles,),
        in_specs=[
            row_spec(D_MODEL),
            pl.BlockSpec((1, D_MODEL), lambda i: (0, 0)),
            pl.BlockSpec((D_MODEL, IN_WIDTH), lambda i: (0, 0)),
            tab_spec, tab_spec,
        ],
        out_specs=(
            transposed_spec(LANES), row_spec(ATTN_WIDTH), transposed_spec(VT_ROWS),
            row_spec(ATTN_WIDTH), row_spec(POOL_WIDTH), row_spec(POOL_WIDTH),
            pl.BlockSpec((blocks_per_tile, 1, ATTN_WIDTH), lambda i: (i, 0, 0)),
        ),
        out_shape=out_shape,
        compiler_params=pltpu.CompilerParams(
            dimension_semantics=("arbitrary",), vmem_limit_bytes=VMEM_LIMIT),
        name="in_proj",
    )(x2, gain, w_bf16, cos_t, sin_t)


def _select_bias_t(gate_t, n_past):
    slot = lax.broadcasted_iota(jnp.int32, gate_t.shape, 0)
    slot_f = slot.astype(jnp.float32)
    neg_inf = jnp.float32(-jnp.inf)
    g = jnp.where(slot < n_past, gate_t, neg_inf)
    bias = jnp.full(gate_t.shape, MASK_BIAS, dtype=jnp.float32)
    for _ in range(MOBA_TOPK):
        best = jnp.max(g, axis=0, keepdims=True)
        first = jnp.min(jnp.where(g == best, slot_f, float(gate_t.shape[0])), axis=0, keepdims=True)
        pick = (slot_f == first) & (best > neg_inf)
        bias = jnp.where(pick, 0.0, bias)
        g = jnp.where(pick, neg_inf, g)
    return bias


def _visited_tiles(n_past):
    rest = n_past % KEY_UNROLL
    whole = n_past - rest
    if isinstance(n_past, int):
        return whole + (0 if rest == 0 else HALF_UNROLL if rest <= HALF_UNROLL else KEY_UNROLL)
    return whole + jnp.where(rest == 0, 0, jnp.where(rest <= HALF_UNROLL, HALF_UNROLL, KEY_UNROLL))


def _fold_rows(t, op):
    return op(t.reshape(t.shape[0] // SUBLANES, SUBLANES, t.shape[1]), axis=0)


def _moba_kernel(qt_ref, k_ref, vt_ref, onehot_ref, kmean_ref, ga_ref, o_ref, qaug_ref, s_ref, own_ref):
    blk = MOBA_BLOCK
    n_blocks = kmean_ref.shape[0]
    heads = range(HEADS_PER_TILE)

    feat = lax.broadcasted_iota(jnp.int32, (LANES, blk), 0)
    km = kmean_ref[:, 0, :]
    km_hi = km.astype(jnp.bfloat16)
    km_lo = (km - km_hi.astype(jnp.float32)).astype(jnp.bfloat16)
    key_pos = lax.broadcasted_iota(jnp.int32, (blk, blk), 0)
    qry_pos = lax.broadcasted_iota(jnp.int32, (blk, blk), 1)
    causal = key_pos <= qry_pos

    def block_rows(j):
        return pl.ds(pl.multiple_of(j * blk, blk), blk)

    def prepare(j):
        qt = qt_ref[j].astype(jnp.float32)
        k_own = k_ref[block_rows(j), :]
        m_init = []
        for h in heads:
            qh = jnp.where((feat // HEAD_DIM) == h, qt, 0.0).astype(jnp.bfloat16)
            gate_t = (jnp.dot(km_hi, qh, preferred_element_type=jnp.float32)
                      + jnp.dot(km_lo, qh, preferred_element_type=jnp.float32))
            qaug_ref[h, :LANES, :] = qh
            qaug_ref[h, LANES:LANES + n_blocks, :] = _select_bias_t(gate_t, j).astype(jnp.bfloat16)
            qaug_ref[h, LANES + n_blocks:, :] = jnp.zeros((LANES - n_blocks, blk), jnp.bfloat16)
            s = jnp.dot(k_own, qh, preferred_element_type=jnp.float32)
            s = jnp.where(causal, s, -jnp.inf)
            own_ref[h] = s
            m_init.append(_fold_rows(s, jnp.max))
        return tuple(m_init)

    def probs(s, m_fin):
        return jnp.exp2(s - m_fin).astype(jnp.bfloat16)

    def weighted_values(p, n, h):
        v_aug = vt_ref[n, h * VT_HEAD_ROWS:(h + 1) * VT_HEAD_ROWS, :]
        r = jnp.dot(v_aug, p, preferred_element_type=jnp.float32)
        return r[:HEAD_DIM], r[HEAD_DIM:HEAD_DIM + SUBLANES]

    def own_probs(m_fin):
        return tuple(probs(own_ref[h], m_fin[h]) for h in heads)

    def own_values(j, p_own):
        return tuple(weighted_values(p_own[h], j, h) for h in heads)

    def column_max(m_run):
        return tuple(jnp.max(m, axis=0, keepdims=True) for m in m_run)

    def emit(j, acc):
        out_t = jnp.concatenate([o / l[:1] for o, l in acc], axis=0)
        rows = block_rows(j)
        o_ref[rows, :] = (out_t.T * _silu(ga_ref[rows, :])).astype(o_ref.dtype)

    def group_body(finish_old, score_new, m_prev, unroll=KEY_UNROLL, first=0):
        def body(g, carry):
            m_run, acc = [list(c) for c in carry]
            for i in range(unroll):
                n = first + g * unroll + i
                if score_new:
                    k_aug = jnp.concatenate([k_ref[block_rows(n), :], onehot_ref[block_rows(n), :]], axis=1)
                for h in heads:
                    if finish_old:
                        o, l = weighted_values(probs(s_ref[h, n], m_prev[h]), n, h)
                        acc[h] = (acc[h][0] + o, acc[h][1] + l)
                    if score_new:
                        s = jnp.dot(k_aug, qaug_ref[h], preferred_element_type=jnp.float32)
                        s_ref[h, n] = s
                        m_run[h] = jnp.maximum(m_run[h], _fold_rows(s, jnp.max))
            return tuple(m_run), tuple(acc)
        return body

    m_fin0 = column_max(prepare(0))
    acc0 = own_values(0, own_probs(m_fin0))
    m_init1 = prepare(1)

    def step(t, carry):
        m_init, acc, m_prev = carry
        old_tiles = _visited_tiles(t - 1)
        whole = old_tiles // KEY_UNROLL
        half = (old_tiles - whole * KEY_UNROLL) // HALF_UNROLL
        extra = (_visited_tiles(t) - old_tiles) // HALF_UNROLL
        inner = (m_init, acc)
        inner = lax.fori_loop(0, whole, group_body(True, True, m_prev), inner)
        inner = lax.fori_loop(0, half, group_body(True, True, m_prev, HALF_UNROLL, whole * KEY_UNROLL), inner)
        inner = lax.fori_loop(0, extra, group_body(False, True, m_prev, HALF_UNROLL, old_tiles), inner)
        m_run, acc = inner
        m_fin = column_max(m_run)
        emit(t - 1, acc)
        p_own = own_probs(m_fin)
        m_next = prepare(jnp.minimum(t + 1, n_blocks - 1))
        acc_next = own_values(t, p_own)
        return m_next, acc_next, m_fin

    _, acc, m_prev = lax.fori_loop(1, n_blocks, step, (m_init1, acc0, m_fin0))

    last = n_blocks - 1
    last_tiles = _visited_tiles(last)
    whole = last_tiles // KEY_UNROLL
    unused_max = tuple(jnp.zeros((SUBLANES, blk), jnp.float32) for _ in heads)
    inner = lax.fori_loop(0, whole, group_body(True, False, m_prev), (unused_max, acc))
    if last_tiles % KEY_UNROLL:
        inner = group_body(True, False, m_prev, HALF_UNROLL, whole * KEY_UNROLL)(0, inner)
    emit(last, inner[1])


def _moba_attention(qt, k, vt, onehot, kmean, gate_attn):
    batch, seq, _ = k.shape
    n_blocks = seq // MOBA_BLOCK
    slots = _visited_tiles(n_blocks - 1)
    per_block = lambda r: pl.BlockSpec((None, n_blocks, None, r, MOBA_BLOCK), lambda b, hp: (b, 0, hp, 0, 0))
    per_row = pl.BlockSpec((None, seq, LANES), lambda b, hp: (b, 0, hp))
    return pl.pallas_call(
        _moba_kernel,
        grid=(batch, HEAD_TILES),
        in_specs=[
            per_block(LANES),
            per_row,
            per_block(VT_ROWS),
            pl.BlockSpec((seq, LANES), lambda b, hp: (0, 0), pipeline_mode=pl.Buffered(1)),
            pl.BlockSpec((n_blocks, 1, LANES), lambda b, hp: (b, 0, hp)),
            per_row,
        ],
        out_specs=per_row,
        out_shape=jax.ShapeDtypeStruct((batch, seq, ATTN_WIDTH), jnp.bfloat16),
        scratch_shapes=[
            pltpu.VMEM((HEADS_PER_TILE, 2 * LANES, MOBA_BLOCK), jnp.bfloat16),
            pltpu.VMEM((HEADS_PER_TILE, slots, MOBA_BLOCK, MOBA_BLOCK), jnp.float32),
            pltpu.VMEM((HEADS_PER_TILE, MOBA_BLOCK, MOBA_BLOCK), jnp.float32),
        ],
        compiler_params=pltpu.CompilerParams(
            dimension_semantics=("arbitrary", "arbitrary"), vmem_limit_bytes=VMEM_LIMIT),
        name="moba_attention",
    )(qt, k, vt, onehot, kmean, gate_attn)


def _out_proj_kernel(apply_final_norm, x_ref, attn_ref, u_ref, uprev_ref, gp_ref, wpool_ref, bpool_ref,
                     pscale_ref, wout_ref, fgain_ref, o_ref, ext_ref):
    i = pl.program_id(1)
    halo = uprev_ref[...]
    ext_ref[:POOL_HALO, :] = jnp.where(i > 0, halo, jnp.zeros_like(halo))
    ext_ref[POOL_HALO:, :] = u_ref[...]
    pos = i * ROW_TILE + lax.broadcasted_iota(jnp.int32, (ROW_TILE, 1), 0)

    pooled_parts = []
    for g, window in enumerate(POOL_WINDOWS):
        sl = slice(g * POOL_GROUP, (g + 1) * POOL_GROUP)
        u_g = ext_ref[POOL_HALO:, sl]
        total = u_g
        for t in range(1, window):
            total = total + ext_ref[POOL_HALO - t:POOL_HALO - t + ROW_TILE, sl]
        count = jnp.minimum(pos + 1, window).astype(jnp.float32)
        pooled = total / count - u_g
        y = jnp.dot(pooled.astype(jnp.bfloat16), wpool_ref[g], preferred_element_type=jnp.float32)
        pooled_parts.append(y + bpool_ref[g])
    pool = jnp.concatenate(pooled_parts, axis=1) * pscale_ref[...]
    pool = (pool * _silu(gp_ref[...])).astype(jnp.bfloat16)

    mixed = jnp.concatenate([attn_ref[...], pool], axis=1)
    y = x_ref[...] + jnp.dot(mixed, wout_ref[...], preferred_element_type=jnp.float32)
    if apply_final_norm:
        inv = lax.rsqrt(jnp.mean(y * y, axis=-1, keepdims=True) + EPS)
        y = y * inv * fgain_ref[...]
    o_ref[...] = y


def _out_proj(x, attn, u, gate_pool, w_pool, b_pool, pool_scale, w_out, final_gain, apply_final_norm):
    batch, seq, _ = x.shape
    tiles = seq // ROW_TILE
    halo_per_tile = ROW_TILE // POOL_HALO
    row_spec = lambda w: pl.BlockSpec((None, ROW_TILE, w), lambda b, i: (b, i, 0))
    const2 = lambda shape: pl.BlockSpec(shape, lambda b, i: (0, 0))
    const3 = lambda shape: pl.BlockSpec(shape, lambda b, i: (0, 0, 0))
    return pl.pallas_call(
        functools.partial(_out_proj_kernel, apply_final_norm),
        grid=(batch, tiles),
        in_specs=[
            row_spec(D_MODEL),
            row_spec(ATTN_WIDTH),
            row_spec(POOL_WIDTH),
            pl.BlockSpec((None, POOL_HALO, POOL_WIDTH),
                         lambda b, i: (b, jnp.maximum(i * halo_per_tile - 1, 0), 0)),
            row_spec(POOL_WIDTH),
            const3((len(POOL_WINDOWS), POOL_GROUP, POOL_GROUP)),
            const3((len(POOL_WINDOWS), 1, POOL_GROUP)),
            const2((1, POOL_WIDTH)),
            const2((D_MODEL, D_MODEL)),
            const2((1, D_MODEL)),
        ],
        out_specs=row_spec(D_MODEL),
        out_shape=jax.ShapeDtypeStruct((batch, seq, D_MODEL), jnp.float32),
        scratch_shapes=[pltpu.VMEM((POOL_HALO + ROW_TILE, POOL_WIDTH), jnp.float32)],
        compiler_params=pltpu.CompilerParams(
            dimension_semantics=("arbitrary", "arbitrary"), vmem_limit_bytes=VMEM_LIMIT),
        name="pool_out_proj",
    )(x, attn, u, u, gate_pool, w_pool, b_pool, pool_scale, w_out, final_gain)


def _position_tables(seq):
    pos = np.arange(seq, dtype=np.float64)
    inv_freq = 1.0 / (ROPE_THETA ** (np.arange(0, HEAD_DIM, 2, dtype=np.float64) / HEAD_DIM))
    ang = pos[:, None] * inv_freq[None, :]
    cos, sin = np.cos(ang), np.sin(ang)
    cos_t = np.tile(cos, (1, LANES // (HEAD_DIM // 2)))
    sin_t = np.tile(np.concatenate([-sin, sin], axis=1), (1, HEADS_PER_TILE))
    onehot = np.arange(seq)[:, None] // MOBA_BLOCK == np.arange(LANES)[None, :]
    return (jnp.asarray(cos_t, jnp.float32), jnp.asarray(sin_t, jnp.float32),
            jnp.asarray(onehot, jnp.bfloat16))


def kernel(x, norm_gain, w_in, w_pool, b_pool, pool_scale, w_out, final_gain):
    batch, seq, d_model = x.shape
    depth = w_in.shape[0]
    assert d_model == D_MODEL and seq % ROW_TILE == 0 and 2 <= seq // MOBA_BLOCK <= LANES
    n_blocks = seq // MOBA_BLOCK
    cos_t, sin_t, onehot = _position_tables(seq)

    for l in range(depth):
        qt, k, vt, gate_attn, u, gate_pool, kmean = _in_proj(
            x.reshape(batch * seq, d_model), norm_gain[l][None, :], w_in[l].astype(jnp.bfloat16),
            cos_t, sin_t, seq)
        shape3 = lambda t: t.reshape(batch, seq, t.shape[-1])
        per_block = lambda t: t.reshape(batch, n_blocks, HEAD_TILES, t.shape[-2], MOBA_BLOCK)
        attn = _moba_attention(per_block(qt), shape3(k), per_block(vt), onehot, kmean, shape3(gate_attn))
        x = _out_proj(x, attn, shape3(u), shape3(gate_pool), w_pool[l].astype(jnp.bfloat16),
                      b_pool[l][:, None, :], pool_scale[l][None, :], w_out[l].astype(jnp.bfloat16),
                      final_gain[None, :], apply_final_norm=(l == depth - 1))
    return x
```

```python
import functools

import numpy as np

import jax
import jax.numpy as jnp
from jax import lax
from jax.experimental import pallas as pl
from jax.experimental.pallas import tpu as pltpu

D_MODEL = 1024
ATTN_WIDTH = D_MODEL // 2
POOL_WIDTH = D_MODEL - ATTN_WIDTH
HEAD_DIM = 64
MOBA_BLOCK = 256
MOBA_TOPK = 3
POOL_WINDOWS = (2, 4, 8, 16)
POOL_GROUP = POOL_WIDTH // len(POOL_WINDOWS)
ROPE_THETA = 10000.0
EPS = 1e-6
IN_WIDTH = 4 * ATTN_WIDTH + 2 * POOL_WIDTH

LANES = 128
SUBLANES = 8
HEADS_PER_TILE = LANES // HEAD_DIM
HEAD_TILES = ATTN_WIDTH // LANES
ROW_TILE = 512
POOL_HALO = 16
MASK_BIAS = -1e30
KEY_UNROLL = 8
HALF_UNROLL = KEY_UNROLL // 2
WIDE_UNROLL = KEY_UNROLL * 2
BF16_ROWS = 16
VT_HEAD_ROWS = HEAD_DIM + BF16_ROWS
VT_ROWS = HEADS_PER_TILE * VT_HEAD_ROWS
Q_SCALE = HEAD_DIM ** -0.5 * 1.4426950408889634
VMEM_LIMIT = 56 * 1024 * 1024


def _silu(t):
    return t * (1.0 / (1.0 + jnp.exp(-t)))


def _rope(t, cos, sin_signed, first_half):
    swapped = jnp.where(first_half, pltpu.roll(t, LANES - HEAD_DIM // 2, axis=1),
                        pltpu.roll(t, HEAD_DIM // 2, axis=1))
    return t * cos + swapped * sin_signed


def _in_proj_kernel(x_ref, gain_ref, w_ref, cos_ref, sin_ref,
                    qt_ref, k_ref, vt_ref, ga_ref, u_ref, gp_ref, kmean_ref):
    x = x_ref[...]
    inv = lax.rsqrt(jnp.mean(x * x, axis=-1, keepdims=True) + EPS)
    h = (x * inv * gain_ref[...]).astype(jnp.bfloat16)

    def proj(c):
        return jnp.dot(h, w_ref[:, c * ATTN_WIDTH:(c + 1) * ATTN_WIDTH],
                       preferred_element_type=jnp.float32)

    cos = cos_ref[...]
    sin = sin_ref[...]
    lane = lax.broadcasted_iota(jnp.int32, (ROW_TILE, LANES), 1)
    first_half = (lane % HEAD_DIM) < (HEAD_DIM // 2)

    q = proj(0)
    k = proj(1)
    v = proj(2)
    for t in range(HEAD_TILES):
        sl = slice(t * LANES, (t + 1) * LANES)
        qr = _rope(q[:, sl], cos, sin, first_half) * Q_SCALE
        kr = _rope(k[:, sl], cos, sin, first_half)
        k_ref[:, sl] = kr.astype(k_ref.dtype)
        for b in range(ROW_TILE // MOBA_BLOCK):
            blk_rows = slice(b * MOBA_BLOCK, (b + 1) * MOBA_BLOCK)
            kmean_ref[b, :, sl] = jnp.sum(kr[blk_rows], axis=0, keepdims=True) * (1.0 / MOBA_BLOCK)
            qt_ref[b, t] = qr[blk_rows].T.astype(qt_ref.dtype)
            v_t = v[blk_rows, sl].T.astype(vt_ref.dtype)
            for head in range(HEADS_PER_TILE):
                base = head * VT_HEAD_ROWS
                vt_ref[b, t, base:base + HEAD_DIM] = v_t[head * HEAD_DIM:(head + 1) * HEAD_DIM]
                vt_ref[b, t, base + HEAD_DIM:base + VT_HEAD_ROWS] = jnp.ones((BF16_ROWS, MOBA_BLOCK),
                                                                             vt_ref.dtype)
    ga_ref[...] = proj(3)
    u_ref[...] = proj(4)
    gp_ref[...] = proj(5)


def _in_proj(x2, gain, w_bf16, cos_t, sin_t, seq):
    rows = x2.shape[0]
    n_tiles = rows // ROW_TILE
    seq_tiles = seq // ROW_TILE
    row_spec = lambda w: pl.BlockSpec((ROW_TILE, w), lambda i: (i, 0))
    tab_spec = pl.BlockSpec((ROW_TILE, LANES), lambda i: (i % seq_tiles, 0))
    blocks_per_tile = ROW_TILE // MOBA_BLOCK
    n_blocks = rows // MOBA_BLOCK
    transposed = lambda r: jax.ShapeDtypeStruct((n_blocks, HEAD_TILES, r, MOBA_BLOCK), jnp.bfloat16)
    transposed_spec = lambda r: pl.BlockSpec((blocks_per_tile, HEAD_TILES, r, MOBA_BLOCK),
                                             lambda i: (i, 0, 0, 0))
    out_shape = (
        transposed(LANES),
        jax.ShapeDtypeStruct((rows, ATTN_WIDTH), jnp.bfloat16),
        transposed(VT_ROWS),
        jax.ShapeDtypeStruct((rows, ATTN_WIDTH), jnp.float32),
        jax.ShapeDtypeStruct((rows, POOL_WIDTH), jnp.float32),
        jax.ShapeDtypeStruct((rows, POOL_WIDTH), jnp.float32),
        jax.ShapeDtypeStruct((n_blocks, 1, ATTN_WIDTH), jnp.float32),
    )
    return pl.pallas_call(
        _in_proj_kernel,
        grid=(n_tiles,),
        in_specs=[
            row_spec(D_MODEL),
            pl.BlockSpec((1, D_MODEL), lambda i: (0, 0)),
            pl.BlockSpec((D_MODEL, IN_WIDTH), lambda i: (0, 0)),
            tab_spec, tab_spec,
        ],
        out_specs=(
            transposed_spec(LANES), row_spec(ATTN_WIDTH), transposed_spec(VT_ROWS),
            row_spec(ATTN_WIDTH), row_spec(POOL_WIDTH), row_spec(POOL_WIDTH),
            pl.BlockSpec((blocks_per_tile, 1, ATTN_WIDTH), lambda i: (i, 0, 0)),
        ),
        out_shape=out_shape,
        compiler_params=pltpu.CompilerParams(
            dimension_semantics=("arbitrary",), vmem_limit_bytes=VMEM_LIMIT),
        name="in_proj",
    )(x2, gain, w_bf16, cos_t, sin_t)


def _select_bias_t(gate_t, n_past):
    slot = lax.broadcasted_iota(jnp.int32, gate_t.shape, 0)
    slot_f = slot.astype(jnp.float32)
    neg_inf = jnp.float32(-jnp.inf)
    g = jnp.where(slot < n_past, gate_t, neg_inf)
    bias = jnp.full(gate_t.shape, MASK_BIAS, dtype=jnp.float32)
    for _ in range(MOBA_TOPK):
        best = jnp.max(g, axis=0, keepdims=True)
        first = jnp.min(jnp.where(g == best, slot_f, float(gate_t.shape[0])), axis=0, keepdims=True)
        pick = (slot_f == first) & (best > neg_inf)
        bias = jnp.where(pick, 0.0, bias)
        g = jnp.where(pick, neg_inf, g)
    return bias


def _visited_tiles(n_past):
    rest = n_past % KEY_UNROLL
    whole = n_past - rest
    if isinstance(n_past, int):
        return whole + (0 if rest == 0 else HALF_UNROLL if rest <= HALF_UNROLL else KEY_UNROLL)
    return whole + jnp.where(rest == 0, 0, jnp.where(rest <= HALF_UNROLL, HALF_UNROLL, KEY_UNROLL))


def _fold_rows(t, op):
    return op(t.reshape(t.shape[0] // SUBLANES, SUBLANES, t.shape[1]), axis=0)


def _moba_kernel(qt_ref, k_ref, vt_ref, onehot_ref, kmean_ref, ga_ref, o_ref, qaug_ref, s_ref, own_ref):
    blk = MOBA_BLOCK
    n_blocks = kmean_ref.shape[0]
    heads = range(HEADS_PER_TILE)

    feat = lax.broadcasted_iota(jnp.int32, (LANES, blk), 0)
    km = kmean_ref[:, 0, :]
    km_hi = km.astype(jnp.bfloat16)
    km_lo = (km - km_hi.astype(jnp.float32)).astype(jnp.bfloat16)
    key_pos = lax.broadcasted_iota(jnp.int32, (blk, blk), 0)
    qry_pos = lax.broadcasted_iota(jnp.int32, (blk, blk), 1)
    causal = key_pos <= qry_pos

    def block_rows(j):
        return pl.ds(pl.multiple_of(j * blk, blk), blk)

    def prepare(j):
        qt = qt_ref[j].astype(jnp.float32)
        k_own = k_ref[block_rows(j), :]
        m_init = []
        for h in heads:
            qh = jnp.where((feat // HEAD_DIM) == h, qt, 0.0).astype(jnp.bfloat16)
            gate_t = (jnp.dot(km_hi, qh, preferred_element_type=jnp.float32)
                      + jnp.dot(km_lo, qh, preferred_element_type=jnp.float32))
            qaug_ref[h, :LANES, :] = qh
            qaug_ref[h, LANES:LANES + n_blocks, :] = _select_bias_t(gate_t, j).astype(jnp.bfloat16)
            qaug_ref[h, LANES + n_blocks:, :] = jnp.zeros((LANES - n_blocks, blk), jnp.bfloat16)
            s = jnp.dot(k_own, qh, preferred_element_type=jnp.float32)
            s = jnp.where(causal, s, -jnp.inf)
            own_ref[h] = s
            m_init.append(_fold_rows(s, jnp.max))
        return tuple(m_init)

    def probs(s, m_fin):
        return jnp.exp2(s - m_fin).astype(jnp.bfloat16)

    def weighted_values(p, n, h):
        v_aug = vt_ref[n, h * VT_HEAD_ROWS:(h + 1) * VT_HEAD_ROWS, :]
        r = jnp.dot(v_aug, p, preferred_element_type=jnp.float32)
        return r[:HEAD_DIM], r[HEAD_DIM:HEAD_DIM + SUBLANES]

    def own_probs(m_fin):
        return tuple(probs(own_ref[h], m_fin[h]) for h in heads)

    def own_values(j, p_own):
        return tuple(weighted_values(p_own[h], j, h) for h in heads)

    def column_max(m_run):
        return tuple(jnp.max(m, axis=0, keepdims=True) for m in m_run)

    def emit(j, acc):
        out_t = jnp.concatenate([o / l[:1] for o, l in acc], axis=0)
        rows = block_rows(j)
        o_ref[rows, :] = (out_t.T * _silu(ga_ref[rows, :])).astype(o_ref.dtype)

    def group_body(finish_old, score_new, m_prev, unroll=KEY_UNROLL, first=0):
        def body(g, carry):
            m_run, acc = [list(c) for c in carry]
            for i in range(unroll):
                n = first + g * unroll + i
                if score_new:
                    k_aug = jnp.concatenate([k_ref[block_rows(n), :], onehot_ref[block_rows(n), :]], axis=1)
                for h in heads:
                    if finish_old:
                        o, l = weighted_values(probs(s_ref[h, n], m_prev[h]), n, h)
                        acc[h] = (acc[h][0] + o, acc[h][1] + l)
                    if score_new:
                        s = jnp.dot(k_aug, qaug_ref[h], preferred_element_type=jnp.float32)
                        s_ref[h, n] = s
                        m_run[h] = jnp.maximum(m_run[h], _fold_rows(s, jnp.max))
            return tuple(m_run), tuple(acc)
        return body

    m_fin0 = column_max(prepare(0))
    acc0 = own_values(0, own_probs(m_fin0))
    m_init1 = prepare(1)

    def step(t, carry):
        m_init, acc, m_prev = carry
        old_tiles = _visited_tiles(t - 1)
        wide = old_tiles // WIDE_UNROLL
        done = wide * WIDE_UNROLL
        whole = (old_tiles - done) // KEY_UNROLL
        half = (old_tiles - done - whole * KEY_UNROLL) // HALF_UNROLL
        extra = (_visited_tiles(t) - old_tiles) // HALF_UNROLL
        inner = (m_init, acc)
        inner = lax.fori_loop(0, wide, group_body(True, True, m_prev, WIDE_UNROLL), inner)
        inner = lax.fori_loop(0, whole, group_body(True, True, m_prev, KEY_UNROLL, done), inner)
        inner = lax.fori_loop(0, half, group_body(True, True, m_prev, HALF_UNROLL,
                                                  done + whole * KEY_UNROLL), inner)
        inner = lax.fori_loop(0, extra, group_body(False, True, m_prev, HALF_UNROLL, old_tiles), inner)
        m_run, acc = inner
        m_fin = column_max(m_run)
        emit(t - 1, acc)
        p_own = own_probs(m_fin)
        m_next = prepare(jnp.minimum(t + 1, n_blocks - 1))
        acc_next = own_values(t, p_own)
        return m_next, acc_next, m_fin

    _, acc, m_prev = lax.fori_loop(1, n_blocks, step, (m_init1, acc0, m_fin0))

    last = n_blocks - 1
    last_tiles = _visited_tiles(last)
    whole = last_tiles // KEY_UNROLL
    unused_max = tuple(jnp.zeros((SUBLANES, blk), jnp.float32) for _ in heads)
    inner = lax.fori_loop(0, whole, group_body(True, False, m_prev), (unused_max, acc))
    if last_tiles % KEY_UNROLL:
        inner = group_body(True, False, m_prev, HALF_UNROLL, whole * KEY_UNROLL)(0, inner)
    emit(last, inner[1])


def _moba_attention(qt, k, vt, onehot, kmean, gate_attn):
    batch, seq, _ = k.shape
    n_blocks = seq // MOBA_BLOCK
    slots = _visited_tiles(n_blocks - 1)
    per_block = lambda r: pl.BlockSpec((None, n_blocks, None, r, MOBA_BLOCK), lambda b, hp: (b, 0, hp, 0, 0))
    per_row = pl.BlockSpec((None, seq, LANES), lambda b, hp: (b, 0, hp))
    return pl.pallas_call(
        _moba_kernel,
        grid=(batch, HEAD_TILES),
        in_specs=[
            per_block(LANES),
            per_row,
            per_block(VT_ROWS),
            pl.BlockSpec((seq, LANES), lambda b, hp: (0, 0), pipeline_mode=pl.Buffered(1)),
            pl.BlockSpec((n_blocks, 1, LANES), lambda b, hp: (b, 0, hp)),
            per_row,
        ],
        out_specs=per_row,
        out_shape=jax.ShapeDtypeStruct((batch, seq, ATTN_WIDTH), jnp.bfloat16),
        scratch_shapes=[
            pltpu.VMEM((HEADS_PER_TILE, 2 * LANES, MOBA_BLOCK), jnp.bfloat16),
            pltpu.VMEM((HEADS_PER_TILE, slots, MOBA_BLOCK, MOBA_BLOCK), jnp.float32),
            pltpu.VMEM((HEADS_PER_TILE, MOBA_BLOCK, MOBA_BLOCK), jnp.float32),
        ],
        compiler_params=pltpu.CompilerParams(
            dimension_semantics=("arbitrary", "arbitrary"), vmem_limit_bytes=VMEM_LIMIT),
        name="moba_attention",
    )(qt, k, vt, onehot, kmean, gate_attn)


def _out_proj_kernel(apply_final_norm, x_ref, attn_ref, u_ref, uprev_ref, gp_ref, wpool_ref, bpool_ref,
                     pscale_ref, wout_ref, fgain_ref, o_ref, ext_ref):
    i = pl.program_id(1)
    halo = uprev_ref[...]
    ext_ref[:POOL_HALO, :] = jnp.where(i > 0, halo, jnp.zeros_like(halo))
    ext_ref[POOL_HALO:, :] = u_ref[...]
    pos = i * ROW_TILE + lax.broadcasted_iota(jnp.int32, (ROW_TILE, 1), 0)

    pooled_parts = []
    for g, window in enumerate(POOL_WINDOWS):
        sl = slice(g * POOL_GROUP, (g + 1) * POOL_GROUP)
        u_g = ext_ref[POOL_HALO:, sl]
        total = u_g
        for t in range(1, window):
            total = total + ext_ref[POOL_HALO - t:POOL_HALO - t + ROW_TILE, sl]
        count = jnp.minimum(pos + 1, window).astype(jnp.float32)
        pooled = total / count - u_g
        y = jnp.dot(pooled.astype(jnp.bfloat16), wpool_ref[g], preferred_element_type=jnp.float32)
        pooled_parts.append(y + bpool_ref[g])
    pool = jnp.concatenate(pooled_parts, axis=1) * pscale_ref[...]
    pool = (pool * _silu(gp_ref[...])).astype(jnp.bfloat16)

    mixed = jnp.concatenate([attn_ref[...], pool], axis=1)
    y = x_ref[...] + jnp.dot(mixed, wout_ref[...], preferred_element_type=jnp.float32)
    if apply_final_norm:
        inv = lax.rsqrt(jnp.mean(y * y, axis=-1, keepdims=True) + EPS)
        y = y * inv * fgain_ref[...]
    o_ref[...] = y


def _out_proj(x, attn, u, gate_pool, w_pool, b_pool, pool_scale, w_out, final_gain, apply_final_norm):
    batch, seq, _ = x.shape
    tiles = seq // ROW_TILE
    halo_per_tile = ROW_TILE // POOL_HALO
    row_spec = lambda w: pl.BlockSpec((None, ROW_TILE, w), lambda b, i: (b, i, 0))
    const2 = lambda shape: pl.BlockSpec(shape, lambda b, i: (0, 0))
    const3 = lambda shape: pl.BlockSpec(shape, lambda b, i: (0, 0, 0))
    return pl.pallas_call(
        functools.partial(_out_proj_kernel, apply_final_norm),
        grid=(batch, tiles),
        in_specs=[
            row_spec(D_MODEL),
            row_spec(ATTN_WIDTH),
            row_spec(POOL_WIDTH),
            pl.BlockSpec((None, POOL_HALO, POOL_WIDTH),
                         lambda b, i: (b, jnp.maximum(i * halo_per_tile - 1, 0), 0)),
            row_spec(POOL_WIDTH),
            const3((len(POOL_WINDOWS), POOL_GROUP, POOL_GROUP)),
            const3((len(POOL_WINDOWS), 1, POOL_GROUP)),
            const2((1, POOL_WIDTH)),
            const2((D_MODEL, D_MODEL)),
            const2((1, D_MODEL)),
        ],
        out_specs=row_spec(D_MODEL),
        out_shape=jax.ShapeDtypeStruct((batch, seq, D_MODEL), jnp.float32),
        scratch_shapes=[pltpu.VMEM((POOL_HALO + ROW_TILE, POOL_WIDTH), jnp.float32)],
        compiler_params=pltpu.CompilerParams(
            dimension_semantics=("arbitrary", "arbitrary"), vmem_limit_bytes=VMEM_LIMIT),
        name="pool_out_proj",
    )(x, attn, u, u, gate_pool, w_pool, b_pool, pool_scale, w_out, final_gain)


def _position_tables(seq):
    pos = np.arange(seq, dtype=np.float64)
    inv_freq = 1.0 / (ROPE_THETA ** (np.arange(0, HEAD_DIM, 2, dtype=np.float64) / HEAD_DIM))
    ang = pos[:, None] * inv_freq[None, :]
    cos, sin = np.cos(ang), np.sin(ang)
    cos_t = np.tile(cos, (1, LANES // (HEAD_DIM // 2)))
    sin_t = np.tile(np.concatenate([-sin, sin], axis=1), (1, HEADS_PER_TILE))
    onehot = np.arange(seq)[:, None] // MOBA_BLOCK == np.arange(LANES)[None, :]
    return (jnp.asarray(cos_t, jnp.float32), jnp.asarray(sin_t, jnp.float32),
            jnp.asarray(onehot, jnp.bfloat16))


def kernel(x, norm_gain, w_in, w_pool, b_pool, pool_scale, w_out, final_gain):
    batch, seq, d_model = x.shape
    depth = w_in.shape[0]
    assert d_model == D_MODEL and seq % ROW_TILE == 0 and 2 <= seq // MOBA_BLOCK <= LANES
    n_blocks = seq // MOBA_BLOCK
    cos_t, sin_t, onehot = _position_tables(seq)

    for l in range(depth):
        qt, k, vt, gate_attn, u, gate_pool, kmean = _in_proj(
            x.reshape(batch * seq, d_model), norm_gain[l][None, :], w_in[l].astype(jnp.bfloat16),
            cos_t, sin_t, seq)
        shape3 = lambda t: t.reshape(batch, seq, t.shape[-1])
        per_block = lambda t: t.reshape(batch, n_blocks, HEAD_TILES, t.shape[-2], MOBA_BLOCK)
        attn = _moba_attention(per_block(qt), shape3(k), per_block(vt), onehot, kmean, shape3(gate_attn))
        x = _out_proj(x, attn, shape3(u), shape3(gate_pool), w_pool[l].astype(jnp.bfloat16),
                      b_pool[l][:, None, :], pool_scale[l][None, :], w_out[l].astype(jnp.bfloat16),
                      final_gain[None, :], apply_final_norm=(l == depth - 1))
    return x
```

```python
import functools

import numpy as np

import jax
import jax.numpy as jnp
from jax import lax
from jax.experimental import pallas as pl
from jax.experimental.pallas import tpu as pltpu

D_MODEL = 1024
ATTN_WIDTH = D_MODEL // 2
POOL_WIDTH = D_MODEL - ATTN_WIDTH
HEAD_DIM = 64
MOBA_BLOCK = 256
MOBA_TOPK = 3
POOL_WINDOWS = (2, 4, 8, 16)
POOL_GROUP = POOL_WIDTH // len(POOL_WINDOWS)
ROPE_THETA = 10000.0
EPS = 1e-6
IN_WIDTH = 4 * ATTN_WIDTH + 2 * POOL_WIDTH

LANES = 128
SUBLANES = 8
HEADS_PER_TILE = LANES // HEAD_DIM
HEAD_TILES = ATTN_WIDTH // LANES
ROW_TILE = 512
POOL_HALO = 16
MASK_BIAS = -1e30
KEY_UNROLL = 8
HALF_UNROLL = KEY_UNROLL // 2
WIDE_UNROLL = KEY_UNROLL * 2
BF16_ROWS = 16
VT_HEAD_ROWS = HEAD_DIM + BF16_ROWS
VT_ROWS = HEADS_PER_TILE * VT_HEAD_ROWS
Q_SCALE = HEAD_DIM ** -0.5 * 1.4426950408889634
VMEM_LIMIT = 56 * 1024 * 1024


def _silu(t):
    half = 0.5 * t
    return half + half * jnp.tanh(half)


def _rope(t, cos, sin_signed, first_half):
    swapped = jnp.where(first_half, pltpu.roll(t, LANES - HEAD_DIM // 2, axis=1),
                        pltpu.roll(t, HEAD_DIM // 2, axis=1))
    return t * cos + swapped * sin_signed


def _in_proj_kernel(x_ref, gain_ref, w_ref, cos_ref, sin_ref,
                    qt_ref, k_ref, vt_ref, ga_ref, u_ref, gp_ref, kmean_ref):
    x = x_ref[...]
    inv = lax.rsqrt(jnp.mean(x * x, axis=-1, keepdims=True) + EPS)
    h = (x * inv * gain_ref[...]).astype(jnp.bfloat16)

    def proj(c):
        return jnp.dot(h, w_ref[:, c * ATTN_WIDTH:(c + 1) * ATTN_WIDTH],
                       preferred_element_type=jnp.float32)

    cos = cos_ref[...]
    sin = sin_ref[...]
    lane = lax.broadcasted_iota(jnp.int32, (ROW_TILE, LANES), 1)
    first_half = (lane % HEAD_DIM) < (HEAD_DIM // 2)

    q = proj(0)
    k = proj(1)
    v = proj(2)
    for t in range(HEAD_TILES):
        sl = slice(t * LANES, (t + 1) * LANES)
        qr = _rope(q[:, sl], cos, sin, first_half) * Q_SCALE
        kr = _rope(k[:, sl], cos, sin, first_half)
        k_ref[:, sl] = kr.astype(k_ref.dtype)
        for b in range(ROW_TILE // MOBA_BLOCK):
            blk_rows = slice(b * MOBA_BLOCK, (b + 1) * MOBA_BLOCK)
            kmean_ref[b, :, sl] = jnp.sum(kr[blk_rows], axis=0, keepdims=True) * (1.0 / MOBA_BLOCK)
            qt_ref[b, t] = qr[blk_rows].T.astype(qt_ref.dtype)
            v_t = v[blk_rows, sl].T.astype(vt_ref.dtype)
            for head in range(HEADS_PER_TILE):
                base = head * VT_HEAD_ROWS
                vt_ref[b, t, base:base + HEAD_DIM] = v_t[head * HEAD_DIM:(head + 1) * HEAD_DIM]
                vt_ref[b, t, base + HEAD_DIM:base + VT_HEAD_ROWS] = jnp.ones((BF16_ROWS, MOBA_BLOCK),
                                                                             vt_ref.dtype)
    ga_ref[...] = proj(3)
    u_ref[...] = proj(4)
    gp_ref[...] = proj(5)


def _in_proj(x2, gain, w_bf16, cos_t, sin_t, seq):
    rows = x2.shape[0]
    n_tiles = rows // ROW_TILE
    seq_tiles = seq // ROW_TILE
    row_spec = lambda w: pl.BlockSpec((ROW_TILE, w), lambda i: (i, 0))
    tab_spec = pl.BlockSpec((ROW_TILE, LANES), lambda i: (i % seq_tiles, 0))
    blocks_per_tile = ROW_TILE // MOBA_BLOCK
    n_blocks = rows // MOBA_BLOCK
    transposed = lambda r: jax.ShapeDtypeStruct((n_blocks, HEAD_TILES, r, MOBA_BLOCK), jnp.bfloat16)
    transposed_spec = lambda r: pl.BlockSpec((blocks_per_tile, HEAD_TILES, r, MOBA_BLOCK),
                                             lambda i: (i, 0, 0, 0))
    out_shape = (
        transposed(LANES),
        jax.ShapeDtypeStruct((rows, ATTN_WIDTH), jnp.bfloat16),
        transposed(VT_ROWS),
        jax.ShapeDtypeStruct((rows, ATTN_WIDTH), jnp.float32),
        jax.ShapeDtypeStruct((rows, POOL_WIDTH), jnp.float32),
        jax.ShapeDtypeStruct((rows, POOL_WIDTH), jnp.float32),
        jax.ShapeDtypeStruct((n_blocks, 1, ATTN_WIDTH), jnp.float32),
    )
    return pl.pallas_call(
        _in_proj_kernel,
        grid=(n_tiles,),
        in_specs=[
            row_spec(D_MODEL),
            pl.BlockSpec((1, D_MODEL), lambda i: (0, 0)),
            pl.BlockSpec((D_MODEL, IN_WIDTH), lambda i: (0, 0)),
            tab_spec, tab_spec,
        ],
        out_specs=(
            transposed_spec(LANES), row_spec(ATTN_WIDTH), transposed_spec(VT_ROWS),
            row_spec(ATTN_WIDTH), row_spec(POOL_WIDTH), row_spec(POOL_WIDTH),
            pl.BlockSpec((blocks_per_tile, 1, ATTN_WIDTH), lambda i: (i, 0, 0)),
        ),
        out_shape=out_shape,
        compiler_params=pltpu.CompilerParams(
            dimension_semantics=("arbitrary",), vmem_limit_bytes=VMEM_LIMIT),
        name="in_proj",
    )(x2, gain, w_bf16, cos_t, sin_t)


def _select_bias_t(gate_t, n_past):
    slot = lax.broadcasted_iota(jnp.int32, gate_t.shape, 0)
    slot_f = slot.astype(jnp.float32)
    neg_inf = jnp.float32(-jnp.inf)
    g = jnp.where(slot < n_past, gate_t, neg_inf)
    bias = jnp.full(gate_t.shape, MASK_BIAS, dtype=jnp.float32)
    for _ in range(MOBA_TOPK):
        best = jnp.max(g, axis=0, keepdims=True)
        first = jnp.min(jnp.where(g == best, slot_f, float(gate_t.shape[0])), axis=0, keepdims=True)
        pick = (slot_f == first) & (best > neg_inf)
        bias = jnp.where(pick, 0.0, bias)
        g = jnp.where(pick, neg_inf, g)
    return bias


def _visited_tiles(n_past):
    rest = n_past % KEY_UNROLL
    whole = n_past - rest
    if isinstance(n_past, int):
        return whole + (0 if rest == 0 else HALF_UNROLL if rest <= HALF_UNROLL else KEY_UNROLL)
    return whole + jnp.where(rest == 0, 0, jnp.where(rest <= HALF_UNROLL, HALF_UNROLL, KEY_UNROLL))


def _fold_rows(t, op):
    return op(t.reshape(t.shape[0] // SUBLANES, SUBLANES, t.shape[1]), axis=0)


def _moba_kernel(qt_ref, k_ref, vt_ref, onehot_ref, kmean_ref, ga_ref, o_ref, qaug_ref, s_ref, own_ref):
    blk = MOBA_BLOCK
    n_blocks = kmean_ref.shape[0]
    heads = range(HEADS_PER_TILE)

    feat = lax.broadcasted_iota(jnp.int32, (LANES, blk), 0)
    km = kmean_ref[:, 0, :]
    km_hi = km.astype(jnp.bfloat16)
    km_lo = (km - km_hi.astype(jnp.float32)).astype(jnp.bfloat16)
    key_pos = lax.broadcasted_iota(jnp.int32, (blk, blk), 0)
    qry_pos = lax.broadcasted_iota(jnp.int32, (blk, blk), 1)
    causal = key_pos <= qry_pos

    def block_rows(j):
        return pl.ds(pl.multiple_of(j * blk, blk), blk)

    def prepare(j):
        qt = qt_ref[j].astype(jnp.float32)
        k_own = k_ref[block_rows(j), :]
        m_init = []
        for h in heads:
            qh = jnp.where((feat // HEAD_DIM) == h, qt, 0.0).astype(jnp.bfloat16)
            gate_t = (jnp.dot(km_hi, qh, preferred_element_type=jnp.float32)
                      + jnp.dot(km_lo, qh, preferred_element_type=jnp.float32))
            qaug_ref[h, :LANES, :] = qh
            qaug_ref[h, LANES:LANES + n_blocks, :] = _select_bias_t(gate_t, j).astype(jnp.bfloat16)
            qaug_ref[h, LANES + n_blocks:, :] = jnp.zeros((LANES - n_blocks, blk), jnp.bfloat16)
            s = jnp.dot(k_own, qh, preferred_element_type=jnp.float32)
            s = jnp.where(causal, s, -jnp.inf)
            own_ref[h] = s
            m_init.append(_fold_rows(s, jnp.max))
        return tuple(m_init)

    def probs(s, m_fin):
        return jnp.exp2(s - m_fin).astype(jnp.bfloat16)

    def weighted_values(p, n, h):
        v_aug = vt_ref[n, h * VT_HEAD_ROWS:(h + 1) * VT_HEAD_ROWS, :]
        r = jnp.dot(v_aug, p, preferred_element_type=jnp.float32)
        return r[:HEAD_DIM], r[HEAD_DIM:HEAD_DIM + SUBLANES]

    def own_probs(m_fin):
        return tuple(probs(own_ref[h], m_fin[h]) for h in heads)

    def own_values(j, p_own):
        return tuple(weighted_values(p_own[h], j, h) for h in heads)

    def column_max(m_run):
        return tuple(jnp.max(m, axis=0, keepdims=True) for m in m_run)

    def emit(j, acc):
        out_t = jnp.concatenate([o / l[:1] for o, l in acc], axis=0)
        rows = block_rows(j)
        o_ref[rows, :] = (out_t.T * _silu(ga_ref[rows, :])).astype(o_ref.dtype)

    def group_body(finish_old, score_new, m_prev, unroll=KEY_UNROLL, first=0):
        def body(g, carry):
            m_run, acc = [list(c) for c in carry]
            for i in range(unroll):
                n = first + g * unroll + i
                if score_new:
                    k_aug = jnp.concatenate([k_ref[block_rows(n), :], onehot_ref[block_rows(n), :]], axis=1)
                for h in heads:
                    if finish_old:
                        o, l = weighted_values(probs(s_ref[h, n], m_prev[h]), n, h)
                        acc[h] = (acc[h][0] + o, acc[h][1] + l)
                    if score_new:
                        s = jnp.dot(k_aug, qaug_ref[h], preferred_element_type=jnp.float32)
                        s_ref[h, n] = s
                        m_run[h] = jnp.maximum(m_run[h], _fold_rows(s, jnp.max))
            return tuple(m_run), tuple(acc)
        return body

    m_fin0 = column_max(prepare(0))
    acc0 = own_values(0, own_probs(m_fin0))
    m_init1 = prepare(1)

    def step(t, carry):
        m_init, acc, m_prev = carry
        old_tiles = _visited_tiles(t - 1)
        wide = old_tiles // WIDE_UNROLL
        done = wide * WIDE_UNROLL
        whole = (old_tiles - done) // KEY_UNROLL
        half = (old_tiles - done - whole * KEY_UNROLL) // HALF_UNROLL
        extra = (_visited_tiles(t) - old_tiles) // HALF_UNROLL
        inner = (m_init, acc)
        inner = lax.fori_loop(0, wide, group_body(True, True, m_prev, WIDE_UNROLL), inner)
        inner = lax.fori_loop(0, whole, group_body(True, True, m_prev, KEY_UNROLL, done), inner)
        inner = lax.fori_loop(0, half, group_body(True, True, m_prev, HALF_UNROLL,
                                                  done + whole * KEY_UNROLL), inner)
        inner = lax.fori_loop(0, extra, group_body(False, True, m_prev, HALF_UNROLL, old_tiles), inner)
        m_run, acc = inner
        m_fin = column_max(m_run)
        emit(t - 1, acc)
        p_own = own_probs(m_fin)
        m_next = prepare(jnp.minimum(t + 1, n_blocks - 1))
        acc_next = own_values(t, p_own)
        return m_next, acc_next, m_fin

    _, acc, m_prev = lax.fori_loop(1, n_blocks, step, (m_init1, acc0, m_fin0))

    last = n_blocks - 1
    last_tiles = _visited_tiles(last)
    whole = last_tiles // KEY_UNROLL
    unused_max = tuple(jnp.zeros((SUBLANES, blk), jnp.float32) for _ in heads)
    inner = lax.fori_loop(0, whole, group_body(True, False, m_prev), (unused_max, acc))
    if last_tiles % KEY_UNROLL:
        inner = group_body(True, False, m_prev, HALF_UNROLL, whole * KEY_UNROLL)(0, inner)
    emit(last, inner[1])


def _moba_attention(qt, k, vt, onehot, kmean, gate_attn):
    batch, seq, _ = k.shape
    n_blocks = seq // MOBA_BLOCK
    slots = _visited_tiles(n_blocks - 1)
    per_block = lambda r: pl.BlockSpec((None, n_blocks, None, r, MOBA_BLOCK), lambda b, hp: (b, 0, hp, 0, 0))
    per_row = pl.BlockSpec((None, seq, LANES), lambda b, hp: (b, 0, hp))
    return pl.pallas_call(
        _moba_kernel,
        grid=(batch, HEAD_TILES),
        in_specs=[
            per_block(LANES),
            per_row,
            per_block(VT_ROWS),
            pl.BlockSpec((seq, LANES), lambda b, hp: (0, 0), pipeline_mode=pl.Buffered(1)),
            pl.BlockSpec((n_blocks, 1, LANES), lambda b, hp: (b, 0, hp)),
            per_row,
        ],
        out_specs=per_row,
        out_shape=jax.ShapeDtypeStruct((batch, seq, ATTN_WIDTH), jnp.bfloat16),
        scratch_shapes=[
            pltpu.VMEM((HEADS_PER_TILE, 2 * LANES, MOBA_BLOCK), jnp.bfloat16),
            pltpu.VMEM((HEADS_PER_TILE, slots, MOBA_BLOCK, MOBA_BLOCK), jnp.float32),
            pltpu.VMEM((HEADS_PER_TILE, MOBA_BLOCK, MOBA_BLOCK), jnp.float32),
        ],
        compiler_params=pltpu.CompilerParams(
            dimension_semantics=("arbitrary", "arbitrary"), vmem_limit_bytes=VMEM_LIMIT),
        name="moba_attention",
    )(qt, k, vt, onehot, kmean, gate_attn)


def _out_proj_kernel(apply_final_norm, x_ref, attn_ref, u_ref, uprev_ref, gp_ref, wpool_ref, bpool_ref,
                     pscale_ref, wout_ref, fgain_ref, o_ref, ext_ref):
    i = pl.program_id(1)
    halo = uprev_ref[...]
    ext_ref[:POOL_HALO, :] = jnp.where(i > 0, halo, jnp.zeros_like(halo))
    ext_ref[POOL_HALO:, :] = u_ref[...]
    pos = i * ROW_TILE + lax.broadcasted_iota(jnp.int32, (ROW_TILE, 1), 0)

    pooled_parts = []
    for g, window in enumerate(POOL_WINDOWS):
        sl = slice(g * POOL_GROUP, (g + 1) * POOL_GROUP)
        total = ext_ref[:, sl]
        span = 1
        while span < window:
            total = total + pltpu.roll(total, span, axis=0)
            span *= 2
        total = total[POOL_HALO:]
        u_g = ext_ref[POOL_HALO:, sl]
        count = jnp.minimum(pos + 1, window).astype(jnp.float32)
        pooled = total / count - u_g
        y = jnp.dot(pooled.astype(jnp.bfloat16), wpool_ref[g], preferred_element_type=jnp.float32)
        pooled_parts.append(y + bpool_ref[g])
    pool = jnp.concatenate(pooled_parts, axis=1) * pscale_ref[...]
    pool = (pool * _silu(gp_ref[...])).astype(jnp.bfloat16)

    mixed = jnp.concatenate([attn_ref[...], pool], axis=1)
    y = x_ref[...] + jnp.dot(mixed, wout_ref[...], preferred_element_type=jnp.float32)
    if apply_final_norm:
        inv = lax.rsqrt(jnp.mean(y * y, axis=-1, keepdims=True) + EPS)
        y = y * inv * fgain_ref[...]
    o_ref[...] = y


def _out_proj(x, attn, u, gate_pool, w_pool, b_pool, pool_scale, w_out, final_gain, apply_final_norm):
    batch, seq, _ = x.shape
    tiles = seq // ROW_TILE
    halo_per_tile = ROW_TILE // POOL_HALO
    row_spec = lambda w: pl.BlockSpec((None, ROW_TILE, w), lambda b, i: (b, i, 0))
    const2 = lambda shape: pl.BlockSpec(shape, lambda b, i: (0, 0))
    const3 = lambda shape: pl.BlockSpec(shape, lambda b, i: (0, 0, 0))
    return pl.pallas_call(
        functools.partial(_out_proj_kernel, apply_final_norm),
        grid=(batch, tiles),
        in_specs=[
            row_spec(D_MODEL),
            row_spec(ATTN_WIDTH),
            row_spec(POOL_WIDTH),
            pl.BlockSpec((None, POOL_HALO, POOL_WIDTH),
                         lambda b, i: (b, jnp.maximum(i * halo_per_tile - 1, 0), 0)),
            row_spec(POOL_WIDTH),
            const3((len(POOL_WINDOWS), POOL_GROUP, POOL_GROUP)),
            const3((len(POOL_WINDOWS), 1, POOL_GROUP)),
            const2((1, POOL_WIDTH)),
            const2((D_MODEL, D_MODEL)),
            const2((1, D_MODEL)),
        ],
        out_specs=row_spec(D_MODEL),
        out_shape=jax.ShapeDtypeStruct((batch, seq, D_MODEL), jnp.float32),
        scratch_shapes=[pltpu.VMEM((POOL_HALO + ROW_TILE, POOL_WIDTH), jnp.float32)],
        compiler_params=pltpu.CompilerParams(
            dimension_semantics=("arbitrary", "arbitrary"), vmem_limit_bytes=VMEM_LIMIT),
        name="pool_out_proj",
    )(x, attn, u, u, gate_pool, w_pool, b_pool, pool_scale, w_out, final_gain)


def _position_tables(seq):
    pos = np.arange(seq, dtype=np.float64)
    inv_freq = 1.0 / (ROPE_THETA ** (np.arange(0, HEAD_DIM, 2, dtype=np.float64) / HEAD_DIM))
    ang = pos[:, None] * inv_freq[None, :]
    cos, sin = np.cos(ang), np.sin(ang)
    cos_t = np.tile(cos, (1, LANES // (HEAD_DIM // 2)))
    sin_t = np.tile(np.concatenate([-sin, sin], axis=1), (1, HEADS_PER_TILE))
    onehot = np.arange(seq)[:, None] // MOBA_BLOCK == np.arange(LANES)[None, :]
    return (jnp.asarray(cos_t, jnp.float32), jnp.asarray(sin_t, jnp.float32),
            jnp.asarray(onehot, jnp.bfloat16))


def kernel(x, norm_gain, w_in, w_pool, b_pool, pool_scale, w_out, final_gain):
    batch, seq, d_model = x.shape
    depth = w_in.shape[0]
    assert d_model == D_MODEL and seq % ROW_TILE == 0 and 2 <= seq // MOBA_BLOCK <= LANES
    n_blocks = seq // MOBA_BLOCK
    cos_t, sin_t, onehot = _position_tables(seq)

    for l in range(depth):
        qt, k, vt, gate_attn, u, gate_pool, kmean = _in_proj(
            x.reshape(batch * seq, d_model), norm_gain[l][None, :], w_in[l].astype(jnp.bfloat16),
            cos_t, sin_t, seq)
        shape3 = lambda t: t.reshape(batch, seq, t.shape[-1])
        per_block = lambda t: t.reshape(batch, n_blocks, HEAD_TILES, t.shape[-2], MOBA_BLOCK)
        attn = _moba_attention(per_block(qt), shape3(k), per_block(vt), onehot, kmean, shape3(gate_attn))
        x = _out_proj(x, attn, shape3(u), shape3(gate_pool), w_pool[l].astype(jnp.bfloat16),
                      b_pool[l][:, None, :], pool_scale[l][None, :], w_out[l].astype(jnp.bfloat16),
                      final_gain[None, :], apply_final_norm=(l == depth - 1))
    return x
```

```python
import functools

import numpy as np

import jax
import jax.numpy as jnp
from jax import lax
from jax.experimental import pallas as pl
from jax.experimental.pallas import tpu as pltpu

D_MODEL = 1024
ATTN_WIDTH = D_MODEL // 2
POOL_WIDTH = D_MODEL - ATTN_WIDTH
HEAD_DIM = 64
MOBA_BLOCK = 256
MOBA_TOPK = 3
POOL_WINDOWS = (2, 4, 8, 16)
POOL_GROUP = POOL_WIDTH // len(POOL_WINDOWS)
ROPE_THETA = 10000.0
EPS = 1e-6
IN_WIDTH = 4 * ATTN_WIDTH + 2 * POOL_WIDTH

LANES = 128
SUBLANES = 8
HEADS_PER_TILE = LANES // HEAD_DIM
HEAD_TILES = ATTN_WIDTH // LANES
ROW_TILE = 1024
POOL_HALO = 16
MASK_BIAS = -1e30
KEY_UNROLL = 8
HALF_UNROLL = KEY_UNROLL // 2
WIDE_UNROLL = KEY_UNROLL * 2
BF16_ROWS = 16
VT_HEAD_ROWS = HEAD_DIM + BF16_ROWS
VT_ROWS = HEADS_PER_TILE * VT_HEAD_ROWS
Q_SCALE = HEAD_DIM ** -0.5 * 1.4426950408889634
VMEM_LIMIT = 56 * 1024 * 1024


def _silu(t):
    half = 0.5 * t
    return half + half * jnp.tanh(half)


def _rope(t, cos, sin_signed, first_half):
    swapped = jnp.where(first_half, pltpu.roll(t, LANES - HEAD_DIM // 2, axis=1),
                        pltpu.roll(t, HEAD_DIM // 2, axis=1))
    return t * cos + swapped * sin_signed


def _in_proj_kernel(x_ref, gain_ref, w_ref, cos_ref, sin_ref,
                    qt_ref, k_ref, vt_ref, ga_ref, u_ref, gp_ref, kmean_ref, wbf_ref):
    @pl.when(pl.program_id(0) == 0)
    def _():
        wbf_ref[...] = w_ref[...].astype(wbf_ref.dtype)

    x = x_ref[...]
    inv = lax.rsqrt(jnp.mean(x * x, axis=-1, keepdims=True) + EPS)
    h = (x * inv * gain_ref[...]).astype(jnp.bfloat16)

    def proj(c):
        return jnp.dot(h, wbf_ref[:, c * ATTN_WIDTH:(c + 1) * ATTN_WIDTH],
                       preferred_element_type=jnp.float32)

    cos = cos_ref[...]
    sin = sin_ref[...]
    lane = lax.broadcasted_iota(jnp.int32, (ROW_TILE, LANES), 1)
    first_half = (lane % HEAD_DIM) < (HEAD_DIM // 2)

    q = proj(0)
    k = proj(1)
    v = proj(2)
    for t in range(HEAD_TILES):
        sl = slice(t * LANES, (t + 1) * LANES)
        qr = _rope(q[:, sl], cos, sin, first_half) * Q_SCALE
        kr = _rope(k[:, sl], cos, sin, first_half)
        k_ref[:, sl] = kr.astype(k_ref.dtype)
        for b in range(ROW_TILE // MOBA_BLOCK):
            blk_rows = slice(b * MOBA_BLOCK, (b + 1) * MOBA_BLOCK)
            kmean_ref[b, :, sl] = jnp.sum(kr[blk_rows], axis=0, keepdims=True) * (1.0 / MOBA_BLOCK)
            qt_ref[b, t] = qr[blk_rows].T.astype(qt_ref.dtype)
            v_t = v[blk_rows, sl].T.astype(vt_ref.dtype)
            for head in range(HEADS_PER_TILE):
                base = head * VT_HEAD_ROWS
                vt_ref[b, t, base:base + HEAD_DIM] = v_t[head * HEAD_DIM:(head + 1) * HEAD_DIM]
                vt_ref[b, t, base + HEAD_DIM:base + VT_HEAD_ROWS] = jnp.ones((BF16_ROWS, MOBA_BLOCK),
                                                                             vt_ref.dtype)
    ga_ref[...] = proj(3)
    u_ref[...] = proj(4)
    gp_ref[...] = proj(5)


def _in_proj(x2, gain, w_in, cos_t, sin_t, seq):
    rows = x2.shape[0]
    n_tiles = rows // ROW_TILE
    seq_tiles = seq // ROW_TILE
    row_spec = lambda w: pl.BlockSpec((ROW_TILE, w), lambda i: (i, 0))
    tab_spec = pl.BlockSpec((ROW_TILE, LANES), lambda i: (i % seq_tiles, 0))
    blocks_per_tile = ROW_TILE // MOBA_BLOCK
    n_blocks = rows // MOBA_BLOCK
    transposed = lambda r: jax.ShapeDtypeStruct((n_blocks, HEAD_TILES, r, MOBA_BLOCK), jnp.bfloat16)
    transposed_spec = lambda r: pl.BlockSpec((blocks_per_tile, HEAD_TILES, r, MOBA_BLOCK),
                                             lambda i: (i, 0, 0, 0))
    out_shape = (
        transposed(LANES),
        jax.ShapeDtypeStruct((rows, ATTN_WIDTH), jnp.bfloat16),
        transposed(VT_ROWS),
        jax.ShapeDtypeStruct((rows, ATTN_WIDTH), jnp.float32),
        jax.ShapeDtypeStruct((rows, POOL_WIDTH), jnp.float32),
        jax.ShapeDtypeStruct((rows, POOL_WIDTH), jnp.float32),
        jax.ShapeDtypeStruct((n_blocks, 1, ATTN_WIDTH), jnp.float32),
    )
    return pl.pallas_call(
        _in_proj_kernel,
        grid=(n_tiles,),
        in_specs=[
            row_spec(D_MODEL),
            pl.BlockSpec((1, D_MODEL), lambda i: (0, 0)),
            pl.BlockSpec((D_MODEL, IN_WIDTH), lambda i: (0, 0), pipeline_mode=pl.Buffered(1)),
            tab_spec, tab_spec,
        ],
        out_specs=(
            transposed_spec(LANES), row_spec(ATTN_WIDTH), transposed_spec(VT_ROWS),
            row_spec(ATTN_WIDTH), row_spec(POOL_WIDTH), row_spec(POOL_WIDTH),
            pl.BlockSpec((blocks_per_tile, 1, ATTN_WIDTH), lambda i: (i, 0, 0)),
        ),
        out_shape=out_shape,
        scratch_shapes=[pltpu.VMEM((D_MODEL, IN_WIDTH), jnp.bfloat16)],
        compiler_params=pltpu.CompilerParams(
            dimension_semantics=("arbitrary",), vmem_limit_bytes=VMEM_LIMIT),
        name="in_proj",
    )(x2, gain, w_in, cos_t, sin_t)


def _select_bias_t(gate_t, n_past):
    slot = lax.broadcasted_iota(jnp.int32, gate_t.shape, 0)
    slot_f = slot.astype(jnp.float32)
    neg_inf = jnp.float32(-jnp.inf)
    g = jnp.where(slot < n_past, gate_t, neg_inf)
    bias = jnp.full(gate_t.shape, MASK_BIAS, dtype=jnp.float32)
    for _ in range(MOBA_TOPK):
        best = jnp.max(g, axis=0, keepdims=True)
        first = jnp.min(jnp.where(g == best, slot_f, float(gate_t.shape[0])), axis=0, keepdims=True)
        pick = (slot_f == first) & (best > neg_inf)
        bias = jnp.where(pick, 0.0, bias)
        g = jnp.where(pick, neg_inf, g)
    return bias


def _visited_tiles(n_past):
    rest = n_past % KEY_UNROLL
    whole = n_past - rest
    if isinstance(n_past, int):
        return whole + (0 if rest == 0 else HALF_UNROLL if rest <= HALF_UNROLL else KEY_UNROLL)
    return whole + jnp.where(rest == 0, 0, jnp.where(rest <= HALF_UNROLL, HALF_UNROLL, KEY_UNROLL))


def _fold_rows(t, op):
    return op(t.reshape(t.shape[0] // SUBLANES, SUBLANES, t.shape[1]), axis=0)


def _moba_kernel(qt_ref, k_ref, vt_ref, onehot_ref, kmean_ref, ga_ref, o_ref, qaug_ref, s_ref, own_ref):
    blk = MOBA_BLOCK
    n_blocks = kmean_ref.shape[0]
    heads = range(HEADS_PER_TILE)

    feat = lax.broadcasted_iota(jnp.int32, (LANES, blk), 0)
    km = kmean_ref[:, 0, :]
    km_hi = km.astype(jnp.bfloat16)
    km_lo = (km - km_hi.astype(jnp.float32)).astype(jnp.bfloat16)
    key_pos = lax.broadcasted_iota(jnp.int32, (blk, blk), 0)
    qry_pos = lax.broadcasted_iota(jnp.int32, (blk, blk), 1)
    causal = key_pos <= qry_pos

    def block_rows(j):
        return pl.ds(pl.multiple_of(j * blk, blk), blk)

    def prepare(j):
        qt = qt_ref[j].astype(jnp.float32)
        k_own = k_ref[block_rows(j), :]
        m_init = []
        for h in heads:
            qh = jnp.where((feat // HEAD_DIM) == h, qt, 0.0).astype(jnp.bfloat16)
            gate_t = (jnp.dot(km_hi, qh, preferred_element_type=jnp.float32)
                      + jnp.dot(km_lo, qh, preferred_element_type=jnp.float32))
            qaug_ref[h, :LANES, :] = qh
            qaug_ref[h, LANES:LANES + n_blocks, :] = _select_bias_t(gate_t, j).astype(jnp.bfloat16)
            qaug_ref[h, LANES + n_blocks:, :] = jnp.zeros((LANES - n_blocks, blk), jnp.bfloat16)
            s = jnp.dot(k_own, qh, preferred_element_type=jnp.float32)
            s = jnp.where(causal, s, -jnp.inf)
            own_ref[h] = s
            m_init.append(_fold_rows(s, jnp.max))
        return tuple(m_init)

    def probs(s, m_fin):
        return jnp.exp2(s - m_fin).astype(jnp.bfloat16)

    def weighted_values(p, n, h):
        v_aug = vt_ref[n, h * VT_HEAD_ROWS:(h + 1) * VT_HEAD_ROWS, :]
        r = jnp.dot(v_aug, p, preferred_element_type=jnp.float32)
        return r[:HEAD_DIM], r[HEAD_DIM:HEAD_DIM + SUBLANES]

    def own_probs(m_fin):
        return tuple(probs(own_ref[h], m_fin[h]) for h in heads)

    def own_values(j, p_own):
        return tuple(weighted_values(p_own[h], j, h) for h in heads)

    def column_max(m_run):
        return tuple(jnp.max(m, axis=0, keepdims=True) for m in m_run)

    def emit(j, acc):
        out_t = jnp.concatenate([o / l[:1] for o, l in acc], axis=0)
        rows = block_rows(j)
        o_ref[rows, :] = (out_t.T * _silu(ga_ref[rows, :])).astype(o_ref.dtype)

    def group_body(finish_old, score_new, m_prev, unroll=KEY_UNROLL, first=0):
        def body(g, carry):
            m_run, acc = [list(c) for c in carry]
            for i in range(unroll):
                n = first + g * unroll + i
                if score_new:
                    k_aug = jnp.concatenate([k_ref[block_rows(n), :], onehot_ref[block_rows(n), :]], axis=1)
                for h in heads:
                    if finish_old:
                        o, l = weighted_values(probs(s_ref[h, n], m_prev[h]), n, h)
                        acc[h] = (acc[h][0] + o, acc[h][1] + l)
                    if score_new:
                        s = jnp.dot(k_aug, qaug_ref[h], preferred_element_type=jnp.float32)
                        s_ref[h, n] = s
                        m_run[h] = jnp.maximum(m_run[h], _fold_rows(s, jnp.max))
            return tuple(m_run), tuple(acc)
        return body

    m_fin0 = column_max(prepare(0))
    acc0 = own_values(0, own_probs(m_fin0))
    m_init1 = prepare(1)

    def step(t, carry):
        m_init, acc, m_prev = carry
        old_tiles = _visited_tiles(t - 1)
        wide = old_tiles // WIDE_UNROLL
        done = wide * WIDE_UNROLL
        whole = (old_tiles - done) // KEY_UNROLL
        half = (old_tiles - done - whole * KEY_UNROLL) // HALF_UNROLL
        extra = (_visited_tiles(t) - old_tiles) // HALF_UNROLL
        inner = (m_init, acc)
        inner = lax.fori_loop(0, wide, group_body(True, True, m_prev, WIDE_UNROLL), inner)
        inner = lax.fori_loop(0, whole, group_body(True, True, m_prev, KEY_UNROLL, done), inner)
        inner = lax.fori_loop(0, half, group_body(True, True, m_prev, HALF_UNROLL,
                                                  done + whole * KEY_UNROLL), inner)
        inner = lax.fori_loop(0, extra, group_body(False, True, m_prev, HALF_UNROLL, old_tiles), inner)
        m_run, acc = inner
        m_fin = column_max(m_run)
        emit(t - 1, acc)
        p_own = own_probs(m_fin)
        m_next = prepare(jnp.minimum(t + 1, n_blocks - 1))
        acc_next = own_values(t, p_own)
        return m_next, acc_next, m_fin

    _, acc, m_prev = lax.fori_loop(1, n_blocks, step, (m_init1, acc0, m_fin0))

    last = n_blocks - 1
    last_tiles = _visited_tiles(last)
    whole = last_tiles // KEY_UNROLL
    unused_max = tuple(jnp.zeros((SUBLANES, blk), jnp.float32) for _ in heads)
    inner = lax.fori_loop(0, whole, group_body(True, False, m_prev), (unused_max, acc))
    if last_tiles % KEY_UNROLL:
        inner = group_body(True, False, m_prev, HALF_UNROLL, whole * KEY_UNROLL)(0, inner)
    emit(last, inner[1])


def _moba_attention(qt, k, vt, onehot, kmean, gate_attn):
    batch, seq, _ = k.shape
    n_blocks = seq // MOBA_BLOCK
    slots = _visited_tiles(n_blocks - 1)
    per_block = lambda r: pl.BlockSpec((None, n_blocks, None, r, MOBA_BLOCK), lambda b, hp: (b, 0, hp, 0, 0))
    per_row = pl.BlockSpec((None, seq, LANES), lambda b, hp: (b, 0, hp))
    return pl.pallas_call(
        _moba_kernel,
        grid=(batch, HEAD_TILES),
        in_specs=[
            per_block(LANES),
            per_row,
            per_block(VT_ROWS),
            pl.BlockSpec((seq, LANES), lambda b, hp: (0, 0), pipeline_mode=pl.Buffered(1)),
            pl.BlockSpec((n_blocks, 1, LANES), lambda b, hp: (b, 0, hp)),
            per_row,
        ],
        out_specs=per_row,
        out_shape=jax.ShapeDtypeStruct((batch, seq, ATTN_WIDTH), jnp.bfloat16),
        scratch_shapes=[
            pltpu.VMEM((HEADS_PER_TILE, 2 * LANES, MOBA_BLOCK), jnp.bfloat16),
            pltpu.VMEM((HEADS_PER_TILE, slots, MOBA_BLOCK, MOBA_BLOCK), jnp.float32),
            pltpu.VMEM((HEADS_PER_TILE, MOBA_BLOCK, MOBA_BLOCK), jnp.float32),
        ],
        compiler_params=pltpu.CompilerParams(
            dimension_semantics=("arbitrary", "arbitrary"), vmem_limit_bytes=VMEM_LIMIT),
        name="moba_attention",
    )(qt, k, vt, onehot, kmean, gate_attn)


def _out_proj_kernel(apply_final_norm, x_ref, attn_ref, u_ref, uprev_ref, gp_ref, wpool_ref, bpool_ref,
                     pscale_ref, wout_ref, fgain_ref, o_ref, ext_ref, wbf_ref):
    i = pl.program_id(1)

    @pl.when((pl.program_id(0) == 0) & (i == 0))
    def _():
        wbf_ref[...] = wout_ref[...].astype(wbf_ref.dtype)

    halo = uprev_ref[...]
    ext_ref[:POOL_HALO, :] = jnp.where(i > 0, halo, jnp.zeros_like(halo))
    ext_ref[POOL_HALO:, :] = u_ref[...]
    pos = i * ROW_TILE + lax.broadcasted_iota(jnp.int32, (ROW_TILE, 1), 0)

    pooled_parts = []
    for g, window in enumerate(POOL_WINDOWS):
        sl = slice(g * POOL_GROUP, (g + 1) * POOL_GROUP)
        total = ext_ref[:, sl]
        span = 1
        while span < window:
            total = total + pltpu.roll(total, span, axis=0)
            span *= 2
        total = total[POOL_HALO:]
        u_g = ext_ref[POOL_HALO:, sl]
        count = jnp.minimum(pos + 1, window).astype(jnp.float32)
        pooled = total / count - u_g
        y = jnp.dot(pooled.astype(jnp.bfloat16), wpool_ref[g].astype(jnp.bfloat16),
                    preferred_element_type=jnp.float32)
        pooled_parts.append(y + bpool_ref[g])
    pool = jnp.concatenate(pooled_parts, axis=1) * pscale_ref[...]
    pool = (pool * _silu(gp_ref[...])).astype(jnp.bfloat16)

    mixed = jnp.concatenate([attn_ref[...], pool], axis=1)
    y = x_ref[...] + jnp.dot(mixed, wbf_ref[...], preferred_element_type=jnp.float32)
    if apply_final_norm:
        inv = lax.rsqrt(jnp.mean(y * y, axis=-1, keepdims=True) + EPS)
        y = y * inv * fgain_ref[...]
    o_ref[...] = y


def _out_proj(x, attn, u, gate_pool, w_pool, b_pool, pool_scale, w_out, final_gain, apply_final_norm):
    batch, seq, _ = x.shape
    tiles = seq // ROW_TILE
    halo_per_tile = ROW_TILE // POOL_HALO
    row_spec = lambda w: pl.BlockSpec((None, ROW_TILE, w), lambda b, i: (b, i, 0))
    const2 = lambda shape: pl.BlockSpec(shape, lambda b, i: (0, 0))
    const3 = lambda shape: pl.BlockSpec(shape, lambda b, i: (0, 0, 0))
    return pl.pallas_call(
        functools.partial(_out_proj_kernel, apply_final_norm),
        grid=(batch, tiles),
        in_specs=[
            row_spec(D_MODEL),
            row_spec(ATTN_WIDTH),
            row_spec(POOL_WIDTH),
            pl.BlockSpec((None, POOL_HALO, POOL_WIDTH),
                         lambda b, i: (b, jnp.maximum(i * halo_per_tile - 1, 0), 0)),
            row_spec(POOL_WIDTH),
            const3((len(POOL_WINDOWS), POOL_GROUP, POOL_GROUP)),
            const3((len(POOL_WINDOWS), 1, POOL_GROUP)),
            const2((1, POOL_WIDTH)),
            pl.BlockSpec((D_MODEL, D_MODEL), lambda b, i: (0, 0), pipeline_mode=pl.Buffered(1)),
            const2((1, D_MODEL)),
        ],
        out_specs=row_spec(D_MODEL),
        out_shape=jax.ShapeDtypeStruct((batch, seq, D_MODEL), jnp.float32),
        scratch_shapes=[pltpu.VMEM((POOL_HALO + ROW_TILE, POOL_WIDTH), jnp.float32),
                        pltpu.VMEM((D_MODEL, D_MODEL), jnp.bfloat16)],
        compiler_params=pltpu.CompilerParams(
            dimension_semantics=("arbitrary", "arbitrary"), vmem_limit_bytes=VMEM_LIMIT),
        name="pool_out_proj",
    )(x, attn, u, u, gate_pool, w_pool, b_pool, pool_scale, w_out, final_gain)


def _position_tables(seq):
    pos = np.arange(seq, dtype=np.float64)
    inv_freq = 1.0 / (ROPE_THETA ** (np.arange(0, HEAD_DIM, 2, dtype=np.float64) / HEAD_DIM))
    ang = pos[:, None] * inv_freq[None, :]
    cos, sin = np.cos(ang), np.sin(ang)
    cos_t = np.tile(cos, (1, LANES // (HEAD_DIM // 2)))
    sin_t = np.tile(np.concatenate([-sin, sin], axis=1), (1, HEADS_PER_TILE))
    onehot = np.arange(seq)[:, None] // MOBA_BLOCK == np.arange(LANES)[None, :]
    return (jnp.asarray(cos_t, jnp.float32), jnp.asarray(sin_t, jnp.float32),
            jnp.asarray(onehot, jnp.bfloat16))


def kernel(x, norm_gain, w_in, w_pool, b_pool, pool_scale, w_out, final_gain):
    batch, seq, d_model = x.shape
    depth = w_in.shape[0]
    assert d_model == D_MODEL and seq % ROW_TILE == 0 and 2 <= seq // MOBA_BLOCK <= LANES
    n_blocks = seq // MOBA_BLOCK
    cos_t, sin_t, onehot = _position_tables(seq)

    for l in range(depth):
        qt, k, vt, gate_attn, u, gate_pool, kmean = _in_proj(
            x.reshape(batch * seq, d_model), norm_gain[l][None, :], w_in[l], cos_t, sin_t, seq)
        shape3 = lambda t: t.reshape(batch, seq, t.shape[-1])
        per_block = lambda t: t.reshape(batch, n_blocks, HEAD_TILES, t.shape[-2], MOBA_BLOCK)
        attn = _moba_attention(per_block(qt), shape3(k), per_block(vt), onehot, kmean, shape3(gate_attn))
        x = _out_proj(x, attn, shape3(u), shape3(gate_pool), w_pool[l], b_pool[l][:, None, :],
                      pool_scale[l][None, :], w_out[l], final_gain[None, :],
                      apply_final_norm=(l == depth - 1))
    return x
```

```python
import functools

import numpy as np

import jax
import jax.numpy as jnp
from jax import lax
from jax.experimental import pallas as pl
from jax.experimental.pallas import tpu as pltpu

D_MODEL = 1024
ATTN_WIDTH = D_MODEL // 2
POOL_WIDTH = D_MODEL - ATTN_WIDTH
HEAD_DIM = 64
MOBA_BLOCK = 256
MOBA_TOPK = 3
POOL_WINDOWS = (2, 4, 8, 16)
POOL_GROUP = POOL_WIDTH // len(POOL_WINDOWS)
ROPE_THETA = 10000.0
EPS = 1e-6
IN_WIDTH = 4 * ATTN_WIDTH + 2 * POOL_WIDTH

LANES = 128
SUBLANES = 8
HEADS_PER_TILE = LANES // HEAD_DIM
HEAD_TILES = ATTN_WIDTH // LANES
ROW_TILE = 1024
ROW_CHUNK = 128
POOL_HALO = 16
MASK_BIAS = -1e30
KEY_UNROLL = 8
HALF_UNROLL = KEY_UNROLL // 2
WIDE_UNROLL = KEY_UNROLL * 2
BF16_ROWS = 16
VT_HEAD_ROWS = HEAD_DIM + BF16_ROWS
VT_ROWS = HEADS_PER_TILE * VT_HEAD_ROWS
Q_SCALE = HEAD_DIM ** -0.5 * 1.4426950408889634
VMEM_LIMIT = 56 * 1024 * 1024


def _silu(t):
    half = 0.5 * t
    return half + half * jnp.tanh(half)


def _rope(t, cos, sin_signed, first_half):
    swapped = jnp.where(first_half, pltpu.roll(t, LANES - HEAD_DIM // 2, axis=1),
                        pltpu.roll(t, HEAD_DIM // 2, axis=1))
    return t * cos + swapped * sin_signed


def _in_proj_kernel(x_ref, gain_ref, w_ref, cos_ref, sin_ref,
                    qt_ref, k_ref, vt_ref, ga_ref, u_ref, gp_ref, kmean_ref, wbf_ref):
    @pl.when(pl.program_id(0) == 0)
    def _():
        wbf_ref[...] = w_ref[...].astype(wbf_ref.dtype)

    x = x_ref[...]
    inv = lax.rsqrt(jnp.mean(x * x, axis=-1, keepdims=True) + EPS)
    h = (x * inv * gain_ref[...]).astype(jnp.bfloat16)

    def proj(c):
        return jnp.dot(h, wbf_ref[:, c * ATTN_WIDTH:(c + 1) * ATTN_WIDTH],
                       preferred_element_type=jnp.float32)

    cos = cos_ref[...]
    sin = sin_ref[...]
    lane = lax.broadcasted_iota(jnp.int32, (ROW_TILE, LANES), 1)
    first_half = (lane % HEAD_DIM) < (HEAD_DIM // 2)

    q = proj(0)
    k = proj(1)
    v = proj(2)
    for t in range(HEAD_TILES):
        sl = slice(t * LANES, (t + 1) * LANES)
        qr = _rope(q[:, sl], cos, sin, first_half) * Q_SCALE
        kr = _rope(k[:, sl], cos, sin, first_half)
        k_ref[:, sl] = kr.astype(k_ref.dtype)
        for b in range(ROW_TILE // MOBA_BLOCK):
            blk_rows = slice(b * MOBA_BLOCK, (b + 1) * MOBA_BLOCK)
            kmean_ref[b, :, sl] = jnp.sum(kr[blk_rows], axis=0, keepdims=True) * (1.0 / MOBA_BLOCK)
            qt_ref[b, t] = qr[blk_rows].T.astype(qt_ref.dtype)
            v_t = v[blk_rows, sl].T.astype(vt_ref.dtype)
            for head in range(HEADS_PER_TILE):
                base = head * VT_HEAD_ROWS
                vt_ref[b, t, base:base + HEAD_DIM] = v_t[head * HEAD_DIM:(head + 1) * HEAD_DIM]
                vt_ref[b, t, base + HEAD_DIM:base + VT_HEAD_ROWS] = jnp.ones((BF16_ROWS, MOBA_BLOCK),
                                                                             vt_ref.dtype)
    ga_ref[...] = proj(3)
    u_ref[...] = proj(4)
    gp_ref[...] = proj(5)


def _in_proj(x2, gain, w_in, cos_t, sin_t, seq):
    rows = x2.shape[0]
    n_tiles = rows // ROW_TILE
    seq_tiles = seq // ROW_TILE
    row_spec = lambda w: pl.BlockSpec((ROW_TILE, w), lambda i: (i, 0))
    tab_spec = pl.BlockSpec((ROW_TILE, LANES), lambda i: (i % seq_tiles, 0))
    blocks_per_tile = ROW_TILE // MOBA_BLOCK
    n_blocks = rows // MOBA_BLOCK
    transposed = lambda r: jax.ShapeDtypeStruct((n_blocks, HEAD_TILES, r, MOBA_BLOCK), jnp.bfloat16)
    transposed_spec = lambda r: pl.BlockSpec((blocks_per_tile, HEAD_TILES, r, MOBA_BLOCK),
                                             lambda i: (i, 0, 0, 0))
    out_shape = (
        transposed(LANES),
        jax.ShapeDtypeStruct((rows, ATTN_WIDTH), jnp.bfloat16),
        transposed(VT_ROWS),
        jax.ShapeDtypeStruct((rows, ATTN_WIDTH), jnp.float32),
        jax.ShapeDtypeStruct((rows, POOL_WIDTH), jnp.float32),
        jax.ShapeDtypeStruct((rows, POOL_WIDTH), jnp.float32),
        jax.ShapeDtypeStruct((n_blocks, 1, ATTN_WIDTH), jnp.float32),
    )
    return pl.pallas_call(
        _in_proj_kernel,
        grid=(n_tiles,),
        in_specs=[
            row_spec(D_MODEL),
            pl.BlockSpec((1, D_MODEL), lambda i: (0, 0)),
            pl.BlockSpec((D_MODEL, IN_WIDTH), lambda i: (0, 0), pipeline_mode=pl.Buffered(1)),
            tab_spec, tab_spec,
        ],
        out_specs=(
            transposed_spec(LANES), row_spec(ATTN_WIDTH), transposed_spec(VT_ROWS),
            row_spec(ATTN_WIDTH), row_spec(POOL_WIDTH), row_spec(POOL_WIDTH),
            pl.BlockSpec((blocks_per_tile, 1, ATTN_WIDTH), lambda i: (i, 0, 0)),
        ),
        out_shape=out_shape,
        scratch_shapes=[pltpu.VMEM((D_MODEL, IN_WIDTH), jnp.bfloat16)],
        compiler_params=pltpu.CompilerParams(
            dimension_semantics=("arbitrary",), vmem_limit_bytes=VMEM_LIMIT),
        name="in_proj",
    )(x2, gain, w_in, cos_t, sin_t)


def _select_bias_t(gate_t, n_past):
    slot = lax.broadcasted_iota(jnp.int32, gate_t.shape, 0)
    slot_f = slot.astype(jnp.float32)
    neg_inf = jnp.float32(-jnp.inf)
    g = jnp.where(slot < n_past, gate_t, neg_inf)
    bias = jnp.full(gate_t.shape, MASK_BIAS, dtype=jnp.float32)
    for _ in range(MOBA_TOPK):
        best = jnp.max(g, axis=0, keepdims=True)
        first = jnp.min(jnp.where(g == best, slot_f, float(gate_t.shape[0])), axis=0, keepdims=True)
        pick = (slot_f == first) & (best > neg_inf)
        bias = jnp.where(pick, 0.0, bias)
        g = jnp.where(pick, neg_inf, g)
    return bias


def _visited_tiles(n_past):
    rest = n_past % KEY_UNROLL
    whole = n_past - rest
    if isinstance(n_past, int):
        return whole + (0 if rest == 0 else HALF_UNROLL if rest <= HALF_UNROLL else KEY_UNROLL)
    return whole + jnp.where(rest == 0, 0, jnp.where(rest <= HALF_UNROLL, HALF_UNROLL, KEY_UNROLL))


def _fold_rows(t, op):
    return op(t.reshape(t.shape[0] // SUBLANES, SUBLANES, t.shape[1]), axis=0)


def _moba_kernel(qt_ref, k_ref, vt_ref, onehot_ref, kmean_ref, ga_ref, o_ref, qaug_ref, s_ref, own_ref):
    blk = MOBA_BLOCK
    n_blocks = kmean_ref.shape[0]
    heads = range(HEADS_PER_TILE)

    feat = lax.broadcasted_iota(jnp.int32, (LANES, blk), 0)
    km = kmean_ref[:, 0, :]
    km_hi = km.astype(jnp.bfloat16)
    km_lo = (km - km_hi.astype(jnp.float32)).astype(jnp.bfloat16)
    key_pos = lax.broadcasted_iota(jnp.int32, (blk, blk), 0)
    qry_pos = lax.broadcasted_iota(jnp.int32, (blk, blk), 1)
    causal = key_pos <= qry_pos

    def block_rows(j):
        return pl.ds(pl.multiple_of(j * blk, blk), blk)

    def prepare(j):
        qt = qt_ref[j].astype(jnp.float32)
        k_own = k_ref[block_rows(j), :]
        m_init = []
        for h in heads:
            qh = jnp.where((feat // HEAD_DIM) == h, qt, 0.0).astype(jnp.bfloat16)
            gate_t = (jnp.dot(km_hi, qh, preferred_element_type=jnp.float32)
                      + jnp.dot(km_lo, qh, preferred_element_type=jnp.float32))
            qaug_ref[h, :LANES, :] = qh
            qaug_ref[h, LANES:LANES + n_blocks, :] = _select_bias_t(gate_t, j).astype(jnp.bfloat16)
            qaug_ref[h, LANES + n_blocks:, :] = jnp.zeros((LANES - n_blocks, blk), jnp.bfloat16)
            s = jnp.dot(k_own, qh, preferred_element_type=jnp.float32)
            s = jnp.where(causal, s, -jnp.inf)
            own_ref[h] = s
            m_init.append(_fold_rows(s, jnp.max))
        return tuple(m_init)

    def probs(s, m_fin):
        return jnp.exp2(s - m_fin).astype(jnp.bfloat16)

    def weighted_values(p, n, h):
        v_aug = vt_ref[n, h * VT_HEAD_ROWS:(h + 1) * VT_HEAD_ROWS, :]
        r = jnp.dot(v_aug, p, preferred_element_type=jnp.float32)
        return r[:HEAD_DIM], r[HEAD_DIM:HEAD_DIM + SUBLANES]

    def own_probs(m_fin):
        return tuple(probs(own_ref[h], m_fin[h]) for h in heads)

    def own_values(j, p_own):
        return tuple(weighted_values(p_own[h], j, h) for h in heads)

    def column_max(m_run):
        return tuple(jnp.max(m, axis=0, keepdims=True) for m in m_run)

    def emit(j, acc):
        out_t = jnp.concatenate([o / l[:1] for o, l in acc], axis=0)
        rows = block_rows(j)
        o_ref[rows, :] = (out_t.T * _silu(ga_ref[rows, :])).astype(o_ref.dtype)

    def group_body(finish_old, score_new, m_prev, unroll=KEY_UNROLL, first=0):
        def body(g, carry):
            m_run, acc = [list(c) for c in carry]
            for i in range(unroll):
                n = first + g * unroll + i
                if score_new:
                    k_aug = jnp.concatenate([k_ref[block_rows(n), :], onehot_ref[block_rows(n), :]], axis=1)
                for h in heads:
                    if finish_old:
                        o, l = weighted_values(probs(s_ref[h, n], m_prev[h]), n, h)
                        acc[h] = (acc[h][0] + o, acc[h][1] + l)
                    if score_new:
                        s = jnp.dot(k_aug, qaug_ref[h], preferred_element_type=jnp.float32)
                        s_ref[h, n] = s
                        m_run[h] = jnp.maximum(m_run[h], _fold_rows(s, jnp.max))
            return tuple(m_run), tuple(acc)
        return body

    m_fin0 = column_max(prepare(0))
    acc0 = own_values(0, own_probs(m_fin0))
    m_init1 = prepare(1)

    def step(t, carry):
        m_init, acc, m_prev = carry
        old_tiles = _visited_tiles(t - 1)
        wide = old_tiles // WIDE_UNROLL
        done = wide * WIDE_UNROLL
        whole = (old_tiles - done) // KEY_UNROLL
        half = (old_tiles - done - whole * KEY_UNROLL) // HALF_UNROLL
        extra = (_visited_tiles(t) - old_tiles) // HALF_UNROLL
        inner = (m_init, acc)
        inner = lax.fori_loop(0, wide, group_body(True, True, m_prev, WIDE_UNROLL), inner)
        inner = lax.fori_loop(0, whole, group_body(True, True, m_prev, KEY_UNROLL, done), inner)
        inner = lax.fori_loop(0, half, group_body(True, True, m_prev, HALF_UNROLL,
                                                  done + whole * KEY_UNROLL), inner)
        inner = lax.fori_loop(0, extra, group_body(False, True, m_prev, HALF_UNROLL, old_tiles), inner)
        m_run, acc = inner
        m_fin = column_max(m_run)
        emit(t - 1, acc)
        p_own = own_probs(m_fin)
        m_next = prepare(jnp.minimum(t + 1, n_blocks - 1))
        acc_next = own_values(t, p_own)
        return m_next, acc_next, m_fin

    _, acc, m_prev = lax.fori_loop(1, n_blocks, step, (m_init1, acc0, m_fin0))

    last = n_blocks - 1
    last_tiles = _visited_tiles(last)
    whole = last_tiles // KEY_UNROLL
    unused_max = tuple(jnp.zeros((SUBLANES, blk), jnp.float32) for _ in heads)
    inner = lax.fori_loop(0, whole, group_body(True, False, m_prev), (unused_max, acc))
    if last_tiles % KEY_UNROLL:
        inner = group_body(True, False, m_prev, HALF_UNROLL, whole * KEY_UNROLL)(0, inner)
    emit(last, inner[1])


def _moba_attention(qt, k, vt, onehot, kmean, gate_attn):
    batch, seq, _ = k.shape
    n_blocks = seq // MOBA_BLOCK
    slots = _visited_tiles(n_blocks - 1)
    per_block = lambda r: pl.BlockSpec((None, n_blocks, None, r, MOBA_BLOCK), lambda b, hp: (b, 0, hp, 0, 0))
    per_row = pl.BlockSpec((None, seq, LANES), lambda b, hp: (b, 0, hp))
    return pl.pallas_call(
        _moba_kernel,
        grid=(batch, HEAD_TILES),
        in_specs=[
            per_block(LANES),
            per_row,
            per_block(VT_ROWS),
            pl.BlockSpec((seq, LANES), lambda b, hp: (0, 0), pipeline_mode=pl.Buffered(1)),
            pl.BlockSpec((n_blocks, 1, LANES), lambda b, hp: (b, 0, hp)),
            per_row,
        ],
        out_specs=per_row,
        out_shape=jax.ShapeDtypeStruct((batch, seq, ATTN_WIDTH), jnp.bfloat16),
        scratch_shapes=[
            pltpu.VMEM((HEADS_PER_TILE, 2 * LANES, MOBA_BLOCK), jnp.bfloat16),
            pltpu.VMEM((HEADS_PER_TILE, slots, MOBA_BLOCK, MOBA_BLOCK), jnp.float32),
            pltpu.VMEM((HEADS_PER_TILE, MOBA_BLOCK, MOBA_BLOCK), jnp.float32),
        ],
        compiler_params=pltpu.CompilerParams(
            dimension_semantics=("arbitrary", "arbitrary"), vmem_limit_bytes=VMEM_LIMIT),
        name="moba_attention",
    )(qt, k, vt, onehot, kmean, gate_attn)


def _out_proj_kernel(apply_final_norm, x_ref, attn_ref, u_ref, uprev_ref, gp_ref, wpool_ref, bpool_ref,
                     pscale_ref, wout_ref, fgain_ref, o_ref, ext_ref, wbf_ref):
    i = pl.program_id(1)

    @pl.when((pl.program_id(0) == 0) & (i == 0))
    def _():
        wbf_ref[...] = wout_ref[...].astype(wbf_ref.dtype)

    halo = uprev_ref[...]
    ext_ref[:POOL_HALO, :] = jnp.where(i > 0, halo, jnp.zeros_like(halo))
    ext_ref[POOL_HALO:, :] = u_ref[...]

    for first in range(0, ROW_TILE, ROW_CHUNK):
        rows = slice(first, first + ROW_CHUNK)
        pos = i * ROW_TILE + first + lax.broadcasted_iota(jnp.int32, (ROW_CHUNK, 1), 0)
        pooled_parts = []
        for g, window in enumerate(POOL_WINDOWS):
            sl = slice(g * POOL_GROUP, (g + 1) * POOL_GROUP)
            total = ext_ref[first:first + POOL_HALO + ROW_CHUNK, sl]
            span = 1
            while span < window:
                total = total + pltpu.roll(total, span, axis=0)
                span *= 2
            total = total[POOL_HALO:]
            u_g = ext_ref[POOL_HALO + first:POOL_HALO + first + ROW_CHUNK, sl]
            count = jnp.minimum(pos + 1, window).astype(jnp.float32)
            pooled = total / count - u_g
            y = jnp.dot(pooled.astype(jnp.bfloat16), wpool_ref[g].astype(jnp.bfloat16),
                        preferred_element_type=jnp.float32)
            pooled_parts.append(y + bpool_ref[g])
        pool = jnp.concatenate(pooled_parts, axis=1) * pscale_ref[...]
        pool = (pool * _silu(gp_ref[rows, :])).astype(jnp.bfloat16)

        mixed = jnp.concatenate([attn_ref[rows, :], pool], axis=1)
        y = x_ref[rows, :] + jnp.dot(mixed, wbf_ref[...], preferred_element_type=jnp.float32)
        if apply_final_norm:
            inv = lax.rsqrt(jnp.mean(y * y, axis=-1, keepdims=True) + EPS)
            y = y * inv * fgain_ref[...]
        o_ref[rows, :] = y


def _out_proj(x, attn, u, gate_pool, w_pool, b_pool, pool_scale, w_out, final_gain, apply_final_norm):
    batch, seq, _ = x.shape
    tiles = seq // ROW_TILE
    halo_per_tile = ROW_TILE // POOL_HALO
    row_spec = lambda w: pl.BlockSpec((None, ROW_TILE, w), lambda b, i: (b, i, 0))
    const2 = lambda shape: pl.BlockSpec(shape, lambda b, i: (0, 0))
    const3 = lambda shape: pl.BlockSpec(shape, lambda b, i: (0, 0, 0))
    return pl.pallas_call(
        functools.partial(_out_proj_kernel, apply_final_norm),
        grid=(batch, tiles),
        in_specs=[
            row_spec(D_MODEL),
            row_spec(ATTN_WIDTH),
            row_spec(POOL_WIDTH),
            pl.BlockSpec((None, POOL_HALO, POOL_WIDTH),
                         lambda b, i: (b, jnp.maximum(i * halo_per_tile - 1, 0), 0)),
            row_spec(POOL_WIDTH),
            const3((len(POOL_WINDOWS), POOL_GROUP, POOL_GROUP)),
            const3((len(POOL_WINDOWS), 1, POOL_GROUP)),
            const2((1, POOL_WIDTH)),
            pl.BlockSpec((D_MODEL, D_MODEL), lambda b, i: (0, 0), pipeline_mode=pl.Buffered(1)),
            const2((1, D_MODEL)),
        ],
        out_specs=row_spec(D_MODEL),
        out_shape=jax.ShapeDtypeStruct((batch, seq, D_MODEL), jnp.float32),
        scratch_shapes=[pltpu.VMEM((POOL_HALO + ROW_TILE, POOL_WIDTH), jnp.float32),
                        pltpu.VMEM((D_MODEL, D_MODEL), jnp.bfloat16)],
        compiler_params=pltpu.CompilerParams(
            dimension_semantics=("arbitrary", "arbitrary"), vmem_limit_bytes=VMEM_LIMIT),
        name="pool_out_proj",
    )(x, attn, u, u, gate_pool, w_pool, b_pool, pool_scale, w_out, final_gain)


def _position_tables(seq):
    pos = np.arange(seq, dtype=np.float64)
    inv_freq = 1.0 / (ROPE_THETA ** (np.arange(0, HEAD_DIM, 2, dtype=np.float64) / HEAD_DIM))
    ang = pos[:, None] * inv_freq[None, :]
    cos, sin = np.cos(ang), np.sin(ang)
    cos_t = np.tile(cos, (1, LANES // (HEAD_DIM // 2)))
    sin_t = np.tile(np.concatenate([-sin, sin], axis=1), (1, HEADS_PER_TILE))
    onehot = np.arange(seq)[:, None] // MOBA_BLOCK == np.arange(LANES)[None, :]
    return (jnp.asarray(cos_t, jnp.float32), jnp.asarray(sin_t, jnp.float32),
            jnp.asarray(onehot, jnp.bfloat16))


def kernel(x, norm_gain, w_in, w_pool, b_pool, pool_scale, w_out, final_gain):
    batch, seq, d_model = x.shape
    depth = w_in.shape[0]
    assert d_model == D_MODEL and seq % ROW_TILE == 0 and 2 <= seq // MOBA_BLOCK <= LANES
    n_blocks = seq // MOBA_BLOCK
    cos_t, sin_t, onehot = _position_tables(seq)

    for l in range(depth):
        qt, k, vt, gate_attn, u, gate_pool, kmean = _in_proj(
            x.reshape(batch * seq, d_model), norm_gain[l][None, :], w_in[l], cos_t, sin_t, seq)
        shape3 = lambda t: t.reshape(batch, seq, t.shape[-1])
        per_block = lambda t: t.reshape(batch, n_blocks, HEAD_TILES, t.shape[-2], MOBA_BLOCK)
        attn = _moba_attention(per_block(qt), shape3(k), per_block(vt), onehot, kmean, shape3(gate_attn))
        x = _out_proj(x, attn, shape3(u), shape3(gate_pool), w_pool[l], b_pool[l][:, None, :],
                      pool_scale[l][None, :], w_out[l], final_gain[None, :],
                      apply_final_norm=(l == depth - 1))
    return x
```

```python
import functools

import numpy as np

import jax
import jax.numpy as jnp
from jax import lax
from jax.experimental import pallas as pl
from jax.experimental.pallas import tpu as pltpu

D_MODEL = 1024
ATTN_WIDTH = D_MODEL // 2
POOL_WIDTH = D_MODEL - ATTN_WIDTH
HEAD_DIM = 64
MOBA_BLOCK = 256
MOBA_TOPK = 3
POOL_WINDOWS = (2, 4, 8, 16)
POOL_GROUP = POOL_WIDTH // len(POOL_WINDOWS)
ROPE_THETA = 10000.0
EPS = 1e-6
IN_WIDTH = 4 * ATTN_WIDTH + 2 * POOL_WIDTH

LANES = 128
SUBLANES = 8
HEADS_PER_TILE = LANES // HEAD_DIM
HEAD_TILES = ATTN_WIDTH // LANES
ROW_TILE = 1024
POOL_HALO = 16
MASK_BIAS = -1e30
KEY_UNROLL = 8
HALF_UNROLL = KEY_UNROLL // 2
WIDE_UNROLL = KEY_UNROLL * 2
BF16_ROWS = 16
VT_HEAD_ROWS = HEAD_DIM + BF16_ROWS
VT_ROWS = HEADS_PER_TILE * VT_HEAD_ROWS
Q_SCALE = HEAD_DIM ** -0.5 * 1.4426950408889634
VMEM_LIMIT = 56 * 1024 * 1024


def _silu(t):
    half = 0.5 * t
    return half + half * jnp.tanh(half)


def _rope(t, cos, sin_signed, first_half):
    swapped = jnp.where(first_half, pltpu.roll(t, LANES - HEAD_DIM // 2, axis=1),
                        pltpu.roll(t, HEAD_DIM // 2, axis=1))
    return t * cos + swapped * sin_signed


def _in_proj_kernel(seq_tiles, x_ref, gain_ref, w_ref, cos_ref, sin_ref, wpool_ref, bpool_ref, pscale_ref,
                    qt_ref, k_ref, vt_ref, ga_ref, pool_ref, kmean_ref, wbf_ref, ext_ref):
    @pl.when(pl.program_id(0) == 0)
    def _():
        wbf_ref[...] = w_ref[...].astype(wbf_ref.dtype)

    x = x_ref[...]
    inv = lax.rsqrt(jnp.mean(x * x, axis=-1, keepdims=True) + EPS)
    h = (x * inv * gain_ref[...]).astype(jnp.bfloat16)

    def proj(c):
        return jnp.dot(h, wbf_ref[:, c * ATTN_WIDTH:(c + 1) * ATTN_WIDTH],
                       preferred_element_type=jnp.float32)

    cos = cos_ref[...]
    sin = sin_ref[...]
    lane = lax.broadcasted_iota(jnp.int32, (ROW_TILE, LANES), 1)
    first_half = (lane % HEAD_DIM) < (HEAD_DIM // 2)

    q = proj(0)
    k = proj(1)
    v = proj(2)
    for t in range(HEAD_TILES):
        sl = slice(t * LANES, (t + 1) * LANES)
        qr = _rope(q[:, sl], cos, sin, first_half) * Q_SCALE
        kr = _rope(k[:, sl], cos, sin, first_half)
        k_ref[:, sl] = kr.astype(k_ref.dtype)
        for b in range(ROW_TILE // MOBA_BLOCK):
            blk_rows = slice(b * MOBA_BLOCK, (b + 1) * MOBA_BLOCK)
            kmean_ref[b, :, sl] = jnp.sum(kr[blk_rows], axis=0, keepdims=True) * (1.0 / MOBA_BLOCK)
            qt_ref[b, t] = qr[blk_rows].T.astype(qt_ref.dtype)
            v_t = v[blk_rows, sl].T.astype(vt_ref.dtype)
            for head in range(HEADS_PER_TILE):
                base = head * VT_HEAD_ROWS
                vt_ref[b, t, base:base + HEAD_DIM] = v_t[head * HEAD_DIM:(head + 1) * HEAD_DIM]
                vt_ref[b, t, base + HEAD_DIM:base + VT_HEAD_ROWS] = jnp.ones((BF16_ROWS, MOBA_BLOCK),
                                                                             vt_ref.dtype)
    ga_ref[...] = proj(3)

    tile_in_seq = pl.program_id(0) % seq_tiles

    @pl.when(pl.program_id(0) == 0)
    def _():
        ext_ref[...] = jnp.zeros_like(ext_ref)

    history = ext_ref[ROW_TILE:, :]
    ext_ref[:POOL_HALO, :] = jnp.where(tile_in_seq > 0, history, jnp.zeros_like(history))
    ext_ref[POOL_HALO:, :] = proj(4)
    pos = tile_in_seq * ROW_TILE + lax.broadcasted_iota(jnp.int32, (ROW_TILE, 1), 0)
    gate_pool = _silu(proj(5))
    for g, window in enumerate(POOL_WINDOWS):
        sl = slice(g * POOL_GROUP, (g + 1) * POOL_GROUP)
        total = ext_ref[:, sl]
        span = 1
        while span < window:
            total = total + pltpu.roll(total, span, axis=0)
            span *= 2
        total = total[POOL_HALO:]
        count = jnp.minimum(pos + 1, window).astype(jnp.float32)
        pooled = total / count - ext_ref[POOL_HALO:, sl]
        y = jnp.dot(pooled.astype(jnp.bfloat16), wpool_ref[g].astype(jnp.bfloat16),
                    preferred_element_type=jnp.float32)
        y = (y + bpool_ref[g]) * pscale_ref[:, sl]
        pool_ref[:, sl] = (y * gate_pool[:, sl]).astype(pool_ref.dtype)


def _in_proj(x2, gain, w_in, cos_t, sin_t, w_pool, b_pool, pool_scale, seq):
    rows = x2.shape[0]
    n_tiles = rows // ROW_TILE
    seq_tiles = seq // ROW_TILE
    row_spec = lambda w: pl.BlockSpec((ROW_TILE, w), lambda i: (i, 0))
    tab_spec = pl.BlockSpec((ROW_TILE, LANES), lambda i: (i % seq_tiles, 0))
    blocks_per_tile = ROW_TILE // MOBA_BLOCK
    n_blocks = rows // MOBA_BLOCK
    transposed = lambda r: jax.ShapeDtypeStruct((n_blocks, HEAD_TILES, r, MOBA_BLOCK), jnp.bfloat16)
    transposed_spec = lambda r: pl.BlockSpec((blocks_per_tile, HEAD_TILES, r, MOBA_BLOCK),
                                             lambda i: (i, 0, 0, 0))
    out_shape = (
        transposed(LANES),
        jax.ShapeDtypeStruct((rows, ATTN_WIDTH), jnp.bfloat16),
        transposed(VT_ROWS),
        jax.ShapeDtypeStruct((rows, ATTN_WIDTH), jnp.float32),
        jax.ShapeDtypeStruct((rows, POOL_WIDTH), jnp.bfloat16),
        jax.ShapeDtypeStruct((n_blocks, 1, ATTN_WIDTH), jnp.float32),
    )
    return pl.pallas_call(
        functools.partial(_in_proj_kernel, seq_tiles),
        grid=(n_tiles,),
        in_specs=[
            row_spec(D_MODEL),
            pl.BlockSpec((1, D_MODEL), lambda i: (0, 0)),
            pl.BlockSpec((D_MODEL, IN_WIDTH), lambda i: (0, 0), pipeline_mode=pl.Buffered(1)),
            tab_spec, tab_spec,
            pl.BlockSpec((len(POOL_WINDOWS), POOL_GROUP, POOL_GROUP), lambda i: (0, 0, 0)),
            pl.BlockSpec((len(POOL_WINDOWS), 1, POOL_GROUP), lambda i: (0, 0, 0)),
            pl.BlockSpec((1, POOL_WIDTH), lambda i: (0, 0)),
        ],
        out_specs=(
            transposed_spec(LANES), row_spec(ATTN_WIDTH), transposed_spec(VT_ROWS),
            row_spec(ATTN_WIDTH), row_spec(POOL_WIDTH),
            pl.BlockSpec((blocks_per_tile, 1, ATTN_WIDTH), lambda i: (i, 0, 0)),
        ),
        out_shape=out_shape,
        scratch_shapes=[pltpu.VMEM((D_MODEL, IN_WIDTH), jnp.bfloat16),
                        pltpu.VMEM((POOL_HALO + ROW_TILE, POOL_WIDTH), jnp.float32)],
        compiler_params=pltpu.CompilerParams(
            dimension_semantics=("arbitrary",), vmem_limit_bytes=VMEM_LIMIT),
        name="in_proj",
    )(x2, gain, w_in, cos_t, sin_t, w_pool, b_pool, pool_scale)


def _select_bias_t(gate_t, n_past):
    slot = lax.broadcasted_iota(jnp.int32, gate_t.shape, 0)
    slot_f = slot.astype(jnp.float32)
    neg_inf = jnp.float32(-jnp.inf)
    g = jnp.where(slot < n_past, gate_t, neg_inf)
    bias = jnp.full(gate_t.shape, MASK_BIAS, dtype=jnp.float32)
    for _ in range(MOBA_TOPK):
        best = jnp.max(g, axis=0, keepdims=True)
        first = jnp.min(jnp.where(g == best, slot_f, float(gate_t.shape[0])), axis=0, keepdims=True)
        pick = (slot_f == first) & (best > neg_inf)
        bias = jnp.where(pick, 0.0, bias)
        g = jnp.where(pick, neg_inf, g)
    return bias


def _visited_tiles(n_past):
    rest = n_past % KEY_UNROLL
    whole = n_past - rest
    if isinstance(n_past, int):
        return whole + (0 if rest == 0 else HALF_UNROLL if rest <= HALF_UNROLL else KEY_UNROLL)
    return whole + jnp.where(rest == 0, 0, jnp.where(rest <= HALF_UNROLL, HALF_UNROLL, KEY_UNROLL))


def _fold_rows(t, op):
    return op(t.reshape(t.shape[0] // SUBLANES, SUBLANES, t.shape[1]), axis=0)


def _moba_kernel(qt_ref, k_ref, vt_ref, onehot_ref, kmean_ref, ga_ref, o_ref, qaug_ref, s_ref, own_ref):
    blk = MOBA_BLOCK
    n_blocks = kmean_ref.shape[0]
    heads = range(HEADS_PER_TILE)

    feat = lax.broadcasted_iota(jnp.int32, (LANES, blk), 0)
    km = kmean_ref[:, 0, :]
    km_hi = km.astype(jnp.bfloat16)
    km_lo = (km - km_hi.astype(jnp.float32)).astype(jnp.bfloat16)
    key_pos = lax.broadcasted_iota(jnp.int32, (blk, blk), 0)
    qry_pos = lax.broadcasted_iota(jnp.int32, (blk, blk), 1)
    causal = key_pos <= qry_pos

    def block_rows(j):
        return pl.ds(pl.multiple_of(j * blk, blk), blk)

    def prepare(j):
        qt = qt_ref[j].astype(jnp.float32)
        k_own = k_ref[block_rows(j), :]
        m_init = []
        for h in heads:
            qh = jnp.where((feat // HEAD_DIM) == h, qt, 0.0).astype(jnp.bfloat16)
            gate_t = (jnp.dot(km_hi, qh, preferred_element_type=jnp.float32)
                      + jnp.dot(km_lo, qh, preferred_element_type=jnp.float32))
            qaug_ref[h, :LANES, :] = qh
            qaug_ref[h, LANES:LANES + n_blocks, :] = _select_bias_t(gate_t, j).astype(jnp.bfloat16)
            qaug_ref[h, LANES + n_blocks:, :] = jnp.zeros((LANES - n_blocks, blk), jnp.bfloat16)
            s = jnp.dot(k_own, qh, preferred_element_type=jnp.float32)
            s = jnp.where(causal, s, -jnp.inf)
            own_ref[h] = s
            m_init.append(_fold_rows(s, jnp.max))
        return tuple(m_init)

    def probs(s, m_fin):
        return jnp.exp2(s - m_fin).astype(jnp.bfloat16)

    def weighted_values(p, n, h):
        v_aug = vt_ref[n, h * VT_HEAD_ROWS:(h + 1) * VT_HEAD_ROWS, :]
        r = jnp.dot(v_aug, p, preferred_element_type=jnp.float32)
        return r[:HEAD_DIM], r[HEAD_DIM:HEAD_DIM + SUBLANES]

    def own_probs(m_fin):
        return tuple(probs(own_ref[h], m_fin[h]) for h in heads)

    def own_values(j, p_own):
        return tuple(weighted_values(p_own[h], j, h) for h in heads)

    def column_max(m_run):
        return tuple(jnp.max(m, axis=0, keepdims=True) for m in m_run)

    def emit(j, acc):
        out_t = jnp.concatenate([o / l[:1] for o, l in acc], axis=0)
        rows = block_rows(j)
        o_ref[rows, :] = (out_t.T * _silu(ga_ref[rows, :])).astype(o_ref.dtype)

    def group_body(finish_old, score_new, m_prev, unroll=KEY_UNROLL, first=0):
        def body(g, carry):
            m_run, acc = [list(c) for c in carry]
            for i in range(unroll):
                n = first + g * unroll + i
                if score_new:
                    k_aug = jnp.concatenate([k_ref[block_rows(n), :], onehot_ref[block_rows(n), :]], axis=1)
                for h in heads:
                    if finish_old:
                        o, l = weighted_values(probs(s_ref[h, n], m_prev[h]), n, h)
                        acc[h] = (acc[h][0] + o, acc[h][1] + l)
                    if score_new:
                        s = jnp.dot(k_aug, qaug_ref[h], preferred_element_type=jnp.float32)
                        s_ref[h, n] = s
                        m_run[h] = jnp.maximum(m_run[h], _fold_rows(s, jnp.max))
            return tuple(m_run), tuple(acc)
        return body

    m_fin0 = column_max(prepare(0))
    acc0 = own_values(0, own_probs(m_fin0))
    m_init1 = prepare(1)

    def step(t, carry):
        m_init, acc, m_prev = carry
        old_tiles = _visited_tiles(t - 1)
        wide = old_tiles // WIDE_UNROLL
        done = wide * WIDE_UNROLL
        whole = (old_tiles - done) // KEY_UNROLL
        half = (old_tiles - done - whole * KEY_UNROLL) // HALF_UNROLL
        extra = (_visited_tiles(t) - old_tiles) // HALF_UNROLL
        inner = (m_init, acc)
        inner = lax.fori_loop(0, wide, group_body(True, True, m_prev, WIDE_UNROLL), inner)
        inner = lax.fori_loop(0, whole, group_body(True, True, m_prev, KEY_UNROLL, done), inner)
        inner = lax.fori_loop(0, half, group_body(True, True, m_prev, HALF_UNROLL,
                                                  done + whole * KEY_UNROLL), inner)
        inner = lax.fori_loop(0, extra, group_body(False, True, m_prev, HALF_UNROLL, old_tiles), inner)
        m_run, acc = inner
        m_fin = column_max(m_run)
        emit(t - 1, acc)
        p_own = own_probs(m_fin)
        m_next = prepare(jnp.minimum(t + 1, n_blocks - 1))
        acc_next = own_values(t, p_own)
        return m_next, acc_next, m_fin

    _, acc, m_prev = lax.fori_loop(1, n_blocks, step, (m_init1, acc0, m_fin0))

    last = n_blocks - 1
    last_tiles = _visited_tiles(last)
    whole = last_tiles // KEY_UNROLL
    unused_max = tuple(jnp.zeros((SUBLANES, blk), jnp.float32) for _ in heads)
    inner = lax.fori_loop(0, whole, group_body(True, False, m_prev), (unused_max, acc))
    if last_tiles % KEY_UNROLL:
        inner = group_body(True, False, m_prev, HALF_UNROLL, whole * KEY_UNROLL)(0, inner)
    emit(last, inner[1])


def _moba_attention(qt, k, vt, onehot, kmean, gate_attn):
    batch, seq, _ = k.shape
    n_blocks = seq // MOBA_BLOCK
    slots = _visited_tiles(n_blocks - 1)
    per_block = lambda r: pl.BlockSpec((None, n_blocks, None, r, MOBA_BLOCK), lambda b, hp: (b, 0, hp, 0, 0))
    per_row = pl.BlockSpec((None, seq, LANES), lambda b, hp: (b, 0, hp))
    return pl.pallas_call(
        _moba_kernel,
        grid=(batch, HEAD_TILES),
        in_specs=[
            per_block(LANES),
            per_row,
            per_block(VT_ROWS),
            pl.BlockSpec((seq, LANES), lambda b, hp: (0, 0), pipeline_mode=pl.Buffered(1)),
            pl.BlockSpec((n_blocks, 1, LANES), lambda b, hp: (b, 0, hp)),
            per_row,
        ],
        out_specs=per_row,
        out_shape=jax.ShapeDtypeStruct((batch, seq, ATTN_WIDTH), jnp.bfloat16),
        scratch_shapes=[
            pltpu.VMEM((HEADS_PER_TILE, 2 * LANES, MOBA_BLOCK), jnp.bfloat16),
            pltpu.VMEM((HEADS_PER_TILE, slots, MOBA_BLOCK, MOBA_BLOCK), jnp.float32),
            pltpu.VMEM((HEADS_PER_TILE, MOBA_BLOCK, MOBA_BLOCK), jnp.float32),
        ],
        compiler_params=pltpu.CompilerParams(
            dimension_semantics=("arbitrary", "arbitrary"), vmem_limit_bytes=VMEM_LIMIT),
        name="moba_attention",
    )(qt, k, vt, onehot, kmean, gate_attn)


def _out_proj_kernel(apply_final_norm, x_ref, attn_ref, pool_ref, wout_ref, fgain_ref, o_ref, wbf_ref):
    @pl.when(pl.program_id(0) == 0)
    def _():
        wbf_ref[...] = wout_ref[...].astype(wbf_ref.dtype)

    mixed = jnp.concatenate([attn_ref[...], pool_ref[...]], axis=1)
    y = x_ref[...] + jnp.dot(mixed, wbf_ref[...], preferred_element_type=jnp.float32)
    if apply_final_norm:
        inv = lax.rsqrt(jnp.mean(y * y, axis=-1, keepdims=True) + EPS)
        y = y * inv * fgain_ref[...]
    o_ref[...] = y


def _out_proj(x2, attn, pool, w_out, final_gain, apply_final_norm):
    rows = x2.shape[0]
    row_spec = lambda w: pl.BlockSpec((ROW_TILE, w), lambda i: (i, 0))
    return pl.pallas_call(
        functools.partial(_out_proj_kernel, apply_final_norm),
        grid=(rows // ROW_TILE,),
        in_specs=[
            row_spec(D_MODEL), row_spec(ATTN_WIDTH), row_spec(POOL_WIDTH),
            pl.BlockSpec((D_MODEL, D_MODEL), lambda i: (0, 0), pipeline_mode=pl.Buffered(1)),
            pl.BlockSpec((1, D_MODEL), lambda i: (0, 0)),
        ],
        out_specs=row_spec(D_MODEL),
        out_shape=jax.ShapeDtypeStruct((rows, D_MODEL), jnp.float32),
        scratch_shapes=[pltpu.VMEM((D_MODEL, D_MODEL), jnp.bfloat16)],
        compiler_params=pltpu.CompilerParams(
            dimension_semantics=("arbitrary",), vmem_limit_bytes=VMEM_LIMIT),
        name="out_proj",
    )(x2, attn, pool, w_out, final_gain)


def _position_tables(seq):
    pos = np.arange(seq, dtype=np.float64)
    inv_freq = 1.0 / (ROPE_THETA ** (np.arange(0, HEAD_DIM, 2, dtype=np.float64) / HEAD_DIM))
    ang = pos[:, None] * inv_freq[None, :]
    cos, sin = np.cos(ang), np.sin(ang)
    cos_t = np.tile(cos, (1, LANES // (HEAD_DIM // 2)))
    sin_t = np.tile(np.concatenate([-sin, sin], axis=1), (1, HEADS_PER_TILE))
    onehot = np.arange(seq)[:, None] // MOBA_BLOCK == np.arange(LANES)[None, :]
    return (jnp.asarray(cos_t, jnp.float32), jnp.asarray(sin_t, jnp.float32),
            jnp.asarray(onehot, jnp.bfloat16))


def kernel(x, norm_gain, w_in, w_pool, b_pool, pool_scale, w_out, final_gain):
    batch, seq, d_model = x.shape
    depth = w_in.shape[0]
    assert d_model == D_MODEL and seq % ROW_TILE == 0 and 2 <= seq // MOBA_BLOCK <= LANES
    n_blocks = seq // MOBA_BLOCK
    cos_t, sin_t, onehot = _position_tables(seq)

    x2 = x.reshape(batch * seq, d_model)
    for l in range(depth):
        qt, k, vt, gate_attn, pool, kmean = _in_proj(
            x2, norm_gain[l][None, :], w_in[l], cos_t, sin_t, w_pool[l], b_pool[l][:, None, :],
            pool_scale[l][None, :], seq)
        shape3 = lambda t: t.reshape(batch, seq, t.shape[-1])
        per_block = lambda t: t.reshape(batch, n_blocks, HEAD_TILES, t.shape[-2], MOBA_BLOCK)
        attn = _moba_attention(per_block(qt), shape3(k), per_block(vt), onehot, kmean, shape3(gate_attn))
        x2 = _out_proj(x2, attn.reshape(batch * seq, ATTN_WIDTH), pool, w_out[l], final_gain[None, :],
                       apply_final_norm=(l == depth - 1))
    return x2.reshape(batch, seq, d_model)
```

```python
import functools

import numpy as np

import jax
import jax.numpy as jnp
from jax import lax
from jax.experimental import pallas as pl
from jax.experimental.pallas import tpu as pltpu

D_MODEL = 1024
ATTN_WIDTH = D_MODEL // 2
POOL_WIDTH = D_MODEL - ATTN_WIDTH
HEAD_DIM = 64
MOBA_BLOCK = 256
MOBA_TOPK = 3
POOL_WINDOWS = (2, 4, 8, 16)
POOL_GROUP = POOL_WIDTH // len(POOL_WINDOWS)
ROPE_THETA = 10000.0
EPS = 1e-6
IN_WIDTH = 4 * ATTN_WIDTH + 2 * POOL_WIDTH

LANES = 128
SUBLANES = 8
HEADS_PER_TILE = LANES // HEAD_DIM
HEAD_TILES = ATTN_WIDTH // LANES
ROW_TILE = 1024
RESIDUAL_SLOTS = 3
POOL_HALO = 16
MASK_BIAS = -1e30
KEY_UNROLL = 8
HALF_UNROLL = KEY_UNROLL // 2
WIDE_UNROLL = KEY_UNROLL * 2
BF16_ROWS = 16
VT_HEAD_ROWS = HEAD_DIM + BF16_ROWS
VT_ROWS = HEADS_PER_TILE * VT_HEAD_ROWS
Q_SCALE = HEAD_DIM ** -0.5 * 1.4426950408889634
VMEM_LIMIT = 56 * 1024 * 1024


def _silu(t):
    half = 0.5 * t
    return half + half * jnp.tanh(half)


def _rope(t, cos, sin_signed, first_half):
    swapped = jnp.where(first_half, pltpu.roll(t, LANES - HEAD_DIM // 2, axis=1),
                        pltpu.roll(t, HEAD_DIM // 2, axis=1))
    return t * cos + swapped * sin_signed


def _in_proj_kernel(seq_tiles, x_ref, gain_ref, w_ref, cos_ref, sin_ref, wpool_ref, bpool_ref, pscale_ref,
                    qt_ref, k_ref, vt_ref, ga_ref, pool_ref, kmean_ref, wbf_ref, ext_ref):
    @pl.when(pl.program_id(0) == 0)
    def _():
        wbf_ref[...] = w_ref[...].astype(wbf_ref.dtype)

    x = x_ref[...]
    inv = lax.rsqrt(jnp.mean(x * x, axis=-1, keepdims=True) + EPS)
    h = (x * inv * gain_ref[...]).astype(jnp.bfloat16)

    def proj(c):
        return jnp.dot(h, wbf_ref[:, c * ATTN_WIDTH:(c + 1) * ATTN_WIDTH],
                       preferred_element_type=jnp.float32)

    cos = cos_ref[...]
    sin = sin_ref[...]
    lane = lax.broadcasted_iota(jnp.int32, (ROW_TILE, LANES), 1)
    first_half = (lane % HEAD_DIM) < (HEAD_DIM // 2)

    q = proj(0)
    k = proj(1)
    v = proj(2)
    for t in range(HEAD_TILES):
        sl = slice(t * LANES, (t + 1) * LANES)
        qr = _rope(q[:, sl], cos, sin, first_half) * Q_SCALE
        kr = _rope(k[:, sl], cos, sin, first_half)
        k_ref[:, sl] = kr.astype(k_ref.dtype)
        for b in range(ROW_TILE // MOBA_BLOCK):
            blk_rows = slice(b * MOBA_BLOCK, (b + 1) * MOBA_BLOCK)
            kmean_ref[b, :, sl] = jnp.sum(kr[blk_rows], axis=0, keepdims=True) * (1.0 / MOBA_BLOCK)
            qt_ref[b, t] = qr[blk_rows].T.astype(qt_ref.dtype)
            v_t = v[blk_rows, sl].T.astype(vt_ref.dtype)
            for head in range(HEADS_PER_TILE):
                base = head * VT_HEAD_ROWS
                vt_ref[b, t, base:base + HEAD_DIM] = v_t[head * HEAD_DIM:(head + 1) * HEAD_DIM]
                vt_ref[b, t, base + HEAD_DIM:base + VT_HEAD_ROWS] = jnp.ones((BF16_ROWS, MOBA_BLOCK),
                                                                             vt_ref.dtype)
    ga_ref[...] = proj(3)

    tile_in_seq = pl.program_id(0) % seq_tiles

    @pl.when(pl.program_id(0) == 0)
    def _():
        ext_ref[...] = jnp.zeros_like(ext_ref)

    history = ext_ref[ROW_TILE:, :]
    ext_ref[:POOL_HALO, :] = jnp.where(tile_in_seq > 0, history, jnp.zeros_like(history))
    ext_ref[POOL_HALO:, :] = proj(4)
    pos = tile_in_seq * ROW_TILE + lax.broadcasted_iota(jnp.int32, (ROW_TILE, 1), 0)
    gate_pool = _silu(proj(5))
    for g, window in enumerate(POOL_WINDOWS):
        sl = slice(g * POOL_GROUP, (g + 1) * POOL_GROUP)
        total = ext_ref[:, sl]
        span = 1
        while span < window:
            total = total + pltpu.roll(total, span, axis=0)
            span *= 2
        total = total[POOL_HALO:]
        count = jnp.minimum(pos + 1, window).astype(jnp.float32)
        pooled = total / count - ext_ref[POOL_HALO:, sl]
        y = jnp.dot(pooled.astype(jnp.bfloat16), wpool_ref[g].astype(jnp.bfloat16),
                    preferred_element_type=jnp.float32)
        y = (y + bpool_ref[g]) * pscale_ref[:, sl]
        pool_ref[:, sl] = (y * gate_pool[:, sl]).astype(pool_ref.dtype)


def _in_proj(x2, gain, w_in, cos_t, sin_t, w_pool, b_pool, pool_scale, seq):
    rows = x2.shape[0]
    n_tiles = rows // ROW_TILE
    seq_tiles = seq // ROW_TILE
    row_spec = lambda w: pl.BlockSpec((ROW_TILE, w), lambda i: (i, 0))
    tab_spec = pl.BlockSpec((ROW_TILE, LANES), lambda i: (i % seq_tiles, 0))
    blocks_per_tile = ROW_TILE // MOBA_BLOCK
    n_blocks = rows // MOBA_BLOCK
    transposed = lambda r: jax.ShapeDtypeStruct((n_blocks, HEAD_TILES, r, MOBA_BLOCK), jnp.bfloat16)
    transposed_spec = lambda r: pl.BlockSpec((blocks_per_tile, HEAD_TILES, r, MOBA_BLOCK),
                                             lambda i: (i, 0, 0, 0))
    out_shape = (
        transposed(LANES),
        jax.ShapeDtypeStruct((rows, ATTN_WIDTH), jnp.bfloat16),
        transposed(VT_ROWS),
        jax.ShapeDtypeStruct((rows, ATTN_WIDTH), jnp.float32),
        jax.ShapeDtypeStruct((rows, POOL_WIDTH), jnp.bfloat16),
        jax.ShapeDtypeStruct((n_blocks, 1, ATTN_WIDTH), jnp.float32),
    )
    return pl.pallas_call(
        functools.partial(_in_proj_kernel, seq_tiles),
        grid=(n_tiles,),
        in_specs=[
            row_spec(D_MODEL),
            pl.BlockSpec((1, D_MODEL), lambda i: (0, 0)),
            pl.BlockSpec((D_MODEL, IN_WIDTH), lambda i: (0, 0), pipeline_mode=pl.Buffered(1)),
            tab_spec, tab_spec,
            pl.BlockSpec((len(POOL_WINDOWS), POOL_GROUP, POOL_GROUP), lambda i: (0, 0, 0)),
            pl.BlockSpec((len(POOL_WINDOWS), 1, POOL_GROUP), lambda i: (0, 0, 0)),
            pl.BlockSpec((1, POOL_WIDTH), lambda i: (0, 0)),
        ],
        out_specs=(
            transposed_spec(LANES), row_spec(ATTN_WIDTH), transposed_spec(VT_ROWS),
            row_spec(ATTN_WIDTH), row_spec(POOL_WIDTH),
            pl.BlockSpec((blocks_per_tile, 1, ATTN_WIDTH), lambda i: (i, 0, 0)),
        ),
        out_shape=out_shape,
        scratch_shapes=[pltpu.VMEM((D_MODEL, IN_WIDTH), jnp.bfloat16),
                        pltpu.VMEM((POOL_HALO + ROW_TILE, POOL_WIDTH), jnp.float32)],
        compiler_params=pltpu.CompilerParams(
            dimension_semantics=("arbitrary",), vmem_limit_bytes=VMEM_LIMIT),
        name="in_proj",
    )(x2, gain, w_in, cos_t, sin_t, w_pool, b_pool, pool_scale)


def _select_bias_t(gate_t, n_past):
    slot = lax.broadcasted_iota(jnp.int32, gate_t.shape, 0)
    slot_f = slot.astype(jnp.float32)
    neg_inf = jnp.float32(-jnp.inf)
    g = jnp.where(slot < n_past, gate_t, neg_inf)
    bias = jnp.full(gate_t.shape, MASK_BIAS, dtype=jnp.float32)
    for _ in range(MOBA_TOPK):
        best = jnp.max(g, axis=0, keepdims=True)
        first = jnp.min(jnp.where(g == best, slot_f, float(gate_t.shape[0])), axis=0, keepdims=True)
        pick = (slot_f == first) & (best > neg_inf)
        bias = jnp.where(pick, 0.0, bias)
        g = jnp.where(pick, neg_inf, g)
    return bias


def _visited_tiles(n_past):
    rest = n_past % KEY_UNROLL
    whole = n_past - rest
    if isinstance(n_past, int):
        return whole + (0 if rest == 0 else HALF_UNROLL if rest <= HALF_UNROLL else KEY_UNROLL)
    return whole + jnp.where(rest == 0, 0, jnp.where(rest <= HALF_UNROLL, HALF_UNROLL, KEY_UNROLL))


def _fold_rows(t, op):
    return op(t.reshape(t.shape[0] // SUBLANES, SUBLANES, t.shape[1]), axis=0)


def _moba_kernel(qt_ref, k_ref, vt_ref, onehot_ref, kmean_ref, ga_ref, o_ref, qaug_ref, s_ref, own_ref):
    blk = MOBA_BLOCK
    n_blocks = kmean_ref.shape[0]
    heads = range(HEADS_PER_TILE)

    feat = lax.broadcasted_iota(jnp.int32, (LANES, blk), 0)
    km = kmean_ref[:, 0, :]
    km_hi = km.astype(jnp.bfloat16)
    km_lo = (km - km_hi.astype(jnp.float32)).astype(jnp.bfloat16)
    key_pos = lax.broadcasted_iota(jnp.int32, (blk, blk), 0)
    qry_pos = lax.broadcasted_iota(jnp.int32, (blk, blk), 1)
    causal = key_pos <= qry_pos

    def block_rows(j):
        return pl.ds(pl.multiple_of(j * blk, blk), blk)

    def prepare(j):
        qt = qt_ref[j].astype(jnp.float32)
        k_own = k_ref[block_rows(j), :]
        m_init = []
        for h in heads:
            qh = jnp.where((feat // HEAD_DIM) == h, qt, 0.0).astype(jnp.bfloat16)
            gate_t = (jnp.dot(km_hi, qh, preferred_element_type=jnp.float32)
                      + jnp.dot(km_lo, qh, preferred_element_type=jnp.float32))
            qaug_ref[h, :LANES, :] = qh
            qaug_ref[h, LANES:LANES + n_blocks, :] = _select_bias_t(gate_t, j).astype(jnp.bfloat16)
            qaug_ref[h, LANES + n_blocks:, :] = jnp.zeros((LANES - n_blocks, blk), jnp.bfloat16)
            s = jnp.dot(k_own, qh, preferred_element_type=jnp.float32)
            s = jnp.where(causal, s, -jnp.inf)
            own_ref[h] = s
            m_init.append(_fold_rows(s, jnp.max))
        return tuple(m_init)

    def probs(s, m_fin):
        return jnp.exp2(s - m_fin).astype(jnp.bfloat16)

    def weighted_values(p, n, h):
        v_aug = vt_ref[n, h * VT_HEAD_ROWS:(h + 1) * VT_HEAD_ROWS, :]
        r = jnp.dot(v_aug, p, preferred_element_type=jnp.float32)
        return r[:HEAD_DIM], r[HEAD_DIM:HEAD_DIM + SUBLANES]

    def own_probs(m_fin):
        return tuple(probs(own_ref[h], m_fin[h]) for h in heads)

    def own_values(j, p_own):
        return tuple(weighted_values(p_own[h], j, h) for h in heads)

    def column_max(m_run):
        return tuple(jnp.max(m, axis=0, keepdims=True) for m in m_run)

    def emit(j, acc):
        out_t = jnp.concatenate([o / l[:1] for o, l in acc], axis=0)
        rows = block_rows(j)
        o_ref[rows, :] = (out_t.T * _silu(ga_ref[rows, :])).astype(o_ref.dtype)

    def group_body(finish_old, score_new, m_prev, unroll=KEY_UNROLL, first=0):
        def body(g, carry):
            m_run, acc = [list(c) for c in carry]
            for i in range(unroll):
                n = first + g * unroll + i
                if score_new:
                    k_aug = jnp.concatenate([k_ref[block_rows(n), :], onehot_ref[block_rows(n), :]], axis=1)
                for h in heads:
                    if finish_old:
                        o, l = weighted_values(probs(s_ref[h, n], m_prev[h]), n, h)
                        acc[h] = (acc[h][0] + o, acc[h][1] + l)
                    if score_new:
                        s = jnp.dot(k_aug, qaug_ref[h], preferred_element_type=jnp.float32)
                        s_ref[h, n] = s
                        m_run[h] = jnp.maximum(m_run[h], _fold_rows(s, jnp.max))
            return tuple(m_run), tuple(acc)
        return body

    m_fin0 = column_max(prepare(0))
    acc0 = own_values(0, own_probs(m_fin0))
    m_init1 = prepare(1)

    def step(t, carry):
        m_init, acc, m_prev = carry
        old_tiles = _visited_tiles(t - 1)
        wide = old_tiles // WIDE_UNROLL
        done = wide * WIDE_UNROLL
        whole = (old_tiles - done) // KEY_UNROLL
        half = (old_tiles - done - whole * KEY_UNROLL) // HALF_UNROLL
        extra = (_visited_tiles(t) - old_tiles) // HALF_UNROLL
        inner = (m_init, acc)
        inner = lax.fori_loop(0, wide, group_body(True, True, m_prev, WIDE_UNROLL), inner)
        inner = lax.fori_loop(0, whole, group_body(True, True, m_prev, KEY_UNROLL, done), inner)
        inner = lax.fori_loop(0, half, group_body(True, True, m_prev, HALF_UNROLL,
                                                  done + whole * KEY_UNROLL), inner)
        inner = lax.fori_loop(0, extra, group_body(False, True, m_prev, HALF_UNROLL, old_tiles), inner)
        m_run, acc = inner
        m_fin = column_max(m_run)
        emit(t - 1, acc)
        p_own = own_probs(m_fin)
        m_next = prepare(jnp.minimum(t + 1, n_blocks - 1))
        acc_next = own_values(t, p_own)
        return m_next, acc_next, m_fin

    _, acc, m_prev = lax.fori_loop(1, n_blocks, step, (m_init1, acc0, m_fin0))

    last = n_blocks - 1
    last_tiles = _visited_tiles(last)
    whole = last_tiles // KEY_UNROLL
    unused_max = tuple(jnp.zeros((SUBLANES, blk), jnp.float32) for _ in heads)
    inner = lax.fori_loop(0, whole, group_body(True, False, m_prev), (unused_max, acc))
    if last_tiles % KEY_UNROLL:
        inner = group_body(True, False, m_prev, HALF_UNROLL, whole * KEY_UNROLL)(0, inner)
    emit(last, inner[1])


def _moba_attention(qt, k, vt, onehot, kmean, gate_attn):
    batch, seq, _ = k.shape
    n_blocks = seq // MOBA_BLOCK
    slots = _visited_tiles(n_blocks - 1)
    per_block = lambda r: pl.BlockSpec((None, n_blocks, None, r, MOBA_BLOCK), lambda b, hp: (b, 0, hp, 0, 0))
    per_row = pl.BlockSpec((None, seq, LANES), lambda b, hp: (b, 0, hp))
    return pl.pallas_call(
        _moba_kernel,
        grid=(batch, HEAD_TILES),
        in_specs=[
            per_block(LANES),
            per_row,
            per_block(VT_ROWS),
            pl.BlockSpec((seq, LANES), lambda b, hp: (0, 0), pipeline_mode=pl.Buffered(1)),
            pl.BlockSpec((n_blocks, 1, LANES), lambda b, hp: (b, 0, hp)),
            per_row,
        ],
        out_specs=per_row,
        out_shape=jax.ShapeDtypeStruct((batch, seq, ATTN_WIDTH), jnp.bfloat16),
        scratch_shapes=[
            pltpu.VMEM((HEADS_PER_TILE, 2 * LANES, MOBA_BLOCK), jnp.bfloat16),
            pltpu.VMEM((HEADS_PER_TILE, slots, MOBA_BLOCK, MOBA_BLOCK), jnp.float32),
            pltpu.VMEM((HEADS_PER_TILE, MOBA_BLOCK, MOBA_BLOCK), jnp.float32),
        ],
        compiler_params=pltpu.CompilerParams(
            dimension_semantics=("arbitrary", "arbitrary"), vmem_limit_bytes=VMEM_LIMIT),
        name="moba_attention",
    )(qt, k, vt, onehot, kmean, gate_attn)


def _residual_copy(x_hbm, xring_ref, sem, step):
    slot = step % RESIDUAL_SLOTS
    return pltpu.make_async_copy(x_hbm.at[pl.ds(step * ROW_TILE, ROW_TILE), :], xring_ref.at[slot], sem.at[slot])


def _out_proj_kernel(apply_final_norm, x_hbm, attn_ref, pool_ref, wout_ref, fgain_ref, o_ref,
                     wbf_ref, xring_ref, sem):
    step = pl.program_id(0)
    n_steps = pl.num_programs(0)

    @pl.when(step == 0)
    def _():
        for first in range(RESIDUAL_SLOTS - 1):
            _residual_copy(x_hbm, xring_ref, sem, first).start()
        wbf_ref[...] = wout_ref[...].astype(wbf_ref.dtype)

    @pl.when(step + RESIDUAL_SLOTS - 1 < n_steps)
    def _():
        _residual_copy(x_hbm, xring_ref, sem, step + RESIDUAL_SLOTS - 1).start()

    _residual_copy(x_hbm, xring_ref, sem, step).wait()

    mixed = jnp.concatenate([attn_ref[...], pool_ref[...]], axis=1)
    y = xring_ref[step % RESIDUAL_SLOTS] + jnp.dot(mixed, wbf_ref[...], preferred_element_type=jnp.float32)
    if apply_final_norm:
        inv = lax.rsqrt(jnp.mean(y * y, axis=-1, keepdims=True) + EPS)
        y = y * inv * fgain_ref[...]
    o_ref[...] = y


def _out_proj(x2, attn, pool, w_out, final_gain, apply_final_norm):
    rows = x2.shape[0]
    assert rows // ROW_TILE >= RESIDUAL_SLOTS - 1
    row_spec = lambda w: pl.BlockSpec((ROW_TILE, w), lambda i: (i, 0))
    return pl.pallas_call(
        functools.partial(_out_proj_kernel, apply_final_norm),
        grid=(rows // ROW_TILE,),
        in_specs=[
            pl.BlockSpec(memory_space=pl.ANY),
            row_spec(ATTN_WIDTH), row_spec(POOL_WIDTH),
            pl.BlockSpec((D_MODEL, D_MODEL), lambda i: (0, 0), pipeline_mode=pl.Buffered(1)),
            pl.BlockSpec((1, D_MODEL), lambda i: (0, 0)),
        ],
        out_specs=row_spec(D_MODEL),
        out_shape=jax.ShapeDtypeStruct((rows, D_MODEL), jnp.float32),
        scratch_shapes=[pltpu.VMEM((D_MODEL, D_MODEL), jnp.bfloat16),
                        pltpu.VMEM((RESIDUAL_SLOTS, ROW_TILE, D_MODEL), jnp.float32),
                        pltpu.SemaphoreType.DMA((RESIDUAL_SLOTS,))],
        compiler_params=pltpu.CompilerParams(
            dimension_semantics=("arbitrary",), vmem_limit_bytes=VMEM_LIMIT),
        name="out_proj",
    )(x2, attn, pool, w_out, final_gain)


def _position_tables(seq):
    pos = np.arange(seq, dtype=np.float64)
    inv_freq = 1.0 / (ROPE_THETA ** (np.arange(0, HEAD_DIM, 2, dtype=np.float64) / HEAD_DIM))
    ang = pos[:, None] * inv_freq[None, :]
    cos, sin = np.cos(ang), np.sin(ang)
    cos_t = np.tile(cos, (1, LANES // (HEAD_DIM // 2)))
    sin_t = np.tile(np.concatenate([-sin, sin], axis=1), (1, HEADS_PER_TILE))
    onehot = np.arange(seq)[:, None] // MOBA_BLOCK == np.arange(LANES)[None, :]
    return (jnp.asarray(cos_t, jnp.float32), jnp.asarray(sin_t, jnp.float32),
            jnp.asarray(onehot, jnp.bfloat16))


def kernel(x, norm_gain, w_in, w_pool, b_pool, pool_scale, w_out, final_gain):
    batch, seq, d_model = x.shape
    depth = w_in.shape[0]
    assert d_model == D_MODEL and seq % ROW_TILE == 0 and 2 <= seq // MOBA_BLOCK <= LANES
    n_blocks = seq // MOBA_BLOCK
    cos_t, sin_t, onehot = _position_tables(seq)

    x2 = x.reshape(batch * seq, d_model)
    for l in range(depth):
        qt, k, vt, gate_attn, pool, kmean = _in_proj(
            x2, norm_gain[l][None, :], w_in[l], cos_t, sin_t, w_pool[l], b_pool[l][:, None, :],
            pool_scale[l][None, :], seq)
        shape3 = lambda t: t.reshape(batch, seq, t.shape[-1])
        per_block = lambda t: t.reshape(batch, n_blocks, HEAD_TILES, t.shape[-2], MOBA_BLOCK)
        attn = _moba_attention(per_block(qt), shape3(k), per_block(vt), onehot, kmean, shape3(gate_attn))
        x2 = _out_proj(x2, attn.reshape(batch * seq, ATTN_WIDTH), pool, w_out[l], final_gain[None, :],
                       apply_final_norm=(l == depth - 1))
    return x2.reshape(batch, seq, d_model)
```

```python
import functools

import numpy as np

import jax
import jax.numpy as jnp
from jax import lax
from jax.experimental import pallas as pl
from jax.experimental.pallas import tpu as pltpu

D_MODEL = 1024
ATTN_WIDTH = D_MODEL // 2
POOL_WIDTH = D_MODEL - ATTN_WIDTH
HEAD_DIM = 64
MOBA_BLOCK = 256
MOBA_TOPK = 3
POOL_WINDOWS = (2, 4, 8, 16)
POOL_GROUP = POOL_WIDTH // len(POOL_WINDOWS)
ROPE_THETA = 10000.0
EPS = 1e-6
IN_WIDTH = 4 * ATTN_WIDTH + 2 * POOL_WIDTH

LANES = 128
SUBLANES = 8
HEADS_PER_TILE = LANES // HEAD_DIM
HEAD_TILES = ATTN_WIDTH // LANES
ROW_TILE = 1024
RESIDUAL_SLOTS = 3
POOL_HALO = 16
MASK_BIAS = -1e30
KEY_UNROLL = 8
HALF_UNROLL = KEY_UNROLL // 2
WIDE_UNROLL = KEY_UNROLL * 2
BF16_ROWS = 16
VT_HEAD_ROWS = HEAD_DIM + BF16_ROWS
VT_ROWS = HEADS_PER_TILE * VT_HEAD_ROWS
Q_SCALE = HEAD_DIM ** -0.5 * 1.4426950408889634
VMEM_LIMIT = 56 * 1024 * 1024


def _silu(t):
    half = 0.5 * t
    return half + half * jnp.tanh(half)


def _rope(t, cos, sin_signed, first_half):
    swapped = jnp.where(first_half, pltpu.roll(t, LANES - HEAD_DIM // 2, axis=1),
                        pltpu.roll(t, HEAD_DIM // 2, axis=1))
    return t * cos + swapped * sin_signed


def _in_proj_kernel(seq_tiles, x_ref, gain_ref, w_ref, cos_ref, sin_ref, wpool_ref, bpool_ref, pscale_ref,
                    qt_ref, k_ref, vt_ref, ga_ref, pool_ref, kmean_ref, wbf_ref, ext_ref):
    @pl.when(pl.program_id(0) == 0)
    def _():
        wbf_ref[...] = w_ref[...].astype(wbf_ref.dtype)

    x = x_ref[...]
    inv = lax.rsqrt(jnp.mean(x * x, axis=-1, keepdims=True) + EPS)
    h = (x * inv * gain_ref[...]).astype(jnp.bfloat16)

    def proj(c):
        return jnp.dot(h, wbf_ref[:, c * ATTN_WIDTH:(c + 1) * ATTN_WIDTH],
                       preferred_element_type=jnp.float32)

    cos = cos_ref[...]
    sin = sin_ref[...]
    lane = lax.broadcasted_iota(jnp.int32, (ROW_TILE, LANES), 1)
    first_half = (lane % HEAD_DIM) < (HEAD_DIM // 2)

    q = proj(0)
    k = proj(1)
    v = proj(2)
    for t in range(HEAD_TILES):
        sl = slice(t * LANES, (t + 1) * LANES)
        qr = _rope(q[:, sl], cos, sin, first_half) * Q_SCALE
        kr = _rope(k[:, sl], cos, sin, first_half)
        k_ref[:, sl] = kr.astype(k_ref.dtype)
        for b in range(ROW_TILE // MOBA_BLOCK):
            blk_rows = slice(b * MOBA_BLOCK, (b + 1) * MOBA_BLOCK)
            kmean_ref[b, :, sl] = jnp.sum(kr[blk_rows], axis=0, keepdims=True) * (1.0 / MOBA_BLOCK)
            qt_ref[b, t] = qr[blk_rows].T.astype(qt_ref.dtype)
            v_t = v[blk_rows, sl].T.astype(vt_ref.dtype)
            for head in range(HEADS_PER_TILE):
                base = head * VT_HEAD_ROWS
                vt_ref[b, t, base:base + HEAD_DIM] = v_t[head * HEAD_DIM:(head + 1) * HEAD_DIM]
                vt_ref[b, t, base + HEAD_DIM:base + VT_HEAD_ROWS] = jnp.ones((BF16_ROWS, MOBA_BLOCK),
                                                                             vt_ref.dtype)
    ga_ref[...] = proj(3)

    tile_in_seq = pl.program_id(0) % seq_tiles

    @pl.when(pl.program_id(0) == 0)
    def _():
        ext_ref[...] = jnp.zeros_like(ext_ref)

    history = ext_ref[ROW_TILE:, :]
    ext_ref[:POOL_HALO, :] = jnp.where(tile_in_seq > 0, history, jnp.zeros_like(history))
    ext_ref[POOL_HALO:, :] = proj(4)
    pos = tile_in_seq * ROW_TILE + lax.broadcasted_iota(jnp.int32, (ROW_TILE, 1), 0)
    gate_pool = _silu(proj(5))
    for g, window in enumerate(POOL_WINDOWS):
        sl = slice(g * POOL_GROUP, (g + 1) * POOL_GROUP)
        total = ext_ref[:, sl]
        span = 1
        while span < window:
            total = total + pltpu.roll(total, span, axis=0)
            span *= 2
        total = total[POOL_HALO:]
        count = jnp.minimum(pos + 1, window).astype(jnp.float32)
        pooled = total / count - ext_ref[POOL_HALO:, sl]
        y = jnp.dot(pooled.astype(jnp.bfloat16), wpool_ref[g].astype(jnp.bfloat16),
                    preferred_element_type=jnp.float32)
        y = (y + bpool_ref[g]) * pscale_ref[:, sl]
        pool_ref[:, sl] = (y * gate_pool[:, sl]).astype(pool_ref.dtype)


def _in_proj(x2, gain, w_in, cos_t, sin_t, w_pool, b_pool, pool_scale, seq):
    rows = x2.shape[0]
    n_tiles = rows // ROW_TILE
    seq_tiles = seq // ROW_TILE
    row_spec = lambda w: pl.BlockSpec((ROW_TILE, w), lambda i: (i, 0))
    tab_spec = pl.BlockSpec((ROW_TILE, LANES), lambda i: (i % seq_tiles, 0))
    blocks_per_tile = ROW_TILE // MOBA_BLOCK
    n_blocks = rows // MOBA_BLOCK
    transposed = lambda r: jax.ShapeDtypeStruct((n_blocks, HEAD_TILES, r, MOBA_BLOCK), jnp.bfloat16)
    transposed_spec = lambda r: pl.BlockSpec((blocks_per_tile, HEAD_TILES, r, MOBA_BLOCK),
                                             lambda i: (i, 0, 0, 0))
    out_shape = (
        transposed(LANES),
        jax.ShapeDtypeStruct((rows, ATTN_WIDTH), jnp.bfloat16),
        transposed(VT_ROWS),
        jax.ShapeDtypeStruct((rows, ATTN_WIDTH), jnp.float32),
        jax.ShapeDtypeStruct((rows, POOL_WIDTH), jnp.bfloat16),
        jax.ShapeDtypeStruct((n_blocks, 1, ATTN_WIDTH), jnp.float32),
    )
    return pl.pallas_call(
        functools.partial(_in_proj_kernel, seq_tiles),
        grid=(n_tiles,),
        in_specs=[
            row_spec(D_MODEL),
            pl.BlockSpec((1, D_MODEL), lambda i: (0, 0)),
            pl.BlockSpec((D_MODEL, IN_WIDTH), lambda i: (0, 0), pipeline_mode=pl.Buffered(1)),
            tab_spec, tab_spec,
            pl.BlockSpec((len(POOL_WINDOWS), POOL_GROUP, POOL_GROUP), lambda i: (0, 0, 0)),
            pl.BlockSpec((len(POOL_WINDOWS), 1, POOL_GROUP), lambda i: (0, 0, 0)),
            pl.BlockSpec((1, POOL_WIDTH), lambda i: (0, 0)),
        ],
        out_specs=(
            transposed_spec(LANES), row_spec(ATTN_WIDTH), transposed_spec(VT_ROWS),
            row_spec(ATTN_WIDTH), row_spec(POOL_WIDTH),
            pl.BlockSpec((blocks_per_tile, 1, ATTN_WIDTH), lambda i: (i, 0, 0)),
        ),
        out_shape=out_shape,
        scratch_shapes=[pltpu.VMEM((D_MODEL, IN_WIDTH), jnp.bfloat16),
                        pltpu.VMEM((POOL_HALO + ROW_TILE, POOL_WIDTH), jnp.float32)],
        compiler_params=pltpu.CompilerParams(
            dimension_semantics=("arbitrary",), vmem_limit_bytes=VMEM_LIMIT),
        name="in_proj",
    )(x2, gain, w_in, cos_t, sin_t, w_pool, b_pool, pool_scale)


def _select_bias_t(gate_t, n_past):
    slot = lax.broadcasted_iota(jnp.int32, gate_t.shape, 0)
    slot_f = slot.astype(jnp.float32)
    neg_inf = jnp.float32(-jnp.inf)
    g = jnp.where(slot < n_past, gate_t, neg_inf)
    bias = jnp.full(gate_t.shape, MASK_BIAS, dtype=jnp.float32)
    for _ in range(MOBA_TOPK):
        best = jnp.max(g, axis=0, keepdims=True)
        first = jnp.min(jnp.where(g == best, slot_f, float(gate_t.shape[0])), axis=0, keepdims=True)
        pick = (slot_f == first) & (best > neg_inf)
        bias = jnp.where(pick, 0.0, bias)
        g = jnp.where(pick, neg_inf, g)
    return bias


def _visited_tiles(n_past):
    rest = n_past % KEY_UNROLL
    whole = n_past - rest
    if isinstance(n_past, int):
        return whole + (0 if rest == 0 else HALF_UNROLL if rest <= HALF_UNROLL else KEY_UNROLL)
    return whole + jnp.where(rest == 0, 0, jnp.where(rest <= HALF_UNROLL, HALF_UNROLL, KEY_UNROLL))


def _fold_rows(t, op):
    return op(t.reshape(t.shape[0] // SUBLANES, SUBLANES, t.shape[1]), axis=0)


def _moba_kernel(qt_ref, k_ref, vt_ref, onehot_ref, kmean_ref, ga_ref, o_ref, qaug_ref, s_ref, own_ref):
    blk = MOBA_BLOCK
    n_blocks = kmean_ref.shape[0]
    heads = range(HEADS_PER_TILE)

    feat = lax.broadcasted_iota(jnp.int32, (LANES, blk), 0)
    km = kmean_ref[:, 0, :]
    km_hi = km.astype(jnp.bfloat16)
    km_lo = (km - km_hi.astype(jnp.float32)).astype(jnp.bfloat16)
    key_pos = lax.broadcasted_iota(jnp.int32, (blk, blk), 0)
    qry_pos = lax.broadcasted_iota(jnp.int32, (blk, blk), 1)
    causal = key_pos <= qry_pos

    def block_rows(j):
        return pl.ds(pl.multiple_of(j * blk, blk), blk)

    def prepare(j):
        qt = qt_ref[j].astype(jnp.float32)
        k_own = k_ref[block_rows(j), :]
        m_init = []
        for h in heads:
            qh = jnp.where((feat // HEAD_DIM) == h, qt, 0.0).astype(jnp.bfloat16)
            gate_t = (jnp.dot(km_hi, qh, preferred_element_type=jnp.float32)
                      + jnp.dot(km_lo, qh, preferred_element_type=jnp.float32))
            qaug_ref[h, :LANES, :] = qh
            qaug_ref[h, LANES:LANES + n_blocks, :] = _select_bias_t(gate_t, j).astype(jnp.bfloat16)
            qaug_ref[h, LANES + n_blocks:, :] = jnp.zeros((LANES - n_blocks, blk), jnp.bfloat16)
            s = jnp.dot(k_own, qh, preferred_element_type=jnp.float32)
            s = jnp.where(causal, s, -jnp.inf)
            own_ref[h] = s
            m_init.append(_fold_rows(s, jnp.max))
        return tuple(m_init)

    def probs(s, m_fin):
        return jnp.exp2(s - m_fin).astype(jnp.bfloat16)

    def weighted_values(p, n, h):
        v_aug = vt_ref[n, h * VT_HEAD_ROWS:(h + 1) * VT_HEAD_ROWS, :]
        r = jnp.dot(v_aug, p, preferred_element_type=jnp.float32)
        return r[:HEAD_DIM], r[HEAD_DIM:HEAD_DIM + SUBLANES]

    def own_probs(m_fin):
        return tuple(probs(own_ref[h], m_fin[h]) for h in heads)

    def own_values(j, p_own):
        return tuple(weighted_values(p_own[h], j, h) for h in heads)

    def column_max(m_run):
        return tuple(jnp.max(m, axis=0, keepdims=True) for m in m_run)

    def emit(j, acc):
        out_t = jnp.concatenate([o / l[:1] for o, l in acc], axis=0)
        rows = block_rows(j)
        o_ref[rows, :] = (out_t.T * _silu(ga_ref[rows, :])).astype(o_ref.dtype)

    def group_body(finish_old, score_new, m_prev, unroll=KEY_UNROLL, first=0):
        def body(g, carry):
            m_run, acc = [list(c) for c in carry]
            for i in range(unroll):
                n = first + g * unroll + i
                if score_new:
                    k_aug = jnp.concatenate([k_ref[block_rows(n), :], onehot_ref[block_rows(n), :]], axis=1)
                for h in heads:
                    if finish_old:
                        o, l = weighted_values(probs(s_ref[h, n], m_prev[h]), n, h)
                        acc[h] = (acc[h][0] + o, acc[h][1] + l)
                    if score_new:
                        s = jnp.dot(k_aug, qaug_ref[h], preferred_element_type=jnp.float32)
                        s_ref[h, n] = s
                        m_run[h] = jnp.maximum(m_run[h], _fold_rows(s, jnp.max))
            return tuple(m_run), tuple(acc)
        return body

    m_fin0 = column_max(prepare(0))
    acc0 = own_values(0, own_probs(m_fin0))
    m_init1 = prepare(1)

    def step(t, carry):
        m_init, acc, m_prev = carry
        old_tiles = _visited_tiles(t - 1)
        wide = old_tiles // WIDE_UNROLL
        done = wide * WIDE_UNROLL
        whole = (old_tiles - done) // KEY_UNROLL
        half = (old_tiles - done - whole * KEY_UNROLL) // HALF_UNROLL
        extra = (_visited_tiles(t) - old_tiles) // HALF_UNROLL
        inner = (m_init, acc)
        inner = lax.fori_loop(0, wide, group_body(True, True, m_prev, WIDE_UNROLL), inner)
        inner = lax.fori_loop(0, whole, group_body(True, True, m_prev, KEY_UNROLL, done), inner)
        inner = lax.fori_loop(0, half, group_body(True, True, m_prev, HALF_UNROLL,
                                                  done + whole * KEY_UNROLL), inner)
        inner = lax.fori_loop(0, extra, group_body(False, True, m_prev, HALF_UNROLL, old_tiles), inner)
        m_run, acc = inner
        m_fin = column_max(m_run)
        emit(t - 1, acc)
        p_own = own_probs(m_fin)
        m_next = prepare(jnp.minimum(t + 1, n_blocks - 1))
        acc_next = own_values(t, p_own)
        return m_next, acc_next, m_fin

    _, acc, m_prev = lax.fori_loop(1, n_blocks, step, (m_init1, acc0, m_fin0))

    last = n_blocks - 1
    last_tiles = _visited_tiles(last)
    whole = last_tiles // KEY_UNROLL
    unused_max = tuple(jnp.zeros((SUBLANES, blk), jnp.float32) for _ in heads)
    inner = lax.fori_loop(0, whole, group_body(True, False, m_prev), (unused_max, acc))
    if last_tiles % KEY_UNROLL:
        inner = group_body(True, False, m_prev, HALF_UNROLL, whole * KEY_UNROLL)(0, inner)
    emit(last, inner[1])


def _moba_attention(qt, k, vt, onehot, kmean, gate_attn):
    batch, seq, _ = k.shape
    n_blocks = seq // MOBA_BLOCK
    slots = _visited_tiles(n_blocks - 1)
    per_block = lambda r: pl.BlockSpec((None, n_blocks, None, r, MOBA_BLOCK), lambda b, hp: (b, 0, hp, 0, 0))
    per_row = pl.BlockSpec((None, seq, LANES), lambda b, hp: (b, 0, hp))
    return pl.pallas_call(
        _moba_kernel,
        grid=(batch, HEAD_TILES),
        in_specs=[
            per_block(LANES),
            per_row,
            per_block(VT_ROWS),
            pl.BlockSpec((seq, LANES), lambda b, hp: (0, 0), pipeline_mode=pl.Buffered(1)),
            pl.BlockSpec((n_blocks, 1, LANES), lambda b, hp: (b, 0, hp)),
            per_row,
        ],
        out_specs=per_row,
        out_shape=jax.ShapeDtypeStruct((batch, seq, ATTN_WIDTH), jnp.bfloat16),
        scratch_shapes=[
            pltpu.VMEM((HEADS_PER_TILE, 2 * LANES, MOBA_BLOCK), jnp.bfloat16),
            pltpu.VMEM((HEADS_PER_TILE, slots, MOBA_BLOCK, MOBA_BLOCK), jnp.float32),
            pltpu.VMEM((HEADS_PER_TILE, MOBA_BLOCK, MOBA_BLOCK), jnp.float32),
        ],
        compiler_params=pltpu.CompilerParams(
            dimension_semantics=("arbitrary", "arbitrary"), vmem_limit_bytes=VMEM_LIMIT),
        name="moba_attention",
    )(qt, k, vt, onehot, kmean, gate_attn)


def _tile_copies(streams, sem, step):
    slot = step % RESIDUAL_SLOTS
    return [pltpu.make_async_copy(hbm.at[pl.ds(step * ROW_TILE, ROW_TILE), :], ring.at[slot], sem.at[i, slot])
            for i, (hbm, ring) in enumerate(streams)]


def _out_proj_kernel(apply_final_norm, x_hbm, attn_hbm, pool_hbm, wout_ref, fgain_ref, o_ref,
                     wbf_ref, xring_ref, aring_ref, pring_ref, sem):
    step = pl.program_id(0)
    n_steps = pl.num_programs(0)
    streams = ((x_hbm, xring_ref), (attn_hbm, aring_ref), (pool_hbm, pring_ref))

    @pl.when(step == 0)
    def _():
        for first in range(RESIDUAL_SLOTS - 1):
            for copy in _tile_copies(streams, sem, first):
                copy.start()
        wbf_ref[...] = wout_ref[...].astype(wbf_ref.dtype)

    @pl.when(step + RESIDUAL_SLOTS - 1 < n_steps)
    def _():
        for copy in _tile_copies(streams, sem, step + RESIDUAL_SLOTS - 1):
            copy.start()

    for copy in _tile_copies(streams, sem, step):
        copy.wait()

    slot = step % RESIDUAL_SLOTS
    mixed = jnp.concatenate([aring_ref[slot], pring_ref[slot]], axis=1)
    y = xring_ref[slot] + jnp.dot(mixed, wbf_ref[...], preferred_element_type=jnp.float32)
    if apply_final_norm:
        inv = lax.rsqrt(jnp.mean(y * y, axis=-1, keepdims=True) + EPS)
        y = y * inv * fgain_ref[...]
    o_ref[...] = y


def _out_proj(x2, attn, pool, w_out, final_gain, apply_final_norm):
    rows = x2.shape[0]
    assert rows // ROW_TILE >= RESIDUAL_SLOTS - 1
    in_hbm = pl.BlockSpec(memory_space=pl.ANY)
    ring = lambda w, dtype: pltpu.VMEM((RESIDUAL_SLOTS, ROW_TILE, w), dtype)
    return pl.pallas_call(
        functools.partial(_out_proj_kernel, apply_final_norm),
        grid=(rows // ROW_TILE,),
        in_specs=[
            in_hbm, in_hbm, in_hbm,
            pl.BlockSpec((D_MODEL, D_MODEL), lambda i: (0, 0), pipeline_mode=pl.Buffered(1)),
            pl.BlockSpec((1, D_MODEL), lambda i: (0, 0)),
        ],
        out_specs=pl.BlockSpec((ROW_TILE, D_MODEL), lambda i: (i, 0)),
        out_shape=jax.ShapeDtypeStruct((rows, D_MODEL), jnp.float32),
        scratch_shapes=[pltpu.VMEM((D_MODEL, D_MODEL), jnp.bfloat16),
                        ring(D_MODEL, jnp.float32), ring(ATTN_WIDTH, attn.dtype), ring(POOL_WIDTH, pool.dtype),
                        pltpu.SemaphoreType.DMA((3, RESIDUAL_SLOTS))],
        compiler_params=pltpu.CompilerParams(
            dimension_semantics=("arbitrary",), vmem_limit_bytes=VMEM_LIMIT),
        name="out_proj",
    )(x2, attn, pool, w_out, final_gain)


def _position_tables(seq):
    pos = np.arange(seq, dtype=np.float64)
    inv_freq = 1.0 / (ROPE_THETA ** (np.arange(0, HEAD_DIM, 2, dtype=np.float64) / HEAD_DIM))
    ang = pos[:, None] * inv_freq[None, :]
    cos, sin = np.cos(ang), np.sin(ang)
    cos_t = np.tile(cos, (1, LANES // (HEAD_DIM // 2)))
    sin_t = np.tile(np.concatenate([-sin, sin], axis=1), (1, HEADS_PER_TILE))
    onehot = np.arange(seq)[:, None] // MOBA_BLOCK == np.arange(LANES)[None, :]
    return (jnp.asarray(cos_t, jnp.float32), jnp.asarray(sin_t, jnp.float32),
            jnp.asarray(onehot, jnp.bfloat16))


def kernel(x, norm_gain, w_in, w_pool, b_pool, pool_scale, w_out, final_gain):
    batch, seq, d_model = x.shape
    depth = w_in.shape[0]
    assert d_model == D_MODEL and seq % ROW_TILE == 0 and 2 <= seq // MOBA_BLOCK <= LANES
    n_blocks = seq // MOBA_BLOCK
    cos_t, sin_t, onehot = _position_tables(seq)

    x2 = x.reshape(batch * seq, d_model)
    for l in range(depth):
        qt, k, vt, gate_attn, pool, kmean = _in_proj(
            x2, norm_gain[l][None, :], w_in[l], cos_t, sin_t, w_pool[l], b_pool[l][:, None, :],
            pool_scale[l][None, :], seq)
        shape3 = lambda t: t.reshape(batch, seq, t.shape[-1])
        per_block = lambda t: t.reshape(batch, n_blocks, HEAD_TILES, t.shape[-2], MOBA_BLOCK)
        attn = _moba_attention(per_block(qt), shape3(k), per_block(vt), onehot, kmean, shape3(gate_attn))
        x2 = _out_proj(x2, attn.reshape(batch * seq, ATTN_WIDTH), pool, w_out[l], final_gain[None, :],
                       apply_final_norm=(l == depth - 1))
    return x2.reshape(batch, seq, d_model)
```

```python
import functools

import numpy as np

import jax
import jax.numpy as jnp
from jax import lax
from jax.experimental import pallas as pl
from jax.experimental.pallas import tpu as pltpu

D_MODEL = 1024
ATTN_WIDTH = D_MODEL // 2
POOL_WIDTH = D_MODEL - ATTN_WIDTH
HEAD_DIM = 64
MOBA_BLOCK = 256
MOBA_TOPK = 3
POOL_WINDOWS = (2, 4, 8, 16)
POOL_GROUP = POOL_WIDTH // len(POOL_WINDOWS)
ROPE_THETA = 10000.0
EPS = 1e-6
IN_WIDTH = 4 * ATTN_WIDTH + 2 * POOL_WIDTH

LANES = 128
SUBLANES = 8
HEADS_PER_TILE = LANES // HEAD_DIM
HEAD_TILES = ATTN_WIDTH // LANES
ROW_TILE = 1024
RESIDUAL_SLOTS = 4
POOL_HALO = 16
MASK_BIAS = -1e30
KEY_UNROLL = 8
HALF_UNROLL = KEY_UNROLL // 2
WIDE_UNROLL = KEY_UNROLL * 2
BF16_ROWS = 16
VT_HEAD_ROWS = HEAD_DIM + BF16_ROWS
VT_ROWS = HEADS_PER_TILE * VT_HEAD_ROWS
Q_SCALE = HEAD_DIM ** -0.5 * 1.4426950408889634
VMEM_LIMIT = 56 * 1024 * 1024


def _silu(t):
    half = 0.5 * t
    return half + half * jnp.tanh(half)


def _rope(t, cos, sin_signed, first_half):
    swapped = jnp.where(first_half, pltpu.roll(t, LANES - HEAD_DIM // 2, axis=1),
                        pltpu.roll(t, HEAD_DIM // 2, axis=1))
    return t * cos + swapped * sin_signed


def _in_proj_kernel(seq_tiles, x_ref, gain_ref, w_ref, cos_ref, sin_ref, wpool_ref, bpool_ref, pscale_ref,
                    qt_ref, k_ref, vt_ref, ga_ref, pool_ref, kmean_ref, wbf_ref, ext_ref):
    @pl.when(pl.program_id(0) == 0)
    def _():
        wbf_ref[...] = w_ref[...].astype(wbf_ref.dtype)

    x = x_ref[...]
    inv = lax.rsqrt(jnp.mean(x * x, axis=-1, keepdims=True) + EPS)
    h = (x * inv * gain_ref[...]).astype(jnp.bfloat16)

    def proj(c):
        return jnp.dot(h, wbf_ref[:, c * ATTN_WIDTH:(c + 1) * ATTN_WIDTH],
                       preferred_element_type=jnp.float32)

    cos = cos_ref[...]
    sin = sin_ref[...]
    lane = lax.broadcasted_iota(jnp.int32, (ROW_TILE, LANES), 1)
    first_half = (lane % HEAD_DIM) < (HEAD_DIM // 2)

    q = proj(0)
    k = proj(1)
    v = proj(2)
    for t in range(HEAD_TILES):
        sl = slice(t * LANES, (t + 1) * LANES)
        qr = _rope(q[:, sl], cos, sin, first_half) * Q_SCALE
        kr = _rope(k[:, sl], cos, sin, first_half)
        k_ref[:, sl] = kr.astype(k_ref.dtype)
        for b in range(ROW_TILE // MOBA_BLOCK):
            blk_rows = slice(b * MOBA_BLOCK, (b + 1) * MOBA_BLOCK)
            kmean_ref[b, :, sl] = jnp.sum(kr[blk_rows], axis=0, keepdims=True) * (1.0 / MOBA_BLOCK)
            qt_ref[b, t] = qr[blk_rows].T.astype(qt_ref.dtype)
            v_t = v[blk_rows, sl].T.astype(vt_ref.dtype)
            for head in range(HEADS_PER_TILE):
                base = head * VT_HEAD_ROWS
                vt_ref[b, t, base:base + HEAD_DIM] = v_t[head * HEAD_DIM:(head + 1) * HEAD_DIM]
                vt_ref[b, t, base + HEAD_DIM:base + VT_HEAD_ROWS] = jnp.ones((BF16_ROWS, MOBA_BLOCK),
                                                                             vt_ref.dtype)
    ga_ref[...] = proj(3)

    tile_in_seq = pl.program_id(0) % seq_tiles

    @pl.when(pl.program_id(0) == 0)
    def _():
        ext_ref[...] = jnp.zeros_like(ext_ref)

    history = ext_ref[ROW_TILE:, :]
    ext_ref[:POOL_HALO, :] = jnp.where(tile_in_seq > 0, history, jnp.zeros_like(history))
    ext_ref[POOL_HALO:, :] = proj(4)
    pos = tile_in_seq * ROW_TILE + lax.broadcasted_iota(jnp.int32, (ROW_TILE, 1), 0)
    gate_pool = _silu(proj(5))
    for g, window in enumerate(POOL_WINDOWS):
        sl = slice(g * POOL_GROUP, (g + 1) * POOL_GROUP)
        total = ext_ref[:, sl]
        span = 1
        while span < window:
            total = total + pltpu.roll(total, span, axis=0)
            span *= 2
        total = total[POOL_HALO:]
        count = jnp.minimum(pos + 1, window).astype(jnp.float32)
        pooled = total / count - ext_ref[POOL_HALO:, sl]
        y = jnp.dot(pooled.astype(jnp.bfloat16), wpool_ref[g].astype(jnp.bfloat16),
                    preferred_element_type=jnp.float32)
        y = (y + bpool_ref[g]) * pscale_ref[:, sl]
        pool_ref[:, sl] = (y * gate_pool[:, sl]).astype(pool_ref.dtype)


def _in_proj(x2, gain, w_in, cos_t, sin_t, w_pool, b_pool, pool_scale, seq):
    rows = x2.shape[0]
    n_tiles = rows // ROW_TILE
    seq_tiles = seq // ROW_TILE
    row_spec = lambda w: pl.BlockSpec((ROW_TILE, w), lambda i: (i, 0))
    tab_spec = pl.BlockSpec((ROW_TILE, LANES), lambda i: (i % seq_tiles, 0))
    blocks_per_tile = ROW_TILE // MOBA_BLOCK
    n_blocks = rows // MOBA_BLOCK
    transposed = lambda r: jax.ShapeDtypeStruct((n_blocks, HEAD_TILES, r, MOBA_BLOCK), jnp.bfloat16)
    transposed_spec = lambda r: pl.BlockSpec((blocks_per_tile, HEAD_TILES, r, MOBA_BLOCK),
                                             lambda i: (i, 0, 0, 0))
    out_shape = (
        transposed(LANES),
        jax.ShapeDtypeStruct((rows, ATTN_WIDTH), jnp.bfloat16),
        transposed(VT_ROWS),
        jax.ShapeDtypeStruct((rows, ATTN_WIDTH), jnp.float32),
        jax.ShapeDtypeStruct((rows, POOL_WIDTH), jnp.bfloat16),
        jax.ShapeDtypeStruct((n_blocks, 1, ATTN_WIDTH), jnp.float32),
    )
    return pl.pallas_call(
        functools.partial(_in_proj_kernel, seq_tiles),
        grid=(n_tiles,),
        in_specs=[
            row_spec(D_MODEL),
            pl.BlockSpec((1, D_MODEL), lambda i: (0, 0)),
            pl.BlockSpec((D_MODEL, IN_WIDTH), lambda i: (0, 0), pipeline_mode=pl.Buffered(1)),
            tab_spec, tab_spec,
            pl.BlockSpec((len(POOL_WINDOWS), POOL_GROUP, POOL_GROUP), lambda i: (0, 0, 0)),
            pl.BlockSpec((len(POOL_WINDOWS), 1, POOL_GROUP), lambda i: (0, 0, 0)),
            pl.BlockSpec((1, POOL_WIDTH), lambda i: (0, 0)),
        ],
        out_specs=(
            transposed_spec(LANES), row_spec(ATTN_WIDTH), transposed_spec(VT_ROWS),
            row_spec(ATTN_WIDTH), row_spec(POOL_WIDTH),
            pl.BlockSpec((blocks_per_tile, 1, ATTN_WIDTH), lambda i: (i, 0, 0)),
        ),
        out_shape=out_shape,
        scratch_shapes=[pltpu.VMEM((D_MODEL, IN_WIDTH), jnp.bfloat16),
                        pltpu.VMEM((POOL_HALO + ROW_TILE, POOL_WIDTH), jnp.float32)],
        compiler_params=pltpu.CompilerParams(
            dimension_semantics=("arbitrary",), vmem_limit_bytes=VMEM_LIMIT),
        name="in_proj",
    )(x2, gain, w_in, cos_t, sin_t, w_pool, b_pool, pool_scale)


def _select_bias_t(gate_t, n_past):
    slot = lax.broadcasted_iota(jnp.int32, gate_t.shape, 0)
    slot_f = slot.astype(jnp.float32)
    neg_inf = jnp.float32(-jnp.inf)
    g = jnp.where(slot < n_past, gate_t, neg_inf)
    bias = jnp.full(gate_t.shape, MASK_BIAS, dtype=jnp.float32)
    for _ in range(MOBA_TOPK):
        best = jnp.max(g, axis=0, keepdims=True)
        first = jnp.min(jnp.where(g == best, slot_f, float(gate_t.shape[0])), axis=0, keepdims=True)
        pick = (slot_f == first) & (best > neg_inf)
        bias = jnp.where(pick, 0.0, bias)
        g = jnp.where(pick, neg_inf, g)
    return bias


def _visited_tiles(n_past):
    rest = n_past % KEY_UNROLL
    whole = n_past - rest
    if isinstance(n_past, int):
        return whole + (0 if rest == 0 else HALF_UNROLL if rest <= HALF_UNROLL else KEY_UNROLL)
    return whole + jnp.where(rest == 0, 0, jnp.where(rest <= HALF_UNROLL, HALF_UNROLL, KEY_UNROLL))


def _fold_rows(t, op):
    return op(t.reshape(t.shape[0] // SUBLANES, SUBLANES, t.shape[1]), axis=0)


def _moba_kernel(qt_ref, k_ref, vt_ref, onehot_ref, kmean_ref, ga_ref, o_ref, qaug_ref, s_ref, own_ref):
    blk = MOBA_BLOCK
    n_blocks = kmean_ref.shape[0]
    heads = range(HEADS_PER_TILE)

    feat = lax.broadcasted_iota(jnp.int32, (LANES, blk), 0)
    km = kmean_ref[:, 0, :]
    km_hi = km.astype(jnp.bfloat16)
    km_lo = (km - km_hi.astype(jnp.float32)).astype(jnp.bfloat16)
    key_pos = lax.broadcasted_iota(jnp.int32, (blk, blk), 0)
    qry_pos = lax.broadcasted_iota(jnp.int32, (blk, blk), 1)
    causal = key_pos <= qry_pos

    def block_rows(j):
        return pl.ds(pl.multiple_of(j * blk, blk), blk)

    def prepare(j):
        qt = qt_ref[j].astype(jnp.float32)
        k_own = k_ref[block_rows(j), :]
        m_init = []
        for h in heads:
            qh = jnp.where((feat // HEAD_DIM) == h, qt, 0.0).astype(jnp.bfloat16)
            gate_t = (jnp.dot(km_hi, qh, preferred_element_type=jnp.float32)
                      + jnp.dot(km_lo, qh, preferred_element_type=jnp.float32))
            qaug_ref[h, :LANES, :] = qh
            qaug_ref[h, LANES:LANES + n_blocks, :] = _select_bias_t(gate_t, j).astype(jnp.bfloat16)
            qaug_ref[h, LANES + n_blocks:, :] = jnp.zeros((LANES - n_blocks, blk), jnp.bfloat16)
            s = jnp.dot(k_own, qh, preferred_element_type=jnp.float32)
            s = jnp.where(causal, s, -jnp.inf)
            own_ref[h] = s
            m_init.append(_fold_rows(s, jnp.max))
        return tuple(m_init)

    def probs(s, m_fin):
        return jnp.exp2(s - m_fin).astype(jnp.bfloat16)

    def weighted_values(p, n, h):
        v_aug = vt_ref[n, h * VT_HEAD_ROWS:(h + 1) * VT_HEAD_ROWS, :]
        r = jnp.dot(v_aug, p, preferred_element_type=jnp.float32)
        return r[:HEAD_DIM], r[HEAD_DIM:HEAD_DIM + SUBLANES]

    def own_probs(m_fin):
        return tuple(probs(own_ref[h], m_fin[h]) for h in heads)

    def own_values(j, p_own):
        return tuple(weighted_values(p_own[h], j, h) for h in heads)

    def column_max(m_run):
        return tuple(jnp.max(m, axis=0, keepdims=True) for m in m_run)

    def emit(j, acc):
        out_t = jnp.concatenate([o / l[:1] for o, l in acc], axis=0)
        rows = block_rows(j)
        o_ref[rows, :] = (out_t.T * _silu(ga_ref[rows, :])).astype(o_ref.dtype)

    def group_body(finish_old, score_new, m_prev, unroll=KEY_UNROLL, first=0):
        def body(g, carry):
            m_run, acc = [list(c) for c in carry]
            for i in range(unroll):
                n = first + g * unroll + i
                if score_new:
                    k_aug = jnp.concatenate([k_ref[block_rows(n), :], onehot_ref[block_rows(n), :]], axis=1)
                for h in heads:
                    if finish_old:
                        o, l = weighted_values(probs(s_ref[h, n], m_prev[h]), n, h)
                        acc[h] = (acc[h][0] + o, acc[h][1] + l)
                    if score_new:
                        s = jnp.dot(k_aug, qaug_ref[h], preferred_element_type=jnp.float32)
                        s_ref[h, n] = s
                        m_run[h] = jnp.maximum(m_run[h], _fold_rows(s, jnp.max))
            return tuple(m_run), tuple(acc)
        return body

    m_fin0 = column_max(prepare(0))
    acc0 = own_values(0, own_probs(m_fin0))
    m_init1 = prepare(1)

    def step(t, carry):
        m_init, acc, m_prev = carry
        old_tiles = _visited_tiles(t - 1)
        wide = old_tiles // WIDE_UNROLL
        done = wide * WIDE_UNROLL
        whole = (old_tiles - done) // KEY_UNROLL
        half = (old_tiles - done - whole * KEY_UNROLL) // HALF_UNROLL
        extra = (_visited_tiles(t) - old_tiles) // HALF_UNROLL
        inner = (m_init, acc)
        inner = lax.fori_loop(0, wide, group_body(True, True, m_prev, WIDE_UNROLL), inner)
        inner = lax.fori_loop(0, whole, group_body(True, True, m_prev, KEY_UNROLL, done), inner)
        inner = lax.fori_loop(0, half, group_body(True, True, m_prev, HALF_UNROLL,
                                                  done + whole * KEY_UNROLL), inner)
        inner = lax.fori_loop(0, extra, group_body(False, True, m_prev, HALF_UNROLL, old_tiles), inner)
        m_run, acc = inner
        m_fin = column_max(m_run)
        emit(t - 1, acc)
        p_own = own_probs(m_fin)
        m_next = prepare(jnp.minimum(t + 1, n_blocks - 1))
        acc_next = own_values(t, p_own)
        return m_next, acc_next, m_fin

    _, acc, m_prev = lax.fori_loop(1, n_blocks, step, (m_init1, acc0, m_fin0))

    last = n_blocks - 1
    last_tiles = _visited_tiles(last)
    whole = last_tiles // KEY_UNROLL
    unused_max = tuple(jnp.zeros((SUBLANES, blk), jnp.float32) for _ in heads)
    inner = lax.fori_loop(0, whole, group_body(True, False, m_prev), (unused_max, acc))
    if last_tiles % KEY_UNROLL:
        inner = group_body(True, False, m_prev, HALF_UNROLL, whole * KEY_UNROLL)(0, inner)
    emit(last, inner[1])


def _moba_attention(qt, k, vt, onehot, kmean, gate_attn):
    batch, seq, _ = k.shape
    n_blocks = seq // MOBA_BLOCK
    slots = _visited_tiles(n_blocks - 1)
    per_block = lambda r: pl.BlockSpec((None, n_blocks, None, r, MOBA_BLOCK), lambda b, hp: (b, 0, hp, 0, 0))
    per_row = pl.BlockSpec((None, seq, LANES), lambda b, hp: (b, 0, hp))
    return pl.pallas_call(
        _moba_kernel,
        grid=(batch, HEAD_TILES),
        in_specs=[
            per_block(LANES),
            per_row,
            per_block(VT_ROWS),
            pl.BlockSpec((seq, LANES), lambda b, hp: (0, 0), pipeline_mode=pl.Buffered(1)),
            pl.BlockSpec((n_blocks, 1, LANES), lambda b, hp: (b, 0, hp)),
            per_row,
        ],
        out_specs=per_row,
        out_shape=jax.ShapeDtypeStruct((batch, seq, ATTN_WIDTH), jnp.bfloat16),
        scratch_shapes=[
            pltpu.VMEM((HEADS_PER_TILE, 2 * LANES, MOBA_BLOCK), jnp.bfloat16),
            pltpu.VMEM((HEADS_PER_TILE, slots, MOBA_BLOCK, MOBA_BLOCK), jnp.float32),
            pltpu.VMEM((HEADS_PER_TILE, MOBA_BLOCK, MOBA_BLOCK), jnp.float32),
        ],
        compiler_params=pltpu.CompilerParams(
            dimension_semantics=("arbitrary", "arbitrary"), vmem_limit_bytes=VMEM_LIMIT),
        name="moba_attention",
    )(qt, k, vt, onehot, kmean, gate_attn)


def _residual_copy(x_hbm, xring_ref, sem, step):
    slot = step % RESIDUAL_SLOTS
    return pltpu.make_async_copy(x_hbm.at[pl.ds(step * ROW_TILE, ROW_TILE), :], xring_ref.at[slot], sem.at[slot])


def _out_proj_kernel(apply_final_norm, x_hbm, attn_ref, pool_ref, wout_ref, fgain_ref, o_ref,
                     wbf_ref, xring_ref, sem):
    step = pl.program_id(0)
    n_steps = pl.num_programs(0)

    @pl.when(step == 0)
    def _():
        for first in range(RESIDUAL_SLOTS - 1):
            _residual_copy(x_hbm, xring_ref, sem, first).start()
        wbf_ref[...] = wout_ref[...].astype(wbf_ref.dtype)

    @pl.when(step + RESIDUAL_SLOTS - 1 < n_steps)
    def _():
        _residual_copy(x_hbm, xring_ref, sem, step + RESIDUAL_SLOTS - 1).start()

    _residual_copy(x_hbm, xring_ref, sem, step).wait()

    mixed = jnp.concatenate([attn_ref[...], pool_ref[...]], axis=1)
    y = xring_ref[step % RESIDUAL_SLOTS] + jnp.dot(mixed, wbf_ref[...], preferred_element_type=jnp.float32)
    if apply_final_norm:
        inv = lax.rsqrt(jnp.mean(y * y, axis=-1, keepdims=True) + EPS)
        y = y * inv * fgain_ref[...]
    o_ref[...] = y


def _out_proj(x2, attn, pool, w_out, final_gain, apply_final_norm):
    rows = x2.shape[0]
    assert rows // ROW_TILE >= RESIDUAL_SLOTS - 1
    row_spec = lambda w: pl.BlockSpec((ROW_TILE, w), lambda i: (i, 0))
    return pl.pallas_call(
        functools.partial(_out_proj_kernel, apply_final_norm),
        grid=(rows // ROW_TILE,),
        in_specs=[
            pl.BlockSpec(memory_space=pl.ANY),
            row_spec(ATTN_WIDTH), row_spec(POOL_WIDTH),
            pl.BlockSpec((D_MODEL, D_MODEL), lambda i: (0, 0), pipeline_mode=pl.Buffered(1)),
            pl.BlockSpec((1, D_MODEL), lambda i: (0, 0)),
        ],
        out_specs=row_spec(D_MODEL),
        out_shape=jax.ShapeDtypeStruct((rows, D_MODEL), jnp.float32),
        scratch_shapes=[pltpu.VMEM((D_MODEL, D_MODEL), jnp.bfloat16),
                        pltpu.VMEM((RESIDUAL_SLOTS, ROW_TILE, D_MODEL), jnp.float32),
                        pltpu.SemaphoreType.DMA((RESIDUAL_SLOTS,))],
        compiler_params=pltpu.CompilerParams(
            dimension_semantics=("arbitrary",), vmem_limit_bytes=VMEM_LIMIT),
        name="out_proj",
    )(x2, attn, pool, w_out, final_gain)


def _position_tables(seq):
    pos = np.arange(seq, dtype=np.float64)
    inv_freq = 1.0 / (ROPE_THETA ** (np.arange(0, HEAD_DIM, 2, dtype=np.float64) / HEAD_DIM))
    ang = pos[:, None] * inv_freq[None, :]
    cos, sin = np.cos(ang), np.sin(ang)
    cos_t = np.tile(cos, (1, LANES // (HEAD_DIM // 2)))
    sin_t = np.tile(np.concatenate([-sin, sin], axis=1), (1, HEADS_PER_TILE))
    onehot = np.arange(seq)[:, None] // MOBA_BLOCK == np.arange(LANES)[None, :]
    return (jnp.asarray(cos_t, jnp.float32), jnp.asarray(sin_t, jnp.float32),
            jnp.asarray(onehot, jnp.bfloat16))


def kernel(x, norm_gain, w_in, w_pool, b_pool, pool_scale, w_out, final_gain):
    batch, seq, d_model = x.shape
    depth = w_in.shape[0]
    assert d_model == D_MODEL and seq % ROW_TILE == 0 and 2 <= seq // MOBA_BLOCK <= LANES
    n_blocks = seq // MOBA_BLOCK
    cos_t, sin_t, onehot = _position_tables(seq)

    x2 = x.reshape(batch * seq, d_model)
    for l in range(depth):
        qt, k, vt, gate_attn, pool, kmean = _in_proj(
            x2, norm_gain[l][None, :], w_in[l], cos_t, sin_t, w_pool[l], b_pool[l][:, None, :],
            pool_scale[l][None, :], seq)
        shape3 = lambda t: t.reshape(batch, seq, t.shape[-1])
        per_block = lambda t: t.reshape(batch, n_blocks, HEAD_TILES, t.shape[-2], MOBA_BLOCK)
        attn = _moba_attention(per_block(qt), shape3(k), per_block(vt), onehot, kmean, shape3(gate_attn))
        x2 = _out_proj(x2, attn.reshape(batch * seq, ATTN_WIDTH), pool, w_out[l], final_gain[None, :],
                       apply_final_norm=(l == depth - 1))
    return x2.reshape(batch, seq, d_model)
```

```python
import functools

import numpy as np

import jax
import jax.numpy as jnp
from jax import lax
from jax.experimental import pallas as pl
from jax.experimental.pallas import tpu as pltpu

D_MODEL = 1024
ATTN_WIDTH = D_MODEL // 2
POOL_WIDTH = D_MODEL - ATTN_WIDTH
HEAD_DIM = 64
MOBA_BLOCK = 256
MOBA_TOPK = 3
POOL_WINDOWS = (2, 4, 8, 16)
POOL_GROUP = POOL_WIDTH // len(POOL_WINDOWS)
ROPE_THETA = 10000.0
EPS = 1e-6
IN_WIDTH = 4 * ATTN_WIDTH + 2 * POOL_WIDTH

LANES = 128
SUBLANES = 8
HEADS_PER_TILE = LANES // HEAD_DIM
HEAD_TILES = ATTN_WIDTH // LANES
ROW_TILE = 1024
RESIDUAL_SLOTS = 3
POOL_HALO = 16
MASK_BIAS = -1e30
KEY_UNROLL = 8
HALF_UNROLL = KEY_UNROLL // 2
WIDE_UNROLL = KEY_UNROLL * 2
BF16_ROWS = 16
VT_HEAD_ROWS = HEAD_DIM + BF16_ROWS
VT_ROWS = HEADS_PER_TILE * VT_HEAD_ROWS
Q_SCALE = HEAD_DIM ** -0.5 * 1.4426950408889634
VMEM_LIMIT = 56 * 1024 * 1024


def _silu(t):
    half = 0.5 * t
    return half + half * jnp.tanh(half)


def _rope(t, cos, sin_signed, first_half):
    swapped = jnp.where(first_half, pltpu.roll(t, LANES - HEAD_DIM // 2, axis=1),
                        pltpu.roll(t, HEAD_DIM // 2, axis=1))
    return t * cos + swapped * sin_signed


def _in_proj_kernel(seq_tiles, x_ref, gain_ref, w_ref, cos_ref, sin_ref, wpool_ref, bpool_ref, pscale_ref,
                    qt_ref, k_ref, vt_ref, ga_ref, pool_ref, kmean_ref, wbf_ref, ext_ref):
    @pl.when(pl.program_id(0) == 0)
    def _():
        wbf_ref[...] = w_ref[...].astype(wbf_ref.dtype)

    x = x_ref[...]
    inv = lax.rsqrt(jnp.mean(x * x, axis=-1, keepdims=True) + EPS)
    h = (x * inv * gain_ref[...]).astype(jnp.bfloat16)

    def proj(c):
        return jnp.dot(h, wbf_ref[:, c * ATTN_WIDTH:(c + 1) * ATTN_WIDTH],
                       preferred_element_type=jnp.float32)

    cos = cos_ref[...]
    sin = sin_ref[...]
    lane = lax.broadcasted_iota(jnp.int32, (ROW_TILE, LANES), 1)
    first_half = (lane % HEAD_DIM) < (HEAD_DIM // 2)

    q = proj(0)
    k = proj(1)
    v = proj(2)
    for t in range(HEAD_TILES):
        sl = slice(t * LANES, (t + 1) * LANES)
        qr = _rope(q[:, sl], cos, sin, first_half) * Q_SCALE
        kr = _rope(k[:, sl], cos, sin, first_half)
        k_ref[:, sl] = kr.astype(k_ref.dtype)
        for b in range(ROW_TILE // MOBA_BLOCK):
            blk_rows = slice(b * MOBA_BLOCK, (b + 1) * MOBA_BLOCK)
            kmean_ref[b, :, sl] = jnp.sum(kr[blk_rows], axis=0, keepdims=True) * (1.0 / MOBA_BLOCK)
            qt_ref[b, t] = qr[blk_rows].T.astype(qt_ref.dtype)
            v_t = v[blk_rows, sl].T.astype(vt_ref.dtype)
            for head in range(HEADS_PER_TILE):
                base = head * VT_HEAD_ROWS
                vt_ref[b, t, base:base + HEAD_DIM] = v_t[head * HEAD_DIM:(head + 1) * HEAD_DIM]
                vt_ref[b, t, base + HEAD_DIM:base + VT_HEAD_ROWS] = jnp.ones((BF16_ROWS, MOBA_BLOCK),
                                                                             vt_ref.dtype)
    ga_ref[...] = proj(3)

    tile_in_seq = pl.program_id(0) % seq_tiles

    @pl.when(pl.program_id(0) == 0)
    def _():
        ext_ref[...] = jnp.zeros_like(ext_ref)

    history = ext_ref[ROW_TILE:, :]
    ext_ref[:POOL_HALO, :] = jnp.where(tile_in_seq > 0, history, jnp.zeros_like(history))
    ext_ref[POOL_HALO:, :] = proj(4)
    pos = tile_in_seq * ROW_TILE + lax.broadcasted_iota(jnp.int32, (ROW_TILE, 1), 0)
    gate_pool = _silu(proj(5))
    for g, window in enumerate(POOL_WINDOWS):
        sl = slice(g * POOL_GROUP, (g + 1) * POOL_GROUP)
        total = ext_ref[:, sl]
        span = 1
        while span < window:
            total = total + pltpu.roll(total, span, axis=0)
            span *= 2
        total = total[POOL_HALO:]
        count = jnp.minimum(pos + 1, window).astype(jnp.float32)
        pooled = total / count - ext_ref[POOL_HALO:, sl]
        y = jnp.dot(pooled.astype(jnp.bfloat16), wpool_ref[g].astype(jnp.bfloat16),
                    preferred_element_type=jnp.float32)
        y = (y + bpool_ref[g]) * pscale_ref[:, sl]
        pool_ref[:, sl] = (y * gate_pool[:, sl]).astype(pool_ref.dtype)


def _in_proj(x2, gain, w_in, cos_t, sin_t, w_pool, b_pool, pool_scale, seq):
    rows = x2.shape[0]
    n_tiles = rows // ROW_TILE
    seq_tiles = seq // ROW_TILE
    row_spec = lambda w: pl.BlockSpec((ROW_TILE, w), lambda i: (i, 0))
    tab_spec = pl.BlockSpec((ROW_TILE, LANES), lambda i: (i % seq_tiles, 0))
    blocks_per_tile = ROW_TILE // MOBA_BLOCK
    n_blocks = rows // MOBA_BLOCK
    transposed = lambda r: jax.ShapeDtypeStruct((n_blocks, HEAD_TILES, r, MOBA_BLOCK), jnp.bfloat16)
    transposed_spec = lambda r: pl.BlockSpec((blocks_per_tile, HEAD_TILES, r, MOBA_BLOCK),
                                             lambda i: (i, 0, 0, 0))
    out_shape = (
        transposed(LANES),
        jax.ShapeDtypeStruct((rows, ATTN_WIDTH), jnp.bfloat16),
        transposed(VT_ROWS),
        jax.ShapeDtypeStruct((rows, ATTN_WIDTH), jnp.float32),
        jax.ShapeDtypeStruct((rows, POOL_WIDTH), jnp.bfloat16),
        jax.ShapeDtypeStruct((n_blocks, 1, ATTN_WIDTH), jnp.float32),
    )
    return pl.pallas_call(
        functools.partial(_in_proj_kernel, seq_tiles),
        grid=(n_tiles,),
        in_specs=[
            row_spec(D_MODEL),
            pl.BlockSpec((1, D_MODEL), lambda i: (0, 0)),
            pl.BlockSpec((D_MODEL, IN_WIDTH), lambda i: (0, 0), pipeline_mode=pl.Buffered(1)),
            tab_spec, tab_spec,
            pl.BlockSpec((len(POOL_WINDOWS), POOL_GROUP, POOL_GROUP), lambda i: (0, 0, 0)),
            pl.BlockSpec((len(POOL_WINDOWS), 1, POOL_GROUP), lambda i: (0, 0, 0)),
            pl.BlockSpec((1, POOL_WIDTH), lambda i: (0, 0)),
        ],
        out_specs=(
            transposed_spec(LANES), row_spec(ATTN_WIDTH), transposed_spec(VT_ROWS),
            row_spec(ATTN_WIDTH), row_spec(POOL_WIDTH),
            pl.BlockSpec((blocks_per_tile, 1, ATTN_WIDTH), lambda i: (i, 0, 0)),
        ),
        out_shape=out_shape,
        scratch_shapes=[pltpu.VMEM((D_MODEL, IN_WIDTH), jnp.bfloat16),
                        pltpu.VMEM((POOL_HALO + ROW_TILE, POOL_WIDTH), jnp.float32)],
        compiler_params=pltpu.CompilerParams(
            dimension_semantics=("arbitrary",), vmem_limit_bytes=VMEM_LIMIT),
        name="in_proj",
    )(x2, gain, w_in, cos_t, sin_t, w_pool, b_pool, pool_scale)


def _select_bias_t(gate_t, n_past):
    slot = lax.broadcasted_iota(jnp.int32, gate_t.shape, 0)
    slot_f = slot.astype(jnp.float32)
    neg_inf = jnp.float32(-jnp.inf)
    g = jnp.where(slot < n_past, gate_t, neg_inf)
    bias = jnp.full(gate_t.shape, MASK_BIAS, dtype=jnp.float32)
    for _ in range(MOBA_TOPK):
        best = jnp.max(g, axis=0, keepdims=True)
        first = jnp.min(jnp.where(g == best, slot_f, float(gate_t.shape[0])), axis=0, keepdims=True)
        pick = (slot_f == first) & (best > neg_inf)
        bias = jnp.where(pick, 0.0, bias)
        g = jnp.where(pick, neg_inf, g)
    return bias


def _visited_tiles(n_past):
    rest = n_past % KEY_UNROLL
    whole = n_past - rest
    if isinstance(n_past, int):
        return whole + (0 if rest == 0 else HALF_UNROLL if rest <= HALF_UNROLL else KEY_UNROLL)
    return whole + jnp.where(rest == 0, 0, jnp.where(rest <= HALF_UNROLL, HALF_UNROLL, KEY_UNROLL))


def _fold_rows(t, op):
    return op(t.reshape(t.shape[0] // SUBLANES, SUBLANES, t.shape[1]), axis=0)


def _moba_kernel(qt_ref, k_ref, vt_ref, onehot_ref, kmean_ref, ga_ref, o_ref, qaug_ref, s_ref, own_ref):
    blk = MOBA_BLOCK
    n_blocks = kmean_ref.shape[0]
    heads = range(HEADS_PER_TILE)

    feat = lax.broadcasted_iota(jnp.int32, (LANES, blk), 0)
    km = kmean_ref[:, 0, :]
    km_hi = km.astype(jnp.bfloat16)
    km_lo = (km - km_hi.astype(jnp.float32)).astype(jnp.bfloat16)
    key_pos = lax.broadcasted_iota(jnp.int32, (blk, blk), 0)
    qry_pos = lax.broadcasted_iota(jnp.int32, (blk, blk), 1)
    causal = key_pos <= qry_pos

    def block_rows(j):
        return pl.ds(pl.multiple_of(j * blk, blk), blk)

    def prepare(j):
        qt = qt_ref[j].astype(jnp.float32)
        k_own = k_ref[block_rows(j), :]
        m_init = []
        for h in heads:
            qh = jnp.where((feat // HEAD_DIM) == h, qt, 0.0).astype(jnp.bfloat16)
            gate_t = (jnp.dot(km_hi, qh, preferred_element_type=jnp.float32)
                      + jnp.dot(km_lo, qh, preferred_element_type=jnp.float32))
            qaug_ref[h, :LANES, :] = qh
            qaug_ref[h, LANES:LANES + n_blocks, :] = _select_bias_t(gate_t, j).astype(jnp.bfloat16)
            qaug_ref[h, LANES + n_blocks:, :] = jnp.zeros((LANES - n_blocks, blk), jnp.bfloat16)
            s = jnp.dot(k_own, qh, preferred_element_type=jnp.float32)
            s = jnp.where(causal, s, -jnp.inf)
            own_ref[h] = s
            m_init.append(_fold_rows(s, jnp.max))
        return tuple(m_init)

    def probs(s, m_fin):
        return jnp.exp2(s - m_fin).astype(jnp.bfloat16)

    def weighted_values(p, n, h):
        v_aug = vt_ref[n, h * VT_HEAD_ROWS:(h + 1) * VT_HEAD_ROWS, :]
        r = jnp.dot(v_aug, p, preferred_element_type=jnp.float32)
        return r[:HEAD_DIM], r[HEAD_DIM:HEAD_DIM + SUBLANES]

    def own_probs(m_fin):
        return tuple(probs(own_ref[h], m_fin[h]) for h in heads)

    def own_values(j, p_own):
        return tuple(weighted_values(p_own[h], j, h) for h in heads)

    def column_max(m_run):
        return tuple(jnp.max(m, axis=0, keepdims=True) for m in m_run)

    def emit(j, acc):
        out_t = jnp.concatenate([o / l[:1] for o, l in acc], axis=0)
        rows = block_rows(j)
        o_ref[rows, :] = (out_t.T * _silu(ga_ref[rows, :])).astype(o_ref.dtype)

    def group_body(finish_old, score_new, m_prev, unroll=KEY_UNROLL, first=0):
        def body(g, carry):
            m_run, acc = [list(c) for c in carry]
            for i in range(unroll):
                n = first + g * unroll + i
                if score_new:
                    k_aug = jnp.concatenate([k_ref[block_rows(n), :], onehot_ref[block_rows(n), :]], axis=1)
                for h in heads:
                    if finish_old:
                        o, l = weighted_values(probs(s_ref[h, n], m_prev[h]), n, h)
                        acc[h] = (acc[h][0] + o, acc[h][1] + l)
                    if score_new:
                        s = jnp.dot(k_aug, qaug_ref[h], preferred_element_type=jnp.float32)
                        s_ref[h, n] = s
                        m_run[h] = jnp.maximum(m_run[h], _fold_rows(s, jnp.max))
            return tuple(m_run), tuple(acc)
        return body

    m_fin0 = column_max(prepare(0))
    acc0 = own_values(0, own_probs(m_fin0))
    m_init1 = prepare(1)

    def step(t, carry):
        m_init, acc, m_prev = carry
        old_tiles = _visited_tiles(t - 1)
        wide = old_tiles // WIDE_UNROLL
        done = wide * WIDE_UNROLL
        whole = (old_tiles - done) // KEY_UNROLL
        half = (old_tiles - done - whole * KEY_UNROLL) // HALF_UNROLL
        extra = (_visited_tiles(t) - old_tiles) // HALF_UNROLL
        inner = (m_init, acc)
        inner = lax.fori_loop(0, wide, group_body(True, True, m_prev, WIDE_UNROLL), inner)
        inner = lax.fori_loop(0, whole, group_body(True, True, m_prev, KEY_UNROLL, done), inner)
        inner = lax.fori_loop(0, half, group_body(True, True, m_prev, HALF_UNROLL,
                                                  done + whole * KEY_UNROLL), inner)
        inner = lax.fori_loop(0, extra, group_body(False, True, m_prev, HALF_UNROLL, old_tiles), inner)
        m_run, acc = inner
        m_fin = column_max(m_run)
        emit(t - 1, acc)
        p_own = own_probs(m_fin)
        m_next = prepare(jnp.minimum(t + 1, n_blocks - 1))
        acc_next = own_values(t, p_own)
        return m_next, acc_next, m_fin

    _, acc, m_prev = lax.fori_loop(1, n_blocks, step, (m_init1, acc0, m_fin0))

    last = n_blocks - 1
    last_tiles = _visited_tiles(last)
    whole = last_tiles // KEY_UNROLL
    unused_max = tuple(jnp.zeros((SUBLANES, blk), jnp.float32) for _ in heads)
    inner = lax.fori_loop(0, whole, group_body(True, False, m_prev), (unused_max, acc))
    if last_tiles % KEY_UNROLL:
        inner = group_body(True, False, m_prev, HALF_UNROLL, whole * KEY_UNROLL)(0, inner)
    emit(last, inner[1])


def _moba_attention(qt, k, vt, onehot, kmean, gate_attn):
    batch, seq, _ = k.shape
    n_blocks = seq // MOBA_BLOCK
    slots = _visited_tiles(n_blocks - 1)
    per_block = lambda r: pl.BlockSpec((None, n_blocks, None, r, MOBA_BLOCK), lambda b, hp: (b, 0, hp, 0, 0))
    per_row = pl.BlockSpec((None, seq, LANES), lambda b, hp: (b, 0, hp))
    return pl.pallas_call(
        _moba_kernel,
        grid=(batch, HEAD_TILES),
        in_specs=[
            per_block(LANES),
            per_row,
            per_block(VT_ROWS),
            pl.BlockSpec((seq, LANES), lambda b, hp: (0, 0)),
            pl.BlockSpec((n_blocks, 1, LANES), lambda b, hp: (b, 0, hp)),
            per_row,
        ],
        out_specs=per_row,
        out_shape=jax.ShapeDtypeStruct((batch, seq, ATTN_WIDTH), jnp.bfloat16),
        scratch_shapes=[
            pltpu.VMEM((HEADS_PER_TILE, 2 * LANES, MOBA_BLOCK), jnp.bfloat16),
            pltpu.VMEM((HEADS_PER_TILE, slots, MOBA_BLOCK, MOBA_BLOCK), jnp.float32),
            pltpu.VMEM((HEADS_PER_TILE, MOBA_BLOCK, MOBA_BLOCK), jnp.float32),
        ],
        compiler_params=pltpu.CompilerParams(
            dimension_semantics=("arbitrary", "arbitrary"), vmem_limit_bytes=VMEM_LIMIT),
        name="moba_attention",
    )(qt, k, vt, onehot, kmean, gate_attn)


def _residual_copy(x_hbm, xring_ref, sem, step):
    slot = step % RESIDUAL_SLOTS
    return pltpu.make_async_copy(x_hbm.at[pl.ds(step * ROW_TILE, ROW_TILE), :], xring_ref.at[slot], sem.at[slot])


def _out_proj_kernel(apply_final_norm, x_hbm, attn_ref, pool_ref, wout_ref, fgain_ref, o_ref,
                     wbf_ref, xring_ref, sem):
    step = pl.program_id(0)
    n_steps = pl.num_programs(0)

    @pl.when(step == 0)
    def _():
        for first in range(RESIDUAL_SLOTS - 1):
            _residual_copy(x_hbm, xring_ref, sem, first).start()
        wbf_ref[...] = wout_ref[...].astype(wbf_ref.dtype)

    @pl.when(step + RESIDUAL_SLOTS - 1 < n_steps)
    def _():
        _residual_copy(x_hbm, xring_ref, sem, step + RESIDUAL_SLOTS - 1).start()

    _residual_copy(x_hbm, xring_ref, sem, step).wait()

    mixed = jnp.concatenate([attn_ref[...], pool_ref[...]], axis=1)
    y = xring_ref[step % RESIDUAL_SLOTS] + jnp.dot(mixed, wbf_ref[...], preferred_element_type=jnp.float32)
    if apply_final_norm:
        inv = lax.rsqrt(jnp.mean(y * y, axis=-1, keepdims=True) + EPS)
        y = y * inv * fgain_ref[...]
    o_ref[...] = y


def _out_proj(x2, attn, pool, w_out, final_gain, apply_final_norm):
    rows = x2.shape[0]
    assert rows // ROW_TILE >= RESIDUAL_SLOTS - 1
    row_spec = lambda w: pl.BlockSpec((ROW_TILE, w), lambda i: (i, 0))
    return pl.pallas_call(
        functools.partial(_out_proj_kernel, apply_final_norm),
        grid=(rows // ROW_TILE,),
        in_specs=[
            pl.BlockSpec(memory_space=pl.ANY),
            row_spec(ATTN_WIDTH), row_spec(POOL_WIDTH),
            pl.BlockSpec((D_MODEL, D_MODEL), lambda i: (0, 0), pipeline_mode=pl.Buffered(1)),
            pl.BlockSpec((1, D_MODEL), lambda i: (0, 0)),
        ],
        out_specs=row_spec(D_MODEL),
        out_shape=jax.ShapeDtypeStruct((rows, D_MODEL), jnp.float32),
        scratch_shapes=[pltpu.VMEM((D_MODEL, D_MODEL), jnp.bfloat16),
                        pltpu.VMEM((RESIDUAL_SLOTS, ROW_TILE, D_MODEL), jnp.float32),
                        pltpu.SemaphoreType.DMA((RESIDUAL_SLOTS,))],
        compiler_params=pltpu.CompilerParams(
            dimension_semantics=("arbitrary",), vmem_limit_bytes=VMEM_LIMIT),
        name="out_proj",
    )(x2, attn, pool, w_out, final_gain)


def _position_tables(seq):
    pos = np.arange(seq, dtype=np.float64)
    inv_freq = 1.0 / (ROPE_THETA ** (np.arange(0, HEAD_DIM, 2, dtype=np.float64) / HEAD_DIM))
    ang = pos[:, None] * inv_freq[None, :]
    cos, sin = np.cos(ang), np.sin(ang)
    cos_t = np.tile(cos, (1, LANES // (HEAD_DIM // 2)))
    sin_t = np.tile(np.concatenate([-sin, sin], axis=1), (1, HEADS_PER_TILE))
    onehot = np.arange(seq)[:, None] // MOBA_BLOCK == np.arange(LANES)[None, :]
    return (jnp.asarray(cos_t, jnp.float32), jnp.asarray(sin_t, jnp.float32),
            jnp.asarray(onehot, jnp.bfloat16))


def kernel(x, norm_gain, w_in, w_pool, b_pool, pool_scale, w_out, final_gain):
    batch, seq, d_model = x.shape
    depth = w_in.shape[0]
    assert d_model == D_MODEL and seq % ROW_TILE == 0 and 2 <= seq // MOBA_BLOCK <= LANES
    n_blocks = seq // MOBA_BLOCK
    cos_t, sin_t, onehot = _position_tables(seq)

    x2 = x.reshape(batch * seq, d_model)
    for l in range(depth):
        qt, k, vt, gate_attn, pool, kmean = _in_proj(
            x2, norm_gain[l][None, :], w_in[l], cos_t, sin_t, w_pool[l], b_pool[l][:, None, :],
            pool_scale[l][None, :], seq)
        shape3 = lambda t: t.reshape(batch, seq, t.shape[-1])
        per_block = lambda t: t.reshape(batch, n_blocks, HEAD_TILES, t.shape[-2], MOBA_BLOCK)
        attn = _moba_attention(per_block(qt), shape3(k), per_block(vt), onehot, kmean, shape3(gate_attn))
        x2 = _out_proj(x2, attn.reshape(batch * seq, ATTN_WIDTH), pool, w_out[l], final_gain[None, :],
                       apply_final_norm=(l == depth - 1))
    return x2.reshape(batch, seq, d_model)
```

```python
import functools

import numpy as np

import jax
import jax.numpy as jnp
from jax import lax
from jax.experimental import pallas as pl
from jax.experimental.pallas import tpu as pltpu

D_MODEL = 1024
ATTN_WIDTH = D_MODEL // 2
POOL_WIDTH = D_MODEL - ATTN_WIDTH
HEAD_DIM = 64
MOBA_BLOCK = 256
MOBA_TOPK = 3
POOL_WINDOWS = (2, 4, 8, 16)
POOL_GROUP = POOL_WIDTH // len(POOL_WINDOWS)
ROPE_THETA = 10000.0
EPS = 1e-6
IN_WIDTH = 4 * ATTN_WIDTH + 2 * POOL_WIDTH

LANES = 128
SUBLANES = 8
HEADS_PER_TILE = LANES // HEAD_DIM
HEAD_TILES = ATTN_WIDTH // LANES
ROW_TILE = 1024
RESIDUAL_SLOTS = 3
POOL_HALO = 16
MASK_BIAS = -1e30
KEY_UNROLL = 8
HALF_UNROLL = KEY_UNROLL // 2
WIDE_UNROLL = KEY_UNROLL * 2
BF16_ROWS = 16
VT_HEAD_ROWS = HEAD_DIM + BF16_ROWS
VT_ROWS = HEADS_PER_TILE * VT_HEAD_ROWS
Q_SCALE = HEAD_DIM ** -0.5 * 1.4426950408889634
VMEM_LIMIT = 56 * 1024 * 1024


def _silu(t):
    half = 0.5 * t
    return half + half * jnp.tanh(half)


def _rope(t, cos, sin_signed, first_half):
    swapped = jnp.where(first_half, pltpu.roll(t, LANES - HEAD_DIM // 2, axis=1),
                        pltpu.roll(t, HEAD_DIM // 2, axis=1))
    return t * cos + swapped * sin_signed


def _in_proj_kernel(seq_tiles, x_ref, gain_ref, w_ref, cos_ref, sin_ref, wpool_ref, bpool_ref, pscale_ref,
                    qt_ref, k_ref, vt_ref, ga_ref, pool_ref, kmean_ref, wbf_ref, ext_ref):
    @pl.when(pl.program_id(0) == 0)
    def _():
        wbf_ref[...] = w_ref[...].astype(wbf_ref.dtype)

    x = x_ref[...]
    inv = lax.rsqrt(jnp.mean(x * x, axis=-1, keepdims=True) + EPS)
    h = (x * inv * gain_ref[...]).astype(jnp.bfloat16)

    def proj(c):
        return jnp.dot(h, wbf_ref[:, c * ATTN_WIDTH:(c + 1) * ATTN_WIDTH],
                       preferred_element_type=jnp.float32)

    cos = cos_ref[...]
    sin = sin_ref[...]
    lane = lax.broadcasted_iota(jnp.int32, (ROW_TILE, LANES), 1)
    first_half = (lane % HEAD_DIM) < (HEAD_DIM // 2)

    q = proj(0)
    k = proj(1)
    v = proj(2)
    for t in range(HEAD_TILES):
        sl = slice(t * LANES, (t + 1) * LANES)
        qr = _rope(q[:, sl], cos, sin, first_half) * Q_SCALE
        kr = _rope(k[:, sl], cos, sin, first_half)
        k_ref[:, sl] = kr.astype(k_ref.dtype)
        for b in range(ROW_TILE // MOBA_BLOCK):
            blk_rows = slice(b * MOBA_BLOCK, (b + 1) * MOBA_BLOCK)
            kmean_ref[b, :, sl] = jnp.sum(kr[blk_rows], axis=0, keepdims=True) * (1.0 / MOBA_BLOCK)
            qt_ref[b, t] = qr[blk_rows].T.astype(qt_ref.dtype)
            v_t = v[blk_rows, sl].T.astype(vt_ref.dtype)
            for head in range(HEADS_PER_TILE):
                base = head * VT_HEAD_ROWS
                vt_ref[b, t, base:base + HEAD_DIM] = v_t[head * HEAD_DIM:(head + 1) * HEAD_DIM]
                vt_ref[b, t, base + HEAD_DIM:base + VT_HEAD_ROWS] = jnp.ones((BF16_ROWS, MOBA_BLOCK),
                                                                             vt_ref.dtype)
    ga_ref[...] = proj(3)

    tile_in_seq = pl.program_id(0) % seq_tiles

    @pl.when(pl.program_id(0) == 0)
    def _():
        ext_ref[...] = jnp.zeros_like(ext_ref)

    history = ext_ref[ROW_TILE:, :]
    ext_ref[:POOL_HALO, :] = jnp.where(tile_in_seq > 0, history, jnp.zeros_like(history))
    ext_ref[POOL_HALO:, :] = proj(4)
    pos = tile_in_seq * ROW_TILE + lax.broadcasted_iota(jnp.int32, (ROW_TILE, 1), 0)
    gate_pool = _silu(proj(5))
    for g, window in enumerate(POOL_WINDOWS):
        sl = slice(g * POOL_GROUP, (g + 1) * POOL_GROUP)
        total = ext_ref[:, sl]
        span = 1
        while span < window:
            total = total + pltpu.roll(total, span, axis=0)
            span *= 2
        total = total[POOL_HALO:]
        count = jnp.minimum(pos + 1, window).astype(jnp.float32)
        pooled = total / count - ext_ref[POOL_HALO:, sl]
        y = jnp.dot(pooled.astype(jnp.bfloat16), wpool_ref[g].astype(jnp.bfloat16),
                    preferred_element_type=jnp.float32)
        y = (y + bpool_ref[g]) * pscale_ref[:, sl]
        pool_ref[:, sl] = (y * gate_pool[:, sl]).astype(pool_ref.dtype)


def _in_proj(x2, gain, w_in, cos_t, sin_t, w_pool, b_pool, pool_scale, seq):
    rows = x2.shape[0]
    n_tiles = rows // ROW_TILE
    seq_tiles = seq // ROW_TILE
    row_spec = lambda w: pl.BlockSpec((ROW_TILE, w), lambda i: (i, 0))
    tab_spec = pl.BlockSpec((ROW_TILE, LANES), lambda i: (i % seq_tiles, 0))
    blocks_per_tile = ROW_TILE // MOBA_BLOCK
    n_blocks = rows // MOBA_BLOCK
    transposed = lambda r: jax.ShapeDtypeStruct((n_blocks, HEAD_TILES, r, MOBA_BLOCK), jnp.bfloat16)
    transposed_spec = lambda r: pl.BlockSpec((blocks_per_tile, HEAD_TILES, r, MOBA_BLOCK),
                                             lambda i: (i, 0, 0, 0))
    out_shape = (
        transposed(LANES),
        jax.ShapeDtypeStruct((rows, ATTN_WIDTH), jnp.bfloat16),
        transposed(VT_ROWS),
        jax.ShapeDtypeStruct((rows, ATTN_WIDTH), jnp.float32),
        jax.ShapeDtypeStruct((rows, POOL_WIDTH), jnp.bfloat16),
        jax.ShapeDtypeStruct((n_blocks, 1, ATTN_WIDTH), jnp.float32),
    )
    return pl.pallas_call(
        functools.partial(_in_proj_kernel, seq_tiles),
        grid=(n_tiles,),
        in_specs=[
            row_spec(D_MODEL),
            pl.BlockSpec((1, D_MODEL), lambda i: (0, 0)),
            pl.BlockSpec((D_MODEL, IN_WIDTH), lambda i: (0, 0), pipeline_mode=pl.Buffered(1)),
            tab_spec, tab_spec,
            pl.BlockSpec((len(POOL_WINDOWS), POOL_GROUP, POOL_GROUP), lambda i: (0, 0, 0)),
            pl.BlockSpec((len(POOL_WINDOWS), 1, POOL_GROUP), lambda i: (0, 0, 0)),
            pl.BlockSpec((1, POOL_WIDTH), lambda i: (0, 0)),
        ],
        out_specs=(
            transposed_spec(LANES), row_spec(ATTN_WIDTH), transposed_spec(VT_ROWS),
            row_spec(ATTN_WIDTH), row_spec(POOL_WIDTH),
            pl.BlockSpec((blocks_per_tile, 1, ATTN_WIDTH), lambda i: (i, 0, 0)),
        ),
        out_shape=out_shape,
        scratch_shapes=[pltpu.VMEM((D_MODEL, IN_WIDTH), jnp.bfloat16),
                        pltpu.VMEM((POOL_HALO + ROW_TILE, POOL_WIDTH), jnp.float32)],
        compiler_params=pltpu.CompilerParams(
            dimension_semantics=("arbitrary",), vmem_limit_bytes=VMEM_LIMIT),
        name="in_proj",
    )(x2, gain, w_in, cos_t, sin_t, w_pool, b_pool, pool_scale)


def _select_bias_t(gate_t, n_past):
    slot = lax.broadcasted_iota(jnp.int32, gate_t.shape, 0)
    slot_f = slot.astype(jnp.float32)
    neg_inf = jnp.float32(-jnp.inf)
    g = jnp.where(slot < n_past, gate_t, neg_inf)
    bias = jnp.full(gate_t.shape, MASK_BIAS, dtype=jnp.float32)
    for _ in range(MOBA_TOPK):
        best = jnp.max(g, axis=0, keepdims=True)
        first = jnp.min(jnp.where(g == best, slot_f, float(gate_t.shape[0])), axis=0, keepdims=True)
        pick = (slot_f == first) & (best > neg_inf)
        bias = jnp.where(pick, 0.0, bias)
        g = jnp.where(pick, neg_inf, g)
    return bias


def _visited_tiles(n_past):
    rest = n_past % KEY_UNROLL
    whole = n_past - rest
    if isinstance(n_past, int):
        return whole + (0 if rest == 0 else HALF_UNROLL if rest <= HALF_UNROLL else KEY_UNROLL)
    return whole + jnp.where(rest == 0, 0, jnp.where(rest <= HALF_UNROLL, HALF_UNROLL, KEY_UNROLL))


def _fold_rows(t, op):
    return op(t.reshape(t.shape[0] // SUBLANES, SUBLANES, t.shape[1]), axis=0)


def _moba_kernel(qt_ref, k_ref, vt_ref, onehot_ref, kmean_ref, ga_ref, o_ref, qaug_ref, s_ref, own_ref):
    blk = MOBA_BLOCK
    n_blocks = kmean_ref.shape[0]
    heads = range(HEADS_PER_TILE)

    feat = lax.broadcasted_iota(jnp.int32, (LANES, blk), 0)
    km = kmean_ref[:, 0, :]
    km_hi = km.astype(jnp.bfloat16)
    km_lo = (km - km_hi.astype(jnp.float32)).astype(jnp.bfloat16)
    key_pos = lax.broadcasted_iota(jnp.int32, (blk, blk), 0)
    qry_pos = lax.broadcasted_iota(jnp.int32, (blk, blk), 1)
    causal = key_pos <= qry_pos

    def block_rows(j):
        return pl.ds(pl.multiple_of(j * blk, blk), blk)

    def prepare(j):
        qt = qt_ref[j].astype(jnp.float32)
        k_own = k_ref[block_rows(j), :]
        m_init = []
        for h in heads:
            qh = jnp.where((feat // HEAD_DIM) == h, qt, 0.0).astype(jnp.bfloat16)
            gate_t = (jnp.dot(km_hi, qh, preferred_element_type=jnp.float32)
                      + jnp.dot(km_lo, qh, preferred_element_type=jnp.float32))
            qaug_ref[h, :LANES, :] = qh
            qaug_ref[h, LANES:LANES + n_blocks, :] = _select_bias_t(gate_t, j).astype(jnp.bfloat16)
            qaug_ref[h, LANES + n_blocks:, :] = jnp.zeros((LANES - n_blocks, blk), jnp.bfloat16)
            s = jnp.dot(k_own, qh, preferred_element_type=jnp.float32)
            s = jnp.where(causal, s, -jnp.inf)
            own_ref[h] = s
            m_init.append(_fold_rows(s, jnp.max))
        return tuple(m_init)

    def probs(s, m_fin):
        return jnp.exp2(s - m_fin).astype(jnp.bfloat16)

    def weighted_values(p, n, h):
        v_aug = vt_ref[n, h * VT_HEAD_ROWS:(h + 1) * VT_HEAD_ROWS, :]
        r = jnp.dot(v_aug, p, preferred_element_type=jnp.float32)
        return r[:HEAD_DIM], r[HEAD_DIM:HEAD_DIM + SUBLANES]

    def own_probs(m_fin):
        return tuple(probs(own_ref[h], m_fin[h]) for h in heads)

    def own_values(j, p_own):
        return tuple(weighted_values(p_own[h], j, h) for h in heads)

    def column_max(m_run):
        return tuple(jnp.max(m, axis=0, keepdims=True) for m in m_run)

    def emit(j, acc):
        out_t = jnp.concatenate([o / l[:1] for o, l in acc], axis=0)
        rows = block_rows(j)
        o_ref[rows, :] = (out_t.T * _silu(ga_ref[rows, :])).astype(o_ref.dtype)

    def group_body(finish_old, score_new, m_prev, unroll=KEY_UNROLL, first=0):
        def body(g, carry):
            m_run, acc = [list(c) for c in carry]
            for i in range(unroll):
                n = first + g * unroll + i
                if score_new:
                    k_aug = jnp.concatenate([k_ref[block_rows(n), :], onehot_ref[block_rows(n), :]], axis=1)
                for h in heads:
                    if finish_old:
                        o, l = weighted_values(probs(s_ref[h, n], m_prev[h]), n, h)
                        acc[h] = (acc[h][0] + o, acc[h][1] + l)
                    if score_new:
                        s = jnp.dot(k_aug, qaug_ref[h], preferred_element_type=jnp.float32)
                        s_ref[h, n] = s
                        m_run[h] = jnp.maximum(m_run[h], _fold_rows(s, jnp.max))
            return tuple(m_run), tuple(acc)
        return body

    m_fin0 = column_max(prepare(0))
    acc0 = own_values(0, own_probs(m_fin0))
    m_init1 = prepare(1)

    def step(t, carry):
        m_init, acc, m_prev = carry
        old_tiles = _visited_tiles(t - 1)
        wide = old_tiles // WIDE_UNROLL
        done = wide * WIDE_UNROLL
        whole = (old_tiles - done) // KEY_UNROLL
        half = (old_tiles - done - whole * KEY_UNROLL) // HALF_UNROLL
        extra = (_visited_tiles(t) - old_tiles) // HALF_UNROLL
        inner = (m_init, acc)
        inner = lax.fori_loop(0, wide, group_body(True, True, m_prev, WIDE_UNROLL), inner)
        inner = lax.fori_loop(0, whole, group_body(True, True, m_prev, KEY_UNROLL, done), inner)
        inner = lax.fori_loop(0, half, group_body(True, True, m_prev, HALF_UNROLL,
                                                  done + whole * KEY_UNROLL), inner)
        inner = lax.fori_loop(0, extra, group_body(False, True, m_prev, HALF_UNROLL, old_tiles), inner)
        m_run, acc = inner
        m_fin = column_max(m_run)
        emit(t - 1, acc)
        p_own = own_probs(m_fin)
        m_next = prepare(jnp.minimum(t + 1, n_blocks - 1))
        acc_next = own_values(t, p_own)
        return m_next, acc_next, m_fin

    _, acc, m_prev = lax.fori_loop(1, n_blocks, step, (m_init1, acc0, m_fin0))

    last = n_blocks - 1
    last_tiles = _visited_tiles(last)
    whole = last_tiles // KEY_UNROLL
    unused_max = tuple(jnp.zeros((SUBLANES, blk), jnp.float32) for _ in heads)
    inner = lax.fori_loop(0, whole, group_body(True, False, m_prev), (unused_max, acc))
    if last_tiles % KEY_UNROLL:
        inner = group_body(True, False, m_prev, HALF_UNROLL, whole * KEY_UNROLL)(0, inner)
    emit(last, inner[1])


def _moba_attention(qt, k, vt, onehot, kmean, gate_attn):
    batch, seq, _ = k.shape
    n_blocks = seq // MOBA_BLOCK
    slots = _visited_tiles(n_blocks - 1)
    per_block = lambda r: pl.BlockSpec((None, n_blocks, None, r, MOBA_BLOCK), lambda b, hp: (b, 0, hp, 0, 0))
    per_row = pl.BlockSpec((None, seq, LANES), lambda b, hp: (b, 0, hp))
    return pl.pallas_call(
        _moba_kernel,
        grid=(batch, HEAD_TILES),
        in_specs=[
            per_block(LANES),
            per_row,
            per_block(VT_ROWS),
            pl.BlockSpec((seq, LANES), lambda b, hp: (0, 0), pipeline_mode=pl.Buffered(1)),
            pl.BlockSpec((n_blocks, 1, LANES), lambda b, hp: (b, 0, hp)),
            per_row,
        ],
        out_specs=per_row,
        out_shape=jax.ShapeDtypeStruct((batch, seq, ATTN_WIDTH), jnp.bfloat16),
        scratch_shapes=[
            pltpu.VMEM((HEADS_PER_TILE, 2 * LANES, MOBA_BLOCK), jnp.bfloat16),
            pltpu.VMEM((HEADS_PER_TILE, slots, MOBA_BLOCK, MOBA_BLOCK), jnp.float32),
            pltpu.VMEM((HEADS_PER_TILE, MOBA_BLOCK, MOBA_BLOCK), jnp.float32),
        ],
        compiler_params=pltpu.CompilerParams(
            dimension_semantics=("arbitrary", "arbitrary"), vmem_limit_bytes=VMEM_LIMIT),
        name="moba_attention",
    )(qt, k, vt, onehot, kmean, gate_attn)


def _residual_copy(x_hbm, xring_ref, sem, step):
    slot = step % RESIDUAL_SLOTS
    return pltpu.make_async_copy(x_hbm.at[pl.ds(step * ROW_TILE, ROW_TILE), :], xring_ref.at[slot], sem.at[slot])


def _out_proj_kernel(apply_final_norm, x_hbm, attn_ref, pool_ref, wout_ref, fgain_ref, o_ref,
                     wbf_ref, xring_ref, sem):
    step = pl.program_id(0)
    n_steps = pl.num_programs(0)

    @pl.when(step == 0)
    def _():
        for first in range(RESIDUAL_SLOTS - 1):
            _residual_copy(x_hbm, xring_ref, sem, first).start(priority=1)
        wbf_ref[...] = wout_ref[...].astype(wbf_ref.dtype)

    @pl.when(step + RESIDUAL_SLOTS - 1 < n_steps)
    def _():
        _residual_copy(x_hbm, xring_ref, sem, step + RESIDUAL_SLOTS - 1).start(priority=1)

    _residual_copy(x_hbm, xring_ref, sem, step).wait()

    mixed = jnp.concatenate([attn_ref[...], pool_ref[...]], axis=1)
    y = xring_ref[step % RESIDUAL_SLOTS] + jnp.dot(mixed, wbf_ref[...], preferred_element_type=jnp.float32)
    if apply_final_norm:
        inv = lax.rsqrt(jnp.mean(y * y, axis=-1, keepdims=True) + EPS)
        y = y * inv * fgain_ref[...]
    o_ref[...] = y


def _out_proj(x2, attn, pool, w_out, final_gain, apply_final_norm):
    rows = x2.shape[0]
    assert rows // ROW_TILE >= RESIDUAL_SLOTS - 1
    row_spec = lambda w: pl.BlockSpec((ROW_TILE, w), lambda i: (i, 0))
    return pl.pallas_call(
        functools.partial(_out_proj_kernel, apply_final_norm),
        grid=(rows // ROW_TILE,),
        in_specs=[
            pl.BlockSpec(memory_space=pl.ANY),
            row_spec(ATTN_WIDTH), row_spec(POOL_WIDTH),
            pl.BlockSpec((D_MODEL, D_MODEL), lambda i: (0, 0), pipeline_mode=pl.Buffered(1)),
            pl.BlockSpec((1, D_MODEL), lambda i: (0, 0)),
        ],
        out_specs=row_spec(D_MODEL),
        out_shape=jax.ShapeDtypeStruct((rows, D_MODEL), jnp.float32),
        scratch_shapes=[pltpu.VMEM((D_MODEL, D_MODEL), jnp.bfloat16),
                        pltpu.VMEM((RESIDUAL_SLOTS, ROW_TILE, D_MODEL), jnp.float32),
                        pltpu.SemaphoreType.DMA((RESIDUAL_SLOTS,))],
        compiler_params=pltpu.CompilerParams(
            dimension_semantics=("arbitrary",), vmem_limit_bytes=VMEM_LIMIT),
        name="out_proj",
    )(x2, attn, pool, w_out, final_gain)


def _position_tables(seq):
    pos = np.arange(seq, dtype=np.float64)
    inv_freq = 1.0 / (ROPE_THETA ** (np.arange(0, HEAD_DIM, 2, dtype=np.float64) / HEAD_DIM))
    ang = pos[:, None] * inv_freq[None, :]
    cos, sin = np.cos(ang), np.sin(ang)
    cos_t = np.tile(cos, (1, LANES // (HEAD_DIM // 2)))
    sin_t = np.tile(np.concatenate([-sin, sin], axis=1), (1, HEADS_PER_TILE))
    onehot = np.arange(seq)[:, None] // MOBA_BLOCK == np.arange(LANES)[None, :]
    return (jnp.asarray(cos_t, jnp.float32), jnp.asarray(sin_t, jnp.float32),
            jnp.asarray(onehot, jnp.bfloat16))


def kernel(x, norm_gain, w_in, w_pool, b_pool, pool_scale, w_out, final_gain):
    batch, seq, d_model = x.shape
    depth = w_in.shape[0]
    assert d_model == D_MODEL and seq % ROW_TILE == 0 and 2 <= seq // MOBA_BLOCK <= LANES
    n_blocks = seq // MOBA_BLOCK
    cos_t, sin_t, onehot = _position_tables(seq)

    x2 = x.reshape(batch * seq, d_model)
    for l in range(depth):
        qt, k, vt, gate_attn, pool, kmean = _in_proj(
            x2, norm_gain[l][None, :], w_in[l], cos_t, sin_t, w_pool[l], b_pool[l][:, None, :],
            pool_scale[l][None, :], seq)
        shape3 = lambda t: t.reshape(batch, seq, t.shape[-1])
        per_block = lambda t: t.reshape(batch, n_blocks, HEAD_TILES, t.shape[-2], MOBA_BLOCK)
        attn = _moba_attention(per_block(qt), shape3(k), per_block(vt), onehot, kmean, shape3(gate_attn))
        x2 = _out_proj(x2, attn.reshape(batch * seq, ATTN_WIDTH), pool, w_out[l], final_gain[None, :],
                       apply_final_norm=(l == depth - 1))
    return x2.reshape(batch, seq, d_model)
```
